```python
import jax, jax.numpy as jnp
from jax import lax
import numpy as np

D_MODEL = 1024
BATCH = 8
SEQ = 4096
DEPTH = 1
DEC_BATCH = 8
DEC_SEQ = 16
PAST_LEN = 1024

CHUNK = 64
N_META = 16
HEAD_DIM = 64
N_Q_HEADS = 16
N_KV_HEADS = 2
GQA_GROUP = N_Q_HEADS // N_KV_HEADS
WINDOW = 128
WIN_CHUNKS = WINDOW // CHUNK
ROPE_THETA = 10000.0
POOL_WINDOWS = (2, 4, 8, 16)
N_POOL_GROUPS = 4
POOL_WIDTH = D_MODEL // 2
POOL_GROUP_W = POOL_WIDTH // N_POOL_GROUPS
POOL_HIST = max(POOL_WINDOWS) - 1
Q_W = N_Q_HEADS * HEAD_DIM
KV_W = N_KV_HEADS * HEAD_DIM
GATE_W = 2 * D_MODEL
IN_W = Q_W + 2 * KV_W + POOL_WIDTH + GATE_W
D_FF = ((8 * D_MODEL // 3) + 127) // 128 * 128
CONV_W = 3
RMS_EPS = 1e-6

kernel_name = "hybrid_swa_pool_convffn_stream_step"


def rmsnorm(x, g):
    xf = x.astype(jnp.float32)
    r = lax.rsqrt(jnp.mean(xf * xf, axis=-1, keepdims=True) + RMS_EPS)
    return (xf * r).astype(x.dtype) * g


def rope(x, pos):
    half = HEAD_DIM // 2
    inv = ROPE_THETA ** (-jnp.arange(half, dtype=jnp.float32) / half)
    ang = pos.astype(jnp.float32)[:, None] * inv[None, :]
    cos = jnp.cos(ang)[None, :, None, :]
    sin = jnp.sin(ang)[None, :, None, :]
    xf = x.astype(jnp.float32)
    x1, x2 = xf[..., :half], xf[..., half:]
    return jnp.concatenate([x1 * cos - x2 * sin, x2 * cos + x1 * sin], axis=-1).astype(x.dtype)


def attend(q, k, v, sinks, mask):
    s = jnp.einsum('...qngd,...snd->...ngqs', q, k).astype(jnp.float32) * (HEAD_DIM ** -0.5)
    if mask is not None:
        s = jnp.where(mask, s, -1e30)
    sink = jnp.broadcast_to(sinks.astype(jnp.float32)[..., None, None], s.shape[:-1] + (1,))
    p = jax.nn.softmax(jnp.concatenate([s, sink], axis=-1), axis=-1)[..., :-1]
    return jnp.einsum('...ngqs,...snd->...qngd', p.astype(v.dtype), v)


def project_inputs(h, w_in, b_gate):
    z = h @ w_in
    q = z[..., :Q_W]
    k = z[..., Q_W:Q_W + KV_W]
    v = z[..., Q_W + KV_W:Q_W + 2 * KV_W]
    p = z[..., Q_W + 2 * KV_W:Q_W + 2 * KV_W + POOL_WIDTH]
    g = jax.nn.sigmoid((z[..., Q_W + 2 * KV_W + POOL_WIDTH:] + b_gate).astype(jnp.float32)).astype(h.dtype)
    return q, k, v, p, g[..., :D_MODEL], g[..., D_MODEL:]


def multiscale_pool(p, pos):
    bsz, s = p.shape[0], p.shape[1]
    pg = p.reshape(bsz, s, N_POOL_GROUPS, POOL_GROUP_W).astype(jnp.float32)
    cs = jnp.cumsum(pg, axis=1)
    cs0 = jnp.concatenate([jnp.zeros_like(cs[:, :1]), cs], axis=1)
    outs = []
    for gi, w in enumerate(POOL_WINDOWS):
        c = cs0[:, :, gi]
        prev = jnp.concatenate([jnp.zeros_like(c[:, :w - 1]), c[:, :s - w + 1]], axis=1)
        cnt = jnp.minimum(w, pos + 1).astype(jnp.float32)[None, :, None]
        outs.append((c[:, 1:] - prev) / cnt - pg[:, :, gi])
    return jnp.stack(outs, axis=2)


def mixer_residual(x, o_attn, pm, g_attn, g_pool, w_attn_o, w_pool_grp, pool_scale, w_pool_o, w_out):
    bsz, s = pm.shape[0], pm.shape[1]
    pool = jnp.einsum('bsgc,gcd->bsgd', pm.astype(x.dtype), w_pool_grp).reshape(bsz, s, POOL_WIDTH) * pool_scale
    mix = g_attn * (o_attn @ w_attn_o) + g_pool * (pool @ w_pool_o)
    return x + mix @ w_out


def channel_mixer(x1, conv_prefix, g_norm_ffn, w_up, conv_w, conv_b, w_down):
    up = rmsnorm(x1, g_norm_ffn) @ w_up
    up_ext = jnp.concatenate([conv_prefix, up], axis=1)
    t = up.shape[1]
    c = conv_b + up_ext[:, 0:t] * conv_w[0]
    for j in range(1, CONV_W):
        c = c + up_ext[:, j:j + t] * conv_w[j]
    gate, val = c[..., :D_FF], c[..., D_FF:]
    return x1 + (jax.nn.silu(gate) * val) @ w_down, up_ext[:, -(CONV_W - 1):]


def setup_inputs(seed: int = 0) -> dict:
    key = jax.random.key(seed)
    ks = jax.random.split(key, 24)
    n = jax.random.normal
    f32 = jnp.float32
    return {
        "x_prompt": n(ks[0], (BATCH, SEQ, D_MODEL), f32),
        "x_sample": n(ks[1], (DEC_BATCH, DEC_SEQ, D_MODEL), f32),
        "cache_swa_k": n(ks[2], (DEC_BATCH, WINDOW, N_KV_HEADS, HEAD_DIM), f32),
        "cache_swa_v": n(ks[3], (DEC_BATCH, WINDOW, N_KV_HEADS, HEAD_DIM), f32),
        "cache_meta_k": n(ks[4], (DEC_BATCH, N_META, N_KV_HEADS, HEAD_DIM), f32),
        "cache_meta_v": n(ks[5], (DEC_BATCH, N_META, N_KV_HEADS, HEAD_DIM), f32),
        "state_pool": n(ks[6], (DEC_BATCH, POOL_HIST, POOL_WIDTH), f32),
        "state_conv": n(ks[7], (DEC_BATCH, CONV_W - 1, 2 * D_FF), f32),
        "meta_tokens": n(ks[8], (N_META, D_MODEL), f32),
        "g_norm_mix": 1.0 + 0.05 * n(ks[9], (D_MODEL,), f32),
        "w_in": n(ks[10], (D_MODEL, IN_W), f32) * D_MODEL ** -0.5,
        "b_gate": 0.01 * n(ks[11], (GATE_W,), f32),
        "sinks": 0.5 * n(ks[12], (N_Q_HEADS,), f32),
        "w_attn_o": n(ks[13], (Q_W, D_MODEL), f32) * Q_W ** -0.5,
        "w_pool_grp": n(ks[14], (N_POOL_GROUPS, POOL_GROUP_W, POOL_GROUP_W), f32) * POOL_GROUP_W ** -0.5,
        "pool_scale": 1.0 + 0.1 * n(ks[15], (POOL_WIDTH,), f32),
        "w_pool_o": n(ks[16], (POOL_WIDTH, D_MODEL), f32) * POOL_WIDTH ** -0.5,
        "w_out": n(ks[17], (D_MODEL, D_MODEL), f32) * D_MODEL ** -0.5,
        "g_norm_ffn": 1.0 + 0.05 * n(ks[18], (D_MODEL,), f32),
        "w_up": n(ks[19], (D_MODEL, 2 * D_FF), f32) * D_MODEL ** -0.5,
        "conv_w": n(ks[20], (CONV_W, 2 * D_FF), f32) * CONV_W ** -0.5,
        "conv_b": 0.01 * n(ks[21], (2 * D_FF,), f32),
        "w_down": n(ks[22], (D_FF, D_MODEL), f32) * D_FF ** -0.5,
        "g_norm_final": 1.0 + 0.05 * n(ks[23], (D_MODEL,), f32),
    }


def reference(x_prompt, x_sample, cache_swa_k, cache_swa_v, cache_meta_k, cache_meta_v, state_pool, state_conv,
              meta_tokens, g_norm_mix, w_in, b_gate, sinks, w_attn_o, w_pool_grp, pool_scale, w_pool_o, w_out,
              g_norm_ffn, w_up, conv_w, conv_b, w_down, g_norm_final):
    dt = x_prompt.dtype
    sinks_r = sinks.reshape(N_KV_HEADS, GQA_GROUP)

    bsz, seq = x_prompt.shape[0], x_prompt.shape[1]
    length = N_META + seq
    nc = seq // CHUNK
    x = jnp.concatenate([jnp.broadcast_to(meta_tokens.astype(dt)[None], (bsz, N_META, D_MODEL)), x_prompt], axis=1)
    pos = jnp.arange(length, dtype=jnp.int32)
    h = rmsnorm(x, g_norm_mix)
    q, k, v, p, g_attn, g_pool = project_inputs(h, w_in, b_gate)
    q = rope(q.reshape(bsz, length, N_Q_HEADS, HEAD_DIM), pos).reshape(bsz, length, N_KV_HEADS, GQA_GROUP, HEAD_DIM)
    k = rope(k.reshape(bsz, length, N_KV_HEADS, HEAD_DIM), pos)
    v = v.reshape(bsz, length, N_KV_HEADS, HEAD_DIM)
    km, vm = k[:, :N_META], v[:, :N_META]
    o_meta = attend(q[:, :N_META], km, vm, sinks_r, None)
    qr = q[:, N_META:].reshape(bsz, nc, CHUNK, N_KV_HEADS, GQA_GROUP, HEAD_DIM)

    def band(t):
        tc = t.reshape(bsz, nc, CHUNK, N_KV_HEADS, HEAD_DIM)
        tp = jnp.pad(tc, ((0, 0), (WIN_CHUNKS, 0), (0, 0), (0, 0), (0, 0)))
        return jnp.concatenate([tp[:, j:j + nc] for j in range(WIN_CHUNKS + 1)], axis=2)

    meta_b = lambda t: jnp.broadcast_to(t[:, None], (bsz, nc, N_META, N_KV_HEADS, HEAD_DIM))
    kb = jnp.concatenate([meta_b(km), band(k[:, N_META:])], axis=2)
    vb = jnp.concatenate([meta_b(vm), band(v[:, N_META:])], axis=2)
    cidx = jnp.arange(nc)[:, None]
    jidx = jnp.arange((WIN_CHUNKS + 1) * CHUNK)[None, :]
    band_ok = (cidx - WIN_CHUNKS + jidx // CHUNK) >= 0
    mask = jnp.concatenate([jnp.ones((nc, N_META), dtype=bool), band_ok], axis=1)[:, None, None, None, :]
    o_real = attend(qr, kb, vb, sinks_r, mask)
    o_attn = jnp.concatenate([o_meta.reshape(bsz, N_META, Q_W), o_real.reshape(bsz, seq, Q_W)], axis=1)
    pm = multiscale_pool(p, pos)
    x1 = mixer_residual(x, o_attn, pm, g_attn, g_pool, w_attn_o, w_pool_grp, pool_scale, w_pool_o, w_out)
    conv_prefix = jnp.zeros((bsz, CONV_W - 1, 2 * D_FF), dtype=dt)
    x2, p_conv = channel_mixer(x1, conv_prefix, g_norm_ffn, w_up, conv_w, conv_b, w_down)
    y_prompt = rmsnorm(x2, g_norm_final)[:, N_META:]
    p_swa_k = k[:, -WINDOW:]
    p_swa_v = v[:, -WINDOW:]
    p_pool = p[:, -POOL_HIST:]

    dbsz, t = x_sample.shape[0], x_sample.shape[1]
    pos_s = N_META + PAST_LEN + jnp.arange(t, dtype=jnp.int32)
    hs = rmsnorm(x_sample, g_norm_mix)
    qs, ks_, vs, ps, ga_s, gp_s = project_inputs(hs, w_in, b_gate)
    qs = rope(qs.reshape(dbsz, t, N_Q_HEADS, HEAD_DIM), pos_s).reshape(dbsz, t, N_KV_HEADS, GQA_GROUP, HEAD_DIM)
    ks_ = rope(ks_.reshape(dbsz, t, N_KV_HEADS, HEAD_DIM), pos_s)
    vs = vs.reshape(dbsz, t, N_KV_HEADS, HEAD_DIM)
    k_all = jnp.concatenate([cache_meta_k, cache_swa_k, ks_], axis=1)
    v_all = jnp.concatenate([cache_meta_v, cache_swa_v, vs], axis=1)
    o_s = attend(qs, k_all, v_all, sinks_r, None).reshape(dbsz, t, Q_W)
    p_ext = jnp.concatenate([state_pool, ps], axis=1)
    pos_ext = N_META + PAST_LEN - POOL_HIST + jnp.arange(POOL_HIST + t, dtype=jnp.int32)
    pm_s = multiscale_pool(p_ext, pos_ext)[:, POOL_HIST:]
    x1s = mixer_residual(x_sample, o_s, pm_s, ga_s, gp_s, w_attn_o, w_pool_grp, pool_scale, w_pool_o, w_out)
    x2s, s_conv = channel_mixer(x1s, state_conv, g_norm_ffn, w_up, conv_w, conv_b, w_down)
    y_sample = rmsnorm(x2s, g_norm_final)
    s_pool = p_ext[:, -POOL_HIST:]

    return (y_prompt, y_sample, p_swa_k, p_swa_v, km, vm, p_pool, p_conv, ks_, vs, s_pool, s_conv)
```

```python
import functools

import jax
import jax.numpy as jnp
from jax import lax
from jax.experimental import pallas as pl
from jax.experimental.pallas import tpu as pltpu

D_MODEL = 1024
N_META = 16
CHUNK = 64
HEAD_DIM = 64
N_Q_HEADS = 16
N_KV_HEADS = 2
WINDOW = 128
ROPE_THETA = 10000.0
POOL_WINDOWS = (2, 4, 8, 16)
N_POOL_GROUPS = 4
POOL_WIDTH = D_MODEL // 2
POOL_GROUP_W = POOL_WIDTH // N_POOL_GROUPS
POOL_HIST = max(POOL_WINDOWS) - 1
Q_W = N_Q_HEADS * HEAD_DIM
KV_W = N_KV_HEADS * HEAD_DIM
GATE_W = 2 * D_MODEL
IN_W = Q_W + 2 * KV_W + POOL_WIDTH + GATE_W
D_FF = ((8 * D_MODEL // 3) + 127) // 128 * 128
CONV_W = 3
RMS_EPS = 1e-6
PAST_LEN = 1024

LANES = 128
SUBLANES = 8
Q_BLOCKS = Q_W // LANES
BLOCKS_PER_KV = Q_BLOCKS // N_KV_HEADS
NEG = -1e30

K_OFF = Q_W
V_OFF = Q_W + KV_W
P_OFF = Q_W + 2 * KV_W
G_OFF = P_OFF + POOL_WIDTH

F32 = jnp.float32
BF16 = jnp.bfloat16

VMEM_LIMIT = 56 * 1024 * 1024


def _rmsnorm(x, g):
    ms = jnp.mean(x * x, axis=-1, keepdims=True)
    return (x * lax.rsqrt(ms + RMS_EPS)) * g


def _dot(a, b):
    return jnp.dot(a, b, preferred_element_type=F32)


def _rope(x, cos, sin_signed):
    lane = lax.broadcasted_iota(jnp.int32, (x.shape[0], LANES), 1)
    first_half = (lane & (HEAD_DIM // 2)) == 0
    outs = []
    for b in range(x.shape[1] // LANES):
        xb = x[:, b * LANES:(b + 1) * LANES]
        partner = jnp.where(first_half, pltpu.roll(xb, LANES - HEAD_DIM // 2, 1), pltpu.roll(xb, HEAD_DIM // 2, 1))
        outs.append(xb * cos + partner * sin_signed)
    return outs[0] if len(outs) == 1 else jnp.concatenate(outs, axis=1)


def _kv_variants(kv):
    lane = lax.broadcasted_iota(jnp.int32, kv.shape, 1)
    low = lane < HEAD_DIM
    swapped = pltpu.roll(kv, HEAD_DIM, 1)
    zero = jnp.zeros_like(kv)
    return (
        (jnp.where(low, kv, zero).astype(BF16), jnp.where(low, zero, swapped).astype(BF16)),
        (jnp.where(low, swapped, zero).astype(BF16), jnp.where(low, zero, kv).astype(BF16)),
    )


def _attend_group(qg, kst, vst, tab):
    s = lax.dot_general(qg, kst, (((1,), (1,)), ((), ())), preferred_element_type=F32) + tab
    npad = kst.shape[0] // 2
    se, so = s[:, :npad], s[:, npad:]
    pe = jnp.exp(se - jnp.max(se, axis=1, keepdims=True))
    po = jnp.exp(so - jnp.max(so, axis=1, keepdims=True))
    le = jnp.sum(pe, axis=1, keepdims=True)
    lo = jnp.sum(po, axis=1, keepdims=True)
    p = jnp.concatenate([pe, po], axis=1).astype(BF16)
    o = _dot(p, vst)
    lane = lax.broadcasted_iota(jnp.int32, o.shape, 1)
    return o * jnp.where(lane < HEAD_DIM, 1.0 / le, 1.0 / lo)


def _pool_group(ext, g, cnt=None):
    w = POOL_WINDOWS[g]
    s = ext
    k = 1
    while k < w:
        s = s + pltpu.roll(s, k, 0)
        k *= 2
    cur = ext[N_META:]
    return s[N_META:] / (float(w) if cnt is None else cnt) - cur


def _pool_project(pm, wgrp_ref, pscale):
    outs = []
    for g in range(N_POOL_GROUPS):
        sl = slice(g * POOL_GROUP_W, (g + 1) * POOL_GROUP_W)
        outs.append(_dot(pm[:, sl].astype(BF16), wgrp_ref[g]))
    return (jnp.concatenate(outs, axis=1) * pscale).astype(BF16)


def _conv3(u, prev8, w, b):
    t = u.shape[0]
    ext = jnp.concatenate([prev8, u], axis=0)
    u1 = pltpu.roll(ext, 1, 0)[SUBLANES:SUBLANES + t]
    u2 = pltpu.roll(ext, 2, 0)[SUBLANES:SUBLANES + t]
    c = b + u2 * w[0:1]
    c = c + u1 * w[1:2]
    return c + u * w[2:3]


def _silu(x):
    return x * jax.nn.sigmoid(x)


N_SIDE = N_META + 8 * 16
SIDE_FF_CHUNK = 256


def _side_kernel(xs_ref, ck_ref, cv_ref, cmk_ref, cmv_ref, spool_ref, sconv_ref, cos_ref, sin_ref, tabm_ref,
                 tabs_ref, gmix_ref, w_in_ref, bgate_ref, w_ao_ref, wgrp_ref, pscale_ref, w_po_ref, w_out_ref,
                 gffn_ref, w_up_ref, convw_ref, convb_ref, w_down_ref, gfin_ref,
                 y_ref, k_ref, v_ref, p_ref, up_ref):
    dec_b = ck_ref.shape[0]
    t_dec = (N_SIDE - N_META) // dec_b
    x = xs_ref[...]
    h = _rmsnorm(x, gmix_ref[...]).astype(BF16)
    cos, sin = cos_ref[...], sin_ref[...]
    q = _rope(_dot(h, w_in_ref[:, 0:Q_W]) * (HEAD_DIM ** -0.5), cos, sin).astype(BF16)
    k = _rope(_dot(h, w_in_ref[:, K_OFF:K_OFF + KV_W]), cos, sin)
    v = _dot(h, w_in_ref[:, V_OFF:V_OFF + KV_W])
    p = _dot(h, w_in_ref[:, P_OFF:P_OFF + POOL_WIDTH])
    gates = jax.nn.sigmoid(_dot(h, w_in_ref[:, G_OFF:G_OFF + GATE_W]) + bgate_ref[...])
    k_ref[...] = k
    v_ref[...] = v
    p_ref[...] = p

    def attend_rows(r0, nrows, kx, vx, tab_ref):
        kvar, vvar = _kv_variants(kx), _kv_variants(vx)
        blocks = [None] * Q_BLOCKS
        for n in range(N_KV_HEADS):
            qg = jnp.concatenate([q[r0:r0 + nrows, (BLOCKS_PER_KV * n + bi) * LANES:(BLOCKS_PER_KV * n + bi + 1) * LANES]
                                  for bi in range(BLOCKS_PER_KV)], axis=0)
            kst = jnp.concatenate(kvar[n], axis=0)
            vst = jnp.concatenate(vvar[n], axis=0)
            o = _attend_group(qg, kst, vst, tab_ref[n])
            for bi in range(BLOCKS_PER_KV):
                blocks[BLOCKS_PER_KV * n + bi] = o[bi * nrows:(bi + 1) * nrows]
        return jnp.concatenate(blocks, axis=1)

    zpad = jnp.zeros((LANES - N_META, LANES), F32)
    o_rows = [attend_rows(0, N_META, jnp.concatenate([k[:N_META], zpad], axis=0),
                          jnp.concatenate([v[:N_META], zpad], axis=0), tabm_ref)]
    n_keys = N_META + WINDOW + t_dec
    zpad = jnp.zeros((2 * LANES - n_keys, LANES), F32)
    for b in range(dec_b):
        r0 = N_META + b * t_dec
        kx = jnp.concatenate([cmk_ref[b], ck_ref[b], k[r0:r0 + t_dec], zpad], axis=0)
        vx = jnp.concatenate([cmv_ref[b], cv_ref[b], v[r0:r0 + t_dec], zpad], axis=0)
        o_rows.append(attend_rows(r0, t_dec, kx, vx, tabs_ref))
    o_attn = jnp.concatenate(o_rows, axis=0).astype(BF16)

    row = lax.broadcasted_iota(jnp.int32, (N_META, LANES), 0)
    pm_rows = []
    for seg in range(1 + dec_b):
        if seg == 0:
            ext = jnp.concatenate([jnp.zeros((N_META, POOL_WIDTH), F32), p[:N_META]], axis=0)
        else:
            r0 = N_META + (seg - 1) * t_dec
            ext = jnp.concatenate([spool_ref[seg - 1], p[r0:r0 + t_dec]], axis=0)
        groups = []
        for g in range(N_POOL_GROUPS):
            cnt = jnp.minimum(POOL_WINDOWS[g], row + 1).astype(F32) if seg == 0 else None
            groups.append(_pool_group(ext[:, g * POOL_GROUP_W:(g + 1) * POOL_GROUP_W], g, cnt))
        pm_rows.append(jnp.concatenate(groups, axis=1))
    pm = jnp.concatenate(pm_rows, axis=0)

    pool = _pool_project(pm, wgrp_ref, pscale_ref[...])
    mix = gates[:, :D_MODEL] * _dot(o_attn, w_ao_ref[...]) + gates[:, D_MODEL:] * _dot(pool, w_po_ref[...])
    x1 = x + _dot(mix.astype(BF16), w_out_ref[...])

    hn = _rmsnorm(x1, gffn_ref[...]).astype(BF16)
    acc = jnp.zeros((N_SIDE, D_MODEL), F32)
    zprev = jnp.zeros((SUBLANES, SIDE_FF_CHUNK), F32)
    for c0 in range(0, D_FF, SIDE_FF_CHUNK):
        halves = []
        for off in (c0, D_FF + c0):
            cs = slice(off, off + SIDE_FF_CHUNK)
            u = _dot(hn, w_up_ref[:, cs])
            up_ref[:, cs] = u
            w, bias = convw_ref[:, cs], convb_ref[:, cs]
            segs = [_conv3(u[:N_META], zprev, w, bias)]
            for b in range(dec_b):
                r0 = N_META + b * t_dec
                segs.append(_conv3(u[r0:r0 + t_dec], sconv_ref[b, :, cs], w, bias))
            halves.append(jnp.concatenate(segs, axis=0))
        act = (_silu(halves[0]) * halves[1]).astype(BF16)
        acc = acc + _dot(act, w_down_ref[c0:c0 + SIDE_FF_CHUNK, :])
    y_ref[...] = _rmsnorm(x1 + acc, gfin_ref[...])


def _side_call(*args):
    out_shape = (
        jax.ShapeDtypeStruct((N_SIDE, D_MODEL), F32),
        jax.ShapeDtypeStruct((N_SIDE, KV_W), F32),
        jax.ShapeDtypeStruct((N_SIDE, KV_W), F32),
        jax.ShapeDtypeStruct((N_SIDE, POOL_WIDTH), F32),
        jax.ShapeDtypeStruct((N_SIDE, 2 * D_FF), F32),
    )
    return pl.pallas_call(
        _side_kernel,
        out_shape=out_shape,
        compiler_params=pltpu.CompilerParams(vmem_limit_bytes=VMEM_LIMIT),
        name="side_rows",
    )(*args)


TM_IN = 512


def _inproj_kernel(x_ref, cos_ref, sin_ref, gmix_ref, w_in_ref, bgate_ref, q_ref, k_ref, v_ref, p_ref, g_ref):
    h = _rmsnorm(x_ref[...], gmix_ref[...]).astype(BF16)
    cos, sin = cos_ref[...], sin_ref[...]
    q_ref[...] = _rope(_dot(h, w_in_ref[:, 0:Q_W]) * (HEAD_DIM ** -0.5), cos, sin).astype(BF16)
    k_ref[...] = _rope(_dot(h, w_in_ref[:, K_OFF:K_OFF + KV_W]), cos, sin)
    v_ref[...] = _dot(h, w_in_ref[:, V_OFF:V_OFF + KV_W])
    p_ref[...] = _dot(h, w_in_ref[:, P_OFF:P_OFF + POOL_WIDTH])
    g_ref[...] = jax.nn.sigmoid(_dot(h, w_in_ref[:, G_OFF:G_OFF + GATE_W]) + bgate_ref[...]).astype(BF16)


def _const_spec(shape):
    nd = len(shape)
    return pl.BlockSpec(shape, lambda *_: (0,) * nd)


def _inproj_call(x, cos, sin, gmix, w_in, bgate):
    bsz, seq, _ = x.shape
    tm = TM_IN
    row = lambda w: pl.BlockSpec((None, tm, w), lambda b, i: (b, i, 0))
    return pl.pallas_call(
        _inproj_kernel,
        grid=(bsz, seq // tm),
        in_specs=[row(D_MODEL), pl.BlockSpec((tm, LANES), lambda b, i: (i, 0)),
                  pl.BlockSpec((tm, LANES), lambda b, i: (i, 0)),
                  _const_spec(gmix.shape), _const_spec(w_in.shape), _const_spec(bgate.shape)],
        out_specs=(row(Q_W), row(KV_W), row(KV_W), row(POOL_WIDTH), row(GATE_W)),
        out_shape=(jax.ShapeDtypeStruct((bsz, seq, Q_W), BF16), jax.ShapeDtypeStruct((bsz, seq, KV_W), F32),
                   jax.ShapeDtypeStruct((bsz, seq, KV_W), F32), jax.ShapeDtypeStruct((bsz, seq, POOL_WIDTH), F32),
                   jax.ShapeDtypeStruct((bsz, seq, GATE_W), BF16)),
        compiler_params=pltpu.CompilerParams(dimension_semantics=("arbitrary", "arbitrary"),
                                             vmem_limit_bytes=VMEM_LIMIT),
        name="in_proj",
    )(x, cos, sin, gmix, w_in, bgate)


TQ = 256
BAND = WINDOW + CHUNK
KEYS_PAD = 2 * LANES
N_TABS = WINDOW // CHUNK + 1


def _attn_kernel(q_ref, kc_ref, kp_ref, vc_ref, vp_ref, mk_ref, mv_ref, tab_ref, o_ref):
    i = pl.program_id(1)
    kvar = _kv_variants(jnp.concatenate([kp_ref[...], kc_ref[...]], axis=0))
    vvar = _kv_variants(jnp.concatenate([vp_ref[...], vc_ref[...]], axis=0))
    mkvar, mvvar = _kv_variants(mk_ref[...]), _kv_variants(mv_ref[...])
    zpad = jnp.zeros((KEYS_PAD - BAND - N_META, LANES), BF16)
    for j in range(TQ // CHUNK):
        tsel = jnp.minimum(i * (TQ // CHUNK) + j, N_TABS - 1)
        rows = slice(j * CHUNK, (j + 1) * CHUNK)
        band = slice(j * CHUNK, j * CHUNK + BAND)
        for n in range(N_KV_HEADS):
            kst = jnp.concatenate([kvar[n][0][band], mkvar[n][0], zpad, kvar[n][1][band], mkvar[n][1], zpad], axis=0)
            vst = jnp.concatenate([vvar[n][0][band], mvvar[n][0], zpad, vvar[n][1][band], mvvar[n][1], zpad], axis=0)
            qg = jnp.concatenate([q_ref[rows, (BLOCKS_PER_KV * n + bi) * LANES:(BLOCKS_PER_KV * n + bi + 1) * LANES]
                                  for bi in range(BLOCKS_PER_KV)], axis=0)
            o = _attend_group(qg, kst, vst, tab_ref[tsel, n])
            for bi in range(BLOCKS_PER_KV):
                blk = BLOCKS_PER_KV * n + bi
                o_ref[rows, blk * LANES:(blk + 1) * LANES] = o[bi * CHUNK:(bi + 1) * CHUNK].astype(BF16)


def _attn_call(q, k, v, mk, mv, tab):
    bsz, seq, _ = q.shape
    cur = lambda w: pl.BlockSpec((None, TQ, w), lambda b, i: (b, i, 0))
    prev = pl.BlockSpec((None, WINDOW, KV_W), lambda b, i: (b, jnp.maximum(i * (TQ // WINDOW) - 1, 0), 0))
    return pl.pallas_call(
        _attn_kernel,
        grid=(bsz, seq // TQ),
        in_specs=[cur(Q_W), cur(KV_W), prev, cur(KV_W), prev, _const_spec(mk.shape), _const_spec(mv.shape),
                  _const_spec(tab.shape)],
        out_specs=cur(Q_W),
        out_shape=jax.ShapeDtypeStruct((bsz, seq, Q_W), BF16),
        compiler_params=pltpu.CompilerParams(dimension_semantics=("arbitrary", "arbitrary"),
                                             vmem_limit_bytes=VMEM_LIMIT),
        name="swa_attention",
    )(q, k, k, v, v, mk, mv, tab)


TM_MIX = 512


def _mixer_kernel(x_ref, o_ref, p_ref, pprev_ref, pmeta_ref, g_ref, w_ao_ref, wgrp_ref, pscale_ref, w_po_ref,
                  w_out_ref, x1_ref, pext_ref):
    i = pl.program_id(1)

    @pl.when(i == 0)
    def _():
        pext_ref[0:N_META] = pmeta_ref[...]

    @pl.when(i > 0)
    def _():
        pext_ref[0:N_META] = pprev_ref[...]

    pext_ref[N_META:] = p_ref[...]
    pm = jnp.concatenate([_pool_group(pext_ref[:, g * POOL_GROUP_W:(g + 1) * POOL_GROUP_W], g)
                          for g in range(N_POOL_GROUPS)], axis=1)
    pool = _pool_project(pm, wgrp_ref, pscale_ref[...])
    mix = (g_ref[:, :D_MODEL].astype(F32) * _dot(o_ref[...], w_ao_ref[...])
           + g_ref[:, D_MODEL:].astype(F32) * _dot(pool, w_po_ref[...]))
    x1_ref[...] = x_ref[...] + _dot(mix.astype(BF16), w_out_ref[...])


def _mixer_call(x, o_attn, p, p_meta, gates, w_ao, wgrp, pscale, w_po, w_out):
    bsz, seq, _ = x.shape
    tm = TM_MIX
    row = lambda w: pl.BlockSpec((None, tm, w), lambda b, i: (b, i, 0))
    pprev = pl.BlockSpec((None, N_META, POOL_WIDTH), lambda b, i: (b, jnp.maximum(i * (tm // N_META) - 1, 0), 0))
    return pl.pallas_call(
        _mixer_kernel,
        grid=(bsz, seq // tm),
        in_specs=[row(D_MODEL), row(Q_W), row(POOL_WIDTH), pprev, _const_spec(p_meta.shape), row(GATE_W),
                  _const_spec(w_ao.shape), _const_spec(wgrp.shape), _const_spec(pscale.shape),
                  _const_spec(w_po.shape), _const_spec(w_out.shape)],
        out_specs=row(D_MODEL),
        out_shape=jax.ShapeDtypeStruct((bsz, seq, D_MODEL), F32),
        scratch_shapes=[pltpu.VMEM((N_META + tm, POOL_WIDTH), F32)],
        compiler_params=pltpu.CompilerParams(dimension_semantics=("arbitrary", "arbitrary"),
                                             vmem_limit_bytes=VMEM_LIMIT),
        name="pool_mixer",
    )(x, o_attn, p, p, p_meta, gates, w_ao, wgrp, pscale, w_po, w_out)


TM_FFN = 256
FF_CHUNK = 256


def _ffn_kernel(x1_ref, upmeta_ref, gffn_ref, w_up_ref, convw_ref, convb_ref, w_down_ref, gfin_ref, y_ref, carry_ref):
    @pl.when(pl.program_id(1) == 0)
    def _():
        carry_ref[...] = upmeta_ref[...]

    x1 = x1_ref[...]
    tm = x1.shape[0]
    hn = _rmsnorm(x1, gffn_ref[...]).astype(BF16)
    acc = jnp.zeros((tm, D_MODEL), F32)
    for c0 in range(0, D_FF, FF_CHUNK):
        halves = []
        for off in (c0, D_FF + c0):
            cs = slice(off, off + FF_CHUNK)
            u = _dot(hn, w_up_ref[:, cs])
            halves.append(_conv3(u, carry_ref[:, cs], convw_ref[:, cs], convb_ref[:, cs]))
            carry_ref[:, cs] = u[tm - SUBLANES:]
        act = (_silu(halves[0]) * halves[1]).astype(BF16)
        acc = acc + _dot(act, w_down_ref[c0:c0 + FF_CHUNK, :])
    y_ref[...] = _rmsnorm(x1 + acc, gfin_ref[...])


def _ffn_call(x1, up_meta8, gffn, w_up, convw, convb, w_down, gfin):
    bsz, seq, _ = x1.shape
    tm = TM_FFN
    row = pl.BlockSpec((None, tm, D_MODEL), lambda b, i: (b, i, 0))
    return pl.pallas_call(
        _ffn_kernel,
        grid=(bsz, seq // tm),
        in_specs=[row, _const_spec(up_meta8.shape), _const_spec(gffn.shape), _const_spec(w_up.shape),
                  _const_spec(convw.shape), _const_spec(convb.shape), _const_spec(w_down.shape),
                  _const_spec(gfin.shape)],
        out_specs=(row, pl.BlockSpec((None, SUBLANES, 2 * D_FF), lambda b, i: (b, 0, 0))),
        out_shape=(jax.ShapeDtypeStruct((bsz, seq, D_MODEL), F32),
                   jax.ShapeDtypeStruct((bsz, SUBLANES, 2 * D_FF), F32)),
        compiler_params=pltpu.CompilerParams(dimension_semantics=("arbitrary", "arbitrary"),
                                             vmem_limit_bytes=VMEM_LIMIT),
        name="conv_ffn",
    )(x1, up_meta8, gffn, w_up, convw, convb, w_down, gfin)


def _rope_tables(pos):
    half = HEAD_DIM // 2
    inv = ROPE_THETA ** (-jnp.arange(half, dtype=F32) / half)
    ang = pos.astype(F32)[:, None] * inv[None, :]
    cos, sin = jnp.cos(ang), jnp.sin(ang)
    return jnp.tile(cos, (1, LANES // half)), jnp.tile(jnp.concatenate([-sin, sin], axis=1), (1, LANES // HEAD_DIM))


def _score_table(sinks, rows_per_block, n_keys, keys_pad, masked_prefix=0):
    col = jnp.arange(keys_pad)
    base = jnp.where((col >= masked_prefix) & (col < n_keys), 0.0, NEG).astype(F32)
    head = (2 * jnp.arange(Q_BLOCKS)[:, None] + jnp.arange(2)[None, :])
    tab = jnp.where(col[None, None, :] == n_keys, sinks.astype(F32)[head][:, :, None], base[None, None, :])
    tab = tab.reshape(N_KV_HEADS, BLOCKS_PER_KV, 1, 2 * keys_pad)
    tab = jnp.broadcast_to(tab, (N_KV_HEADS, BLOCKS_PER_KV, rows_per_block, 2 * keys_pad))
    return tab.reshape(N_KV_HEADS, BLOCKS_PER_KV * rows_per_block, 2 * keys_pad)


def kernel(x_prompt, x_sample, cache_swa_k, cache_swa_v, cache_meta_k, cache_meta_v, state_pool, state_conv,
           meta_tokens, g_norm_mix, w_in, b_gate, sinks, w_attn_o, w_pool_grp, pool_scale, w_pool_o, w_out,
           g_norm_ffn, w_up, conv_w, conv_b, w_down, g_norm_final):
    bsz, seq, _ = x_prompt.shape
    dbsz, t_dec, _ = x_sample.shape
    row2 = lambda a: a.reshape(1, -1)
    w_in_b, w_ao_b, wgrp_b, w_po_b = w_in.astype(BF16), w_attn_o.astype(BF16), w_pool_grp.astype(BF16), w_pool_o.astype(BF16)
    w_out_b, w_up_b, w_down_b = w_out.astype(BF16), w_up.astype(BF16), w_down.astype(BF16)
    gmix, gffn, gfin = row2(g_norm_mix), row2(g_norm_ffn), row2(g_norm_final)
    bgate, pscale, convb = row2(b_gate), row2(pool_scale), row2(conv_b)

    pos_side = jnp.concatenate([jnp.arange(N_META, dtype=jnp.int32),
                                jnp.tile(N_META + PAST_LEN + jnp.arange(t_dec, dtype=jnp.int32), dbsz)])
    cos_s, sin_s = _rope_tables(pos_side)
    xs = jnp.concatenate([meta_tokens, x_sample.reshape(dbsz * t_dec, D_MODEL)], axis=0)
    spool16 = jnp.pad(state_pool, ((0, 0), (N_META - POOL_HIST, 0), (0, 0)))
    sconv8 = jnp.pad(state_conv, ((0, 0), (SUBLANES - (CONV_W - 1), 0), (0, 0)))
    tab_meta = _score_table(sinks, N_META, N_META, LANES)
    tab_dec = _score_table(sinks, t_dec, N_META + WINDOW + t_dec, 2 * LANES)
    y_side, k_side, v_side, p_side, up_side = _side_call(
        xs, cache_swa_k.reshape(dbsz, WINDOW, KV_W), cache_swa_v.reshape(dbsz, WINDOW, KV_W),
        cache_meta_k.reshape(dbsz, N_META, KV_W), cache_meta_v.reshape(dbsz, N_META, KV_W), spool16, sconv8,
        cos_s, sin_s, tab_meta, tab_dec, gmix, w_in_b, bgate, w_ao_b, wgrp_b, pscale, w_po_b, w_out_b,
        gffn, w_up_b, conv_w, convb, w_down_b, gfin)
    km, vm, p_meta = k_side[:N_META], v_side[:N_META], p_side[:N_META]
    up_meta8 = up_side[N_META - SUBLANES:N_META]

    cos_p, sin_p = _rope_tables(N_META + jnp.arange(seq, dtype=jnp.int32))
    q, k, v, p, gates = _inproj_call(x_prompt, cos_p, sin_p, gmix, w_in_b, bgate)
    tab = jnp.stack([
        _score_table(sinks, CHUNK, BAND + N_META, KEYS_PAD, masked_prefix=(WINDOW // CHUNK - c) * CHUNK)
        for c in range(N_TABS)])
    o_attn = _attn_call(q, k, v, km, vm, tab)
    x1 = _mixer_call(x_prompt, o_attn, p, p_meta, gates, w_ao_b, wgrp_b, pscale, w_po_b, w_out_b)
    y_prompt, up_tail = _ffn_call(x1, up_meta8, gffn, w_up_b, conv_w, convb, w_down_b, gfin)

    kv4 = lambda a, n: a.reshape(a.shape[0], n, N_KV_HEADS, HEAD_DIM)
    dec = lambda a: a[N_META:].reshape(dbsz, t_dec, -1)
    return (
        y_prompt,
        dec(y_side),
        kv4(k[:, seq - WINDOW:], WINDOW),
        kv4(v[:, seq - WINDOW:], WINDOW),
        jnp.broadcast_to(km.reshape(1, N_META, N_KV_HEADS, HEAD_DIM), (bsz, N_META, N_KV_HEADS, HEAD_DIM)),
        jnp.broadcast_to(vm.reshape(1, N_META, N_KV_HEADS, HEAD_DIM), (bsz, N_META, N_KV_HEADS, HEAD_DIM)),
        p[:, seq - POOL_HIST:],
        up_tail[:, SUBLANES - (CONV_W - 1):],
        kv4(dec(k_side), t_dec),
        kv4(dec(v_side), t_dec),
        dec(p_side)[:, t_dec - POOL_HIST:],
        dec(up_side)[:, t_dec - (CONV_W - 1):],
    )
```

```python
import functools

import jax
import jax.numpy as jnp
from jax import lax
from jax.experimental import pallas as pl
from jax.experimental.pallas import tpu as pltpu

D_MODEL = 1024
N_META = 16
CHUNK = 64
HEAD_DIM = 64
N_Q_HEADS = 16
N_KV_HEADS = 2
WINDOW = 128
ROPE_THETA = 10000.0
POOL_WINDOWS = (2, 4, 8, 16)
N_POOL_GROUPS = 4
POOL_WIDTH = D_MODEL // 2
POOL_GROUP_W = POOL_WIDTH // N_POOL_GROUPS
POOL_HIST = max(POOL_WINDOWS) - 1
Q_W = N_Q_HEADS * HEAD_DIM
KV_W = N_KV_HEADS * HEAD_DIM
GATE_W = 2 * D_MODEL
IN_W = Q_W + 2 * KV_W + POOL_WIDTH + GATE_W
D_FF = ((8 * D_MODEL // 3) + 127) // 128 * 128
CONV_W = 3
RMS_EPS = 1e-6
PAST_LEN = 1024

LANES = 128
SUBLANES = 8
Q_BLOCKS = Q_W // LANES
BLOCKS_PER_KV = Q_BLOCKS // N_KV_HEADS
NEG = -1e30

K_OFF = Q_W
V_OFF = Q_W + KV_W
P_OFF = Q_W + 2 * KV_W
G_OFF = P_OFF + POOL_WIDTH

F32 = jnp.float32
BF16 = jnp.bfloat16

VMEM_LIMIT = 56 * 1024 * 1024


def _rmsnorm(x, g):
    ms = jnp.mean(x * x, axis=-1, keepdims=True)
    return (x * lax.rsqrt(ms + RMS_EPS)) * g


def _dot(a, b):
    return jnp.dot(a, b, preferred_element_type=F32)


def _rope(x, cos, sin_signed):
    lane = lax.broadcasted_iota(jnp.int32, (x.shape[0], LANES), 1)
    first_half = (lane & (HEAD_DIM // 2)) == 0
    outs = []
    for b in range(x.shape[1] // LANES):
        xb = x[:, b * LANES:(b + 1) * LANES]
        partner = jnp.where(first_half, pltpu.roll(xb, LANES - HEAD_DIM // 2, 1), pltpu.roll(xb, HEAD_DIM // 2, 1))
        outs.append(xb * cos + partner * sin_signed)
    return outs[0] if len(outs) == 1 else jnp.concatenate(outs, axis=1)


def _kv_variants(kv):
    lane = lax.broadcasted_iota(jnp.int32, kv.shape, 1)
    low = lane < HEAD_DIM
    swapped = pltpu.roll(kv, HEAD_DIM, 1)
    zero = jnp.zeros_like(kv)
    return (
        (jnp.where(low, kv, zero).astype(BF16), jnp.where(low, zero, swapped).astype(BF16)),
        (jnp.where(low, swapped, zero).astype(BF16), jnp.where(low, zero, kv).astype(BF16)),
    )


def _attend_group(qg, kst, vst, tab):
    s = lax.dot_general(qg, kst, (((1,), (1,)), ((), ())), preferred_element_type=F32) + tab
    npad = kst.shape[0] // 2
    se, so = s[:, :npad], s[:, npad:]
    pe = jnp.exp(se - jnp.max(se, axis=1, keepdims=True))
    po = jnp.exp(so - jnp.max(so, axis=1, keepdims=True))
    le = jnp.sum(pe, axis=1, keepdims=True)
    lo = jnp.sum(po, axis=1, keepdims=True)
    p = jnp.concatenate([pe, po], axis=1).astype(BF16)
    o = _dot(p, vst)
    lane = lax.broadcasted_iota(jnp.int32, o.shape, 1)
    return o * jnp.where(lane < HEAD_DIM, 1.0 / le, 1.0 / lo)


def _pool_group(ext, g, cnt=None):
    w = POOL_WINDOWS[g]
    s = ext
    k = 1
    while k < w:
        s = s + pltpu.roll(s, k, 0)
        k *= 2
    cur = ext[N_META:]
    return s[N_META:] / (float(w) if cnt is None else cnt) - cur


def _pool_project(pm, wgrp_ref, pscale):
    outs = []
    for g in range(N_POOL_GROUPS):
        sl = slice(g * POOL_GROUP_W, (g + 1) * POOL_GROUP_W)
        outs.append(_dot(pm[:, sl].astype(BF16), wgrp_ref[g]))
    return (jnp.concatenate(outs, axis=1) * pscale).astype(BF16)


def _conv3(u, prev8, w, b):
    t = u.shape[0]
    ext = jnp.concatenate([prev8, u], axis=0)
    u1 = pltpu.roll(ext, 1, 0)[SUBLANES:SUBLANES + t]
    u2 = pltpu.roll(ext, 2, 0)[SUBLANES:SUBLANES + t]
    c = b + u2 * w[0:1]
    c = c + u1 * w[1:2]
    return c + u * w[2:3]


def _silu(x):
    return x * jax.nn.sigmoid(x)


N_SIDE = N_META + 8 * 16
SIDE_FF_CHUNK = 256


def _side_kernel(xs_ref, ck_ref, cv_ref, cmk_ref, cmv_ref, spool_ref, sconv_ref, cos_ref, sin_ref, tabm_ref,
                 tabs_ref, gmix_ref, w_in_ref, bgate_ref, w_ao_ref, wgrp_ref, pscale_ref, w_po_ref, w_out_ref,
                 gffn_ref, w_up_ref, convw_ref, convb_ref, w_down_ref, gfin_ref,
                 y_ref, k_ref, v_ref, p_ref, up_ref):
    dec_b = ck_ref.shape[0]
    t_dec = (N_SIDE - N_META) // dec_b
    x = xs_ref[...]
    h = _rmsnorm(x, gmix_ref[...]).astype(BF16)
    cos, sin = cos_ref[...], sin_ref[...]
    q = _rope(_dot(h, w_in_ref[:, 0:Q_W]) * (HEAD_DIM ** -0.5), cos, sin).astype(BF16)
    k = _rope(_dot(h, w_in_ref[:, K_OFF:K_OFF + KV_W]), cos, sin)
    v = _dot(h, w_in_ref[:, V_OFF:V_OFF + KV_W])
    p = _dot(h, w_in_ref[:, P_OFF:P_OFF + POOL_WIDTH])
    gates = jax.nn.sigmoid(_dot(h, w_in_ref[:, G_OFF:G_OFF + GATE_W]) + bgate_ref[...])
    k_ref[...] = k
    v_ref[...] = v
    p_ref[...] = p

    def attend_rows(r0, nrows, kx, vx, tab_ref):
        kvar, vvar = _kv_variants(kx), _kv_variants(vx)
        blocks = [None] * Q_BLOCKS
        for n in range(N_KV_HEADS):
            qg = jnp.concatenate([q[r0:r0 + nrows, (BLOCKS_PER_KV * n + bi) * LANES:(BLOCKS_PER_KV * n + bi + 1) * LANES]
                                  for bi in range(BLOCKS_PER_KV)], axis=0)
            kst = jnp.concatenate(kvar[n], axis=0)
            vst = jnp.concatenate(vvar[n], axis=0)
            o = _attend_group(qg, kst, vst, tab_ref[n])
            for bi in range(BLOCKS_PER_KV):
                blocks[BLOCKS_PER_KV * n + bi] = o[bi * nrows:(bi + 1) * nrows]
        return jnp.concatenate(blocks, axis=1)

    zpad = jnp.zeros((LANES - N_META, LANES), F32)
    o_rows = [attend_rows(0, N_META, jnp.concatenate([k[:N_META], zpad], axis=0),
                          jnp.concatenate([v[:N_META], zpad], axis=0), tabm_ref)]
    n_keys = N_META + WINDOW + t_dec
    zpad = jnp.zeros((2 * LANES - n_keys, LANES), F32)
    for b in range(dec_b):
        r0 = N_META + b * t_dec
        kx = jnp.concatenate([cmk_ref[b], ck_ref[b], k[r0:r0 + t_dec], zpad], axis=0)
        vx = jnp.concatenate([cmv_ref[b], cv_ref[b], v[r0:r0 + t_dec], zpad], axis=0)
        o_rows.append(attend_rows(r0, t_dec, kx, vx, tabs_ref))
    o_attn = jnp.concatenate(o_rows, axis=0).astype(BF16)

    row = lax.broadcasted_iota(jnp.int32, (N_META, LANES), 0)
    pm_rows = []
    for seg in range(1 + dec_b):
        if seg == 0:
            ext = jnp.concatenate([jnp.zeros((N_META, POOL_WIDTH), F32), p[:N_META]], axis=0)
        else:
            r0 = N_META + (seg - 1) * t_dec
            ext = jnp.concatenate([spool_ref[seg - 1], p[r0:r0 + t_dec]], axis=0)
        groups = []
        for g in range(N_POOL_GROUPS):
            cnt = jnp.minimum(POOL_WINDOWS[g], row + 1).astype(F32) if seg == 0 else None
            groups.append(_pool_group(ext[:, g * POOL_GROUP_W:(g + 1) * POOL_GROUP_W], g, cnt))
        pm_rows.append(jnp.concatenate(groups, axis=1))
    pm = jnp.concatenate(pm_rows, axis=0)

    pool = _pool_project(pm, wgrp_ref, pscale_ref[...])
    mix = gates[:, :D_MODEL] * _dot(o_attn, w_ao_ref[...]) + gates[:, D_MODEL:] * _dot(pool, w_po_ref[...])
    x1 = x + _dot(mix.astype(BF16), w_out_ref[...])

    hn = _rmsnorm(x1, gffn_ref[...]).astype(BF16)
    acc = jnp.zeros((N_SIDE, D_MODEL), F32)
    zprev = jnp.zeros((SUBLANES, SIDE_FF_CHUNK), F32)
    for c0 in range(0, D_FF, SIDE_FF_CHUNK):
        halves = []
        for off in (c0, D_FF + c0):
            cs = slice(off, off + SIDE_FF_CHUNK)
            u = _dot(hn, w_up_ref[:, cs])
            up_ref[:, cs] = u
            w, bias = convw_ref[:, cs], convb_ref[:, cs]
            segs = [_conv3(u[:N_META], zprev, w, bias)]
            for b in range(dec_b):
                r0 = N_META + b * t_dec
                segs.append(_conv3(u[r0:r0 + t_dec], sconv_ref[b, :, cs], w, bias))
            halves.append(jnp.concatenate(segs, axis=0))
        act = (_silu(halves[0]) * halves[1]).astype(BF16)
        acc = acc + _dot(act, w_down_ref[c0:c0 + SIDE_FF_CHUNK, :])
    y_ref[...] = _rmsnorm(x1 + acc, gfin_ref[...])


def _side_call(*args):
    out_shape = (
        jax.ShapeDtypeStruct((N_SIDE, D_MODEL), F32),
        jax.ShapeDtypeStruct((N_SIDE, KV_W), F32),
        jax.ShapeDtypeStruct((N_SIDE, KV_W), F32),
        jax.ShapeDtypeStruct((N_SIDE, POOL_WIDTH), F32),
        jax.ShapeDtypeStruct((N_SIDE, 2 * D_FF), F32),
    )
    return pl.pallas_call(
        _side_kernel,
        out_shape=out_shape,
        compiler_params=pltpu.CompilerParams(vmem_limit_bytes=VMEM_LIMIT),
        name="side_rows",
    )(*args)


TM_IN = 512


def _inproj_kernel(x_ref, cos_ref, sin_ref, gmix_ref, w_in_ref, bgate_ref, q_ref, k_ref, v_ref, p_ref, g_ref):
    h = _rmsnorm(x_ref[...], gmix_ref[...]).astype(BF16)
    cos, sin = cos_ref[...], sin_ref[...]
    q_ref[...] = _rope(_dot(h, w_in_ref[:, 0:Q_W]) * (HEAD_DIM ** -0.5), cos, sin).astype(BF16)
    k_ref[...] = _rope(_dot(h, w_in_ref[:, K_OFF:K_OFF + KV_W]), cos, sin)
    v_ref[...] = _dot(h, w_in_ref[:, V_OFF:V_OFF + KV_W])
    p_ref[...] = _dot(h, w_in_ref[:, P_OFF:P_OFF + POOL_WIDTH])
    g_ref[...] = jax.nn.sigmoid(_dot(h, w_in_ref[:, G_OFF:G_OFF + GATE_W]) + bgate_ref[...]).astype(BF16)


def _const_spec(shape):
    nd = len(shape)
    return pl.BlockSpec(shape, lambda *_: (0,) * nd)


def _inproj_call(x, cos, sin, gmix, w_in, bgate):
    bsz, seq, _ = x.shape
    tm = TM_IN
    row = lambda w: pl.BlockSpec((None, tm, w), lambda b, i: (b, i, 0))
    return pl.pallas_call(
        _inproj_kernel,
        grid=(bsz, seq // tm),
        in_specs=[row(D_MODEL), pl.BlockSpec((tm, LANES), lambda b, i: (i, 0)),
                  pl.BlockSpec((tm, LANES), lambda b, i: (i, 0)),
                  _const_spec(gmix.shape), _const_spec(w_in.shape), _const_spec(bgate.shape)],
        out_specs=(row(Q_W), row(KV_W), row(KV_W), row(POOL_WIDTH), row(GATE_W)),
        out_shape=(jax.ShapeDtypeStruct((bsz, seq, Q_W), BF16), jax.ShapeDtypeStruct((bsz, seq, KV_W), F32),
                   jax.ShapeDtypeStruct((bsz, seq, KV_W), F32), jax.ShapeDtypeStruct((bsz, seq, POOL_WIDTH), F32),
                   jax.ShapeDtypeStruct((bsz, seq, GATE_W), BF16)),
        compiler_params=pltpu.CompilerParams(dimension_semantics=("arbitrary", "arbitrary"),
                                             vmem_limit_bytes=VMEM_LIMIT),
        name="in_proj",
    )(x, cos, sin, gmix, w_in, bgate)


TQ = 256
BAND = WINDOW + CHUNK
KEYS_PAD = 2 * LANES
N_TABS = WINDOW // CHUNK + 1


def _attn_kernel(q_ref, kc_ref, kp_ref, vc_ref, vp_ref, mk_ref, mv_ref, tab_ref, o_ref):
    i = pl.program_id(1)
    kvar = _kv_variants(jnp.concatenate([kp_ref[...], kc_ref[...]], axis=0))
    vvar = _kv_variants(jnp.concatenate([vp_ref[...], vc_ref[...]], axis=0))
    mkvar, mvvar = _kv_variants(mk_ref[...]), _kv_variants(mv_ref[...])
    zpad = jnp.zeros((KEYS_PAD - BAND - N_META, LANES), BF16)
    for j in range(TQ // CHUNK):
        tsel = jnp.minimum(i * (TQ // CHUNK) + j, N_TABS - 1)
        rows = slice(j * CHUNK, (j + 1) * CHUNK)
        band = slice(j * CHUNK, j * CHUNK + BAND)
        for n in range(N_KV_HEADS):
            kst = jnp.concatenate([kvar[n][0][band], mkvar[n][0], zpad, kvar[n][1][band], mkvar[n][1], zpad], axis=0)
            vst = jnp.concatenate([vvar[n][0][band], mvvar[n][0], zpad, vvar[n][1][band], mvvar[n][1], zpad], axis=0)
            qg = jnp.concatenate([q_ref[rows, (BLOCKS_PER_KV * n + bi) * LANES:(BLOCKS_PER_KV * n + bi + 1) * LANES]
                                  for bi in range(BLOCKS_PER_KV)], axis=0)
            o = _attend_group(qg, kst, vst, tab_ref[tsel, n])
            for bi in range(BLOCKS_PER_KV):
                blk = BLOCKS_PER_KV * n + bi
                o_ref[rows, blk * LANES:(blk + 1) * LANES] = o[bi * CHUNK:(bi + 1) * CHUNK].astype(BF16)


def _attn_call(q, k, v, mk, mv, tab):
    bsz, seq, _ = q.shape
    cur = lambda w: pl.BlockSpec((None, TQ, w), lambda b, i: (b, i, 0))
    prev = pl.BlockSpec((None, WINDOW, KV_W), lambda b, i: (b, jnp.maximum(i * (TQ // WINDOW) - 1, 0), 0))
    return pl.pallas_call(
        _attn_kernel,
        grid=(bsz, seq // TQ),
        in_specs=[cur(Q_W), cur(KV_W), prev, cur(KV_W), prev, _const_spec(mk.shape), _const_spec(mv.shape),
                  _const_spec(tab.shape)],
        out_specs=cur(Q_W),
        out_shape=jax.ShapeDtypeStruct((bsz, seq, Q_W), BF16),
        compiler_params=pltpu.CompilerParams(dimension_semantics=("arbitrary", "arbitrary"),
                                             vmem_limit_bytes=VMEM_LIMIT),
        name="swa_attention",
    )(q, k, k, v, v, mk, mv, tab)


TM_MIX = 512


def _mixer_kernel(x_ref, o_ref, p_ref, pprev_ref, pmeta_ref, g_ref, w_ao_ref, wgrp_ref, pscale_ref, w_po_ref,
                  w_out_ref, x1_ref, pext_ref):
    i = pl.program_id(1)

    @pl.when(i == 0)
    def _():
        pext_ref[0:N_META] = pmeta_ref[...]

    @pl.when(i > 0)
    def _():
        pext_ref[0:N_META] = pprev_ref[...]

    pext_ref[N_META:] = p_ref[...]
    pm = jnp.concatenate([_pool_group(pext_ref[:, g * POOL_GROUP_W:(g + 1) * POOL_GROUP_W], g)
                          for g in range(N_POOL_GROUPS)], axis=1)
    pool = _pool_project(pm, wgrp_ref, pscale_ref[...])
    mix = (g_ref[:, :D_MODEL].astype(F32) * _dot(o_ref[...], w_ao_ref[...])
           + g_ref[:, D_MODEL:].astype(F32) * _dot(pool, w_po_ref[...]))
    x1_ref[...] = x_ref[...] + _dot(mix.astype(BF16), w_out_ref[...])


def _mixer_call(x, o_attn, p, p_meta, gates, w_ao, wgrp, pscale, w_po, w_out):
    bsz, seq, _ = x.shape
    tm = TM_MIX
    row = lambda w: pl.BlockSpec((None, tm, w), lambda b, i: (b, i, 0))
    pprev = pl.BlockSpec((None, N_META, POOL_WIDTH), lambda b, i: (b, jnp.maximum(i * (tm // N_META) - 1, 0), 0))
    return pl.pallas_call(
        _mixer_kernel,
        grid=(bsz, seq // tm),
        in_specs=[row(D_MODEL), row(Q_W), row(POOL_WIDTH), pprev, _const_spec(p_meta.shape), row(GATE_W),
                  _const_spec(w_ao.shape), _const_spec(wgrp.shape), _const_spec(pscale.shape),
                  _const_spec(w_po.shape), _const_spec(w_out.shape)],
        out_specs=row(D_MODEL),
        out_shape=jax.ShapeDtypeStruct((bsz, seq, D_MODEL), F32),
        scratch_shapes=[pltpu.VMEM((N_META + tm, POOL_WIDTH), F32)],
        compiler_params=pltpu.CompilerParams(dimension_semantics=("arbitrary", "arbitrary"),
                                             vmem_limit_bytes=VMEM_LIMIT),
        name="pool_mixer",
    )(x, o_attn, p, p, p_meta, gates, w_ao, wgrp, pscale, w_po, w_out)


TM_FFN = 256
FF_CHUNK = 256
FF_SKEW = 2


def _ffn_kernel(x1_ref, upmeta_ref, gffn_ref, w_up_ref, convw_ref, convb_ref, w_down_ref, gfin_ref, y_ref, carry_ref,
                *uext_refs):
    @pl.when(pl.program_id(1) == 0)
    def _():
        carry_ref[...] = upmeta_ref[...]

    x1 = x1_ref[...]
    tm = x1.shape[0]
    hn = _rmsnorm(x1, gffn_ref[...]).astype(BF16)
    n_chunks = D_FF // FF_CHUNK

    lane_blocks = FF_CHUNK // LANES

    def up_project(c):
        for half, off in enumerate((c * FF_CHUNK, D_FF + c * FF_CHUNK)):
            cs = slice(off, off + FF_CHUNK)
            u = _dot(hn, w_up_ref[:, cs])
            for lb in range(lane_blocks):
                ls = slice(lb * LANES, (lb + 1) * LANES)
                uext_refs[c][half * lane_blocks + lb, 0:SUBLANES] = carry_ref[:, off + lb * LANES:off + (lb + 1) * LANES]
                uext_refs[c][half * lane_blocks + lb, SUBLANES:] = u[:, ls]
            carry_ref[:, cs] = u[tm - SUBLANES:]

    def conv(c, half):
        outs = []
        for lb in range(lane_blocks):
            off = half * D_FF + c * FF_CHUNK + lb * LANES
            w, b = convw_ref[:, off:off + LANES], convb_ref[:, off:off + LANES]
            ext = uext_refs[c].at[half * lane_blocks + lb]
            cv = b + ext[SUBLANES - 2:SUBLANES - 2 + tm] * w[0:1]
            cv = cv + ext[SUBLANES - 1:SUBLANES - 1 + tm] * w[1:2]
            outs.append(cv + ext[SUBLANES:] * w[2:3])
        return jnp.concatenate(outs, axis=1)

    for c in range(FF_SKEW):
        up_project(c)
    acc = None
    for c in range(n_chunks):
        act = (_silu(conv(c, 0)) * conv(c, 1)).astype(BF16)
        if c + FF_SKEW < n_chunks:
            up_project(c + FF_SKEW)
        d = _dot(act, w_down_ref[c * FF_CHUNK:(c + 1) * FF_CHUNK, :])
        acc = d if acc is None else acc + d
    y_ref[...] = _rmsnorm(x1 + acc, gfin_ref[...])


def _ffn_call(x1, up_meta8, gffn, w_up, convw, convb, w_down, gfin):
    bsz, seq, _ = x1.shape
    tm = TM_FFN
    row = pl.BlockSpec((None, tm, D_MODEL), lambda b, i: (b, i, 0))
    return pl.pallas_call(
        _ffn_kernel,
        grid=(bsz, seq // tm),
        in_specs=[row, _const_spec(up_meta8.shape), _const_spec(gffn.shape), _const_spec(w_up.shape),
                  _const_spec(convw.shape), _const_spec(convb.shape), _const_spec(w_down.shape),
                  _const_spec(gfin.shape)],
        out_specs=(row, pl.BlockSpec((None, SUBLANES, 2 * D_FF), lambda b, i: (b, 0, 0))),
        out_shape=(jax.ShapeDtypeStruct((bsz, seq, D_MODEL), F32),
                   jax.ShapeDtypeStruct((bsz, SUBLANES, 2 * D_FF), F32)),
        scratch_shapes=[pltpu.VMEM((2 * FF_CHUNK // LANES, SUBLANES + tm, LANES), F32)
                        for _ in range(D_FF // FF_CHUNK)],
        compiler_params=pltpu.CompilerParams(dimension_semantics=("arbitrary", "arbitrary"),
                                             vmem_limit_bytes=VMEM_LIMIT),
        name="conv_ffn",
    )(x1, up_meta8, gffn, w_up, convw, convb, w_down, gfin)


def _rope_tables(pos):
    half = HEAD_DIM // 2
    inv = ROPE_THETA ** (-jnp.arange(half, dtype=F32) / half)
    ang = pos.astype(F32)[:, None] * inv[None, :]
    cos, sin = jnp.cos(ang), jnp.sin(ang)
    return jnp.tile(cos, (1, LANES // half)), jnp.tile(jnp.concatenate([-sin, sin], axis=1), (1, LANES // HEAD_DIM))


def _score_table(sinks, rows_per_block, n_keys, keys_pad, masked_prefix=0):
    col = jnp.arange(keys_pad)
    base = jnp.where((col >= masked_prefix) & (col < n_keys), 0.0, NEG).astype(F32)
    head = (2 * jnp.arange(Q_BLOCKS)[:, None] + jnp.arange(2)[None, :])
    tab = jnp.where(col[None, None, :] == n_keys, sinks.astype(F32)[head][:, :, None], base[None, None, :])
    tab = tab.reshape(N_KV_HEADS, BLOCKS_PER_KV, 1, 2 * keys_pad)
    tab = jnp.broadcast_to(tab, (N_KV_HEADS, BLOCKS_PER_KV, rows_per_block, 2 * keys_pad))
    return tab.reshape(N_KV_HEADS, BLOCKS_PER_KV * rows_per_block, 2 * keys_pad)


def kernel(x_prompt, x_sample, cache_swa_k, cache_swa_v, cache_meta_k, cache_meta_v, state_pool, state_conv,
           meta_tokens, g_norm_mix, w_in, b_gate, sinks, w_attn_o, w_pool_grp, pool_scale, w_pool_o, w_out,
           g_norm_ffn, w_up, conv_w, conv_b, w_down, g_norm_final):
    bsz, seq, _ = x_prompt.shape
    dbsz, t_dec, _ = x_sample.shape
    row2 = lambda a: a.reshape(1, -1)
    w_in_b, w_ao_b, wgrp_b, w_po_b = w_in.astype(BF16), w_attn_o.astype(BF16), w_pool_grp.astype(BF16), w_pool_o.astype(BF16)
    w_out_b, w_up_b, w_down_b = w_out.astype(BF16), w_up.astype(BF16), w_down.astype(BF16)
    gmix, gffn, gfin = row2(g_norm_mix), row2(g_norm_ffn), row2(g_norm_final)
    bgate, pscale, convb = row2(b_gate), row2(pool_scale), row2(conv_b)

    pos_side = jnp.concatenate([jnp.arange(N_META, dtype=jnp.int32),
                                jnp.tile(N_META + PAST_LEN + jnp.arange(t_dec, dtype=jnp.int32), dbsz)])
    cos_s, sin_s = _rope_tables(pos_side)
    xs = jnp.concatenate([meta_tokens, x_sample.reshape(dbsz * t_dec, D_MODEL)], axis=0)
    spool16 = jnp.pad(state_pool, ((0, 0), (N_META - POOL_HIST, 0), (0, 0)))
    sconv8 = jnp.pad(state_conv, ((0, 0), (SUBLANES - (CONV_W - 1), 0), (0, 0)))
    tab_meta = _score_table(sinks, N_META, N_META, LANES)
    tab_dec = _score_table(sinks, t_dec, N_META + WINDOW + t_dec, 2 * LANES)
    y_side, k_side, v_side, p_side, up_side = _side_call(
        xs, cache_swa_k.reshape(dbsz, WINDOW, KV_W), cache_swa_v.reshape(dbsz, WINDOW, KV_W),
        cache_meta_k.reshape(dbsz, N_META, KV_W), cache_meta_v.reshape(dbsz, N_META, KV_W), spool16, sconv8,
        cos_s, sin_s, tab_meta, tab_dec, gmix, w_in_b, bgate, w_ao_b, wgrp_b, pscale, w_po_b, w_out_b,
        gffn, w_up_b, conv_w, convb, w_down_b, gfin)
    km, vm, p_meta = k_side[:N_META], v_side[:N_META], p_side[:N_META]
    up_meta8 = up_side[N_META - SUBLANES:N_META]

    cos_p, sin_p = _rope_tables(N_META + jnp.arange(seq, dtype=jnp.int32))
    q, k, v, p, gates = _inproj_call(x_prompt, cos_p, sin_p, gmix, w_in_b, bgate)
    tab = jnp.stack([
        _score_table(sinks, CHUNK, BAND + N_META, KEYS_PAD, masked_prefix=(WINDOW // CHUNK - c) * CHUNK)
        for c in range(N_TABS)])
    o_attn = _attn_call(q, k, v, km, vm, tab)
    x1 = _mixer_call(x_prompt, o_attn, p, p_meta, gates, w_ao_b, wgrp_b, pscale, w_po_b, w_out_b)
    y_prompt, up_tail = _ffn_call(x1, up_meta8, gffn, w_up_b, conv_w, convb, w_down_b, gfin)

    kv4 = lambda a, n: a.reshape(a.shape[0], n, N_KV_HEADS, HEAD_DIM)
    dec = lambda a: a[N_META:].reshape(dbsz, t_dec, -1)
    return (
        y_prompt,
        dec(y_side),
        kv4(k[:, seq - WINDOW:], WINDOW),
        kv4(v[:, seq - WINDOW:], WINDOW),
        jnp.broadcast_to(km.reshape(1, N_META, N_KV_HEADS, HEAD_DIM), (bsz, N_META, N_KV_HEADS, HEAD_DIM)),
        jnp.broadcast_to(vm.reshape(1, N_META, N_KV_HEADS, HEAD_DIM), (bsz, N_META, N_KV_HEADS, HEAD_DIM)),
        p[:, seq - POOL_HIST:],
        up_tail[:, SUBLANES - (CONV_W - 1):],
        kv4(dec(k_side), t_dec),
        kv4(dec(v_side), t_dec),
        dec(p_side)[:, t_dec - POOL_HIST:],
        dec(up_side)[:, t_dec - (CONV_W - 1):],
    )
```

```python
import functools

import jax
import jax.numpy as jnp
from jax import lax
from jax.experimental import pallas as pl
from jax.experimental.pallas import tpu as pltpu

D_MODEL = 1024
N_META = 16
CHUNK = 64
HEAD_DIM = 64
N_Q_HEADS = 16
N_KV_HEADS = 2
WINDOW = 128
ROPE_THETA = 10000.0
POOL_WINDOWS = (2, 4, 8, 16)
N_POOL_GROUPS = 4
POOL_WIDTH = D_MODEL // 2
POOL_GROUP_W = POOL_WIDTH // N_POOL_GROUPS
POOL_HIST = max(POOL_WINDOWS) - 1
Q_W = N_Q_HEADS * HEAD_DIM
KV_W = N_KV_HEADS * HEAD_DIM
GATE_W = 2 * D_MODEL
IN_W = Q_W + 2 * KV_W + POOL_WIDTH + GATE_W
D_FF = ((8 * D_MODEL // 3) + 127) // 128 * 128
CONV_W = 3
RMS_EPS = 1e-6
PAST_LEN = 1024

LANES = 128
SUBLANES = 8
Q_BLOCKS = Q_W // LANES
BLOCKS_PER_KV = Q_BLOCKS // N_KV_HEADS
NEG = -1e30

K_OFF = Q_W
V_OFF = Q_W + KV_W
P_OFF = Q_W + 2 * KV_W
G_OFF = P_OFF + POOL_WIDTH

F32 = jnp.float32
BF16 = jnp.bfloat16

VMEM_LIMIT = 56 * 1024 * 1024


def _rmsnorm(x, g):
    ms = jnp.mean(x * x, axis=-1, keepdims=True)
    return (x * lax.rsqrt(ms + RMS_EPS)) * g


def _dot(a, b):
    return jnp.dot(a, b, preferred_element_type=F32)


def _rope(x, cos, sin_signed):
    lane = lax.broadcasted_iota(jnp.int32, (x.shape[0], LANES), 1)
    first_half = (lane & (HEAD_DIM // 2)) == 0
    outs = []
    for b in range(x.shape[1] // LANES):
        xb = x[:, b * LANES:(b + 1) * LANES]
        partner = jnp.where(first_half, pltpu.roll(xb, LANES - HEAD_DIM // 2, 1), pltpu.roll(xb, HEAD_DIM // 2, 1))
        outs.append(xb * cos + partner * sin_signed)
    return outs[0] if len(outs) == 1 else jnp.concatenate(outs, axis=1)


def _kv_variants(kv):
    lane = lax.broadcasted_iota(jnp.int32, kv.shape, 1)
    low = lane < HEAD_DIM
    swapped = pltpu.roll(kv, HEAD_DIM, 1)
    zero = jnp.zeros_like(kv)
    return (
        (jnp.where(low, kv, zero).astype(BF16), jnp.where(low, zero, swapped).astype(BF16)),
        (jnp.where(low, swapped, zero).astype(BF16), jnp.where(low, zero, kv).astype(BF16)),
    )


def _attend_group(qg, kst, vst, tab):
    s = lax.dot_general(qg, kst, (((1,), (1,)), ((), ())), preferred_element_type=F32) + tab
    npad = kst.shape[0] // 2
    se, so = s[:, :npad], s[:, npad:]
    pe = jnp.exp(se - jnp.max(se, axis=1, keepdims=True))
    po = jnp.exp(so - jnp.max(so, axis=1, keepdims=True))
    le = jnp.sum(pe, axis=1, keepdims=True)
    lo = jnp.sum(po, axis=1, keepdims=True)
    p = jnp.concatenate([pe, po], axis=1).astype(BF16)
    o = _dot(p, vst)
    lane = lax.broadcasted_iota(jnp.int32, o.shape, 1)
    return o * jnp.where(lane < HEAD_DIM, 1.0 / le, 1.0 / lo)


def _pool_group(ext, g, cnt=None):
    w = POOL_WINDOWS[g]
    s = ext
    k = 1
    while k < w:
        s = s + pltpu.roll(s, k, 0)
        k *= 2
    cur = ext[N_META:]
    return s[N_META:] / (float(w) if cnt is None else cnt) - cur


def _pool_project(pm, wgrp_ref, pscale):
    outs = []
    for g in range(N_POOL_GROUPS):
        sl = slice(g * POOL_GROUP_W, (g + 1) * POOL_GROUP_W)
        outs.append(_dot(pm[:, sl].astype(BF16), wgrp_ref[g]))
    return (jnp.concatenate(outs, axis=1) * pscale).astype(BF16)


def _conv3(u, prev8, w, b):
    t = u.shape[0]
    ext = jnp.concatenate([prev8, u], axis=0)
    u1 = pltpu.roll(ext, 1, 0)[SUBLANES:SUBLANES + t]
    u2 = pltpu.roll(ext, 2, 0)[SUBLANES:SUBLANES + t]
    c = b + u2 * w[0:1]
    c = c + u1 * w[1:2]
    return c + u * w[2:3]


def _sigmoid(x):
    return 0.5 * jnp.tanh(0.5 * x) + 0.5


def _silu(x):
    half = 0.5 * x
    return half + half * jnp.tanh(half)


N_SIDE = N_META + 8 * 16
SIDE_FF_CHUNK = 256


def _side_kernel(xs_ref, ck_ref, cv_ref, cmk_ref, cmv_ref, spool_ref, sconv_ref, cos_ref, sin_ref, tabm_ref,
                 tabs_ref, gmix_ref, w_in_ref, bgate_ref, w_ao_ref, wgrp_ref, pscale_ref, w_po_ref, w_out_ref,
                 gffn_ref, w_up_ref, convw_ref, convb_ref, w_down_ref, gfin_ref,
                 y_ref, k_ref, v_ref, p_ref, up_ref):
    dec_b = ck_ref.shape[0]
    t_dec = (N_SIDE - N_META) // dec_b
    x = xs_ref[...]
    h = _rmsnorm(x, gmix_ref[...]).astype(BF16)
    cos, sin = cos_ref[...], sin_ref[...]
    q = _rope(_dot(h, w_in_ref[:, 0:Q_W]) * (HEAD_DIM ** -0.5), cos, sin).astype(BF16)
    k = _rope(_dot(h, w_in_ref[:, K_OFF:K_OFF + KV_W]), cos, sin)
    v = _dot(h, w_in_ref[:, V_OFF:V_OFF + KV_W])
    p = _dot(h, w_in_ref[:, P_OFF:P_OFF + POOL_WIDTH])
    gates = _sigmoid(_dot(h, w_in_ref[:, G_OFF:G_OFF + GATE_W]) + bgate_ref[...])
    k_ref[...] = k
    v_ref[...] = v
    p_ref[...] = p

    def attend_rows(r0, nrows, kx, vx, tab_ref):
        kvar, vvar = _kv_variants(kx), _kv_variants(vx)
        blocks = [None] * Q_BLOCKS
        for n in range(N_KV_HEADS):
            qg = jnp.concatenate([q[r0:r0 + nrows, (BLOCKS_PER_KV * n + bi) * LANES:(BLOCKS_PER_KV * n + bi + 1) * LANES]
                                  for bi in range(BLOCKS_PER_KV)], axis=0)
            kst = jnp.concatenate(kvar[n], axis=0)
            vst = jnp.concatenate(vvar[n], axis=0)
            o = _attend_group(qg, kst, vst, tab_ref[n])
            for bi in range(BLOCKS_PER_KV):
                blocks[BLOCKS_PER_KV * n + bi] = o[bi * nrows:(bi + 1) * nrows]
        return jnp.concatenate(blocks, axis=1)

    zpad = jnp.zeros((LANES - N_META, LANES), F32)
    o_rows = [attend_rows(0, N_META, jnp.concatenate([k[:N_META], zpad], axis=0),
                          jnp.concatenate([v[:N_META], zpad], axis=0), tabm_ref)]
    n_keys = N_META + WINDOW + t_dec
    zpad = jnp.zeros((2 * LANES - n_keys, LANES), F32)
    for b in range(dec_b):
        r0 = N_META + b * t_dec
        kx = jnp.concatenate([cmk_ref[b], ck_ref[b], k[r0:r0 + t_dec], zpad], axis=0)
        vx = jnp.concatenate([cmv_ref[b], cv_ref[b], v[r0:r0 + t_dec], zpad], axis=0)
        o_rows.append(attend_rows(r0, t_dec, kx, vx, tabs_ref))
    o_attn = jnp.concatenate(o_rows, axis=0).astype(BF16)

    row = lax.broadcasted_iota(jnp.int32, (N_META, LANES), 0)
    pm_rows = []
    for seg in range(1 + dec_b):
        if seg == 0:
            ext = jnp.concatenate([jnp.zeros((N_META, POOL_WIDTH), F32), p[:N_META]], axis=0)
        else:
            r0 = N_META + (seg - 1) * t_dec
            ext = jnp.concatenate([spool_ref[seg - 1], p[r0:r0 + t_dec]], axis=0)
        groups = []
        for g in range(N_POOL_GROUPS):
            cnt = jnp.minimum(POOL_WINDOWS[g], row + 1).astype(F32) if seg == 0 else None
            groups.append(_pool_group(ext[:, g * POOL_GROUP_W:(g + 1) * POOL_GROUP_W], g, cnt))
        pm_rows.append(jnp.concatenate(groups, axis=1))
    pm = jnp.concatenate(pm_rows, axis=0)

    pool = _pool_project(pm, wgrp_ref, pscale_ref[...])
    mix = gates[:, :D_MODEL] * _dot(o_attn, w_ao_ref[...]) + gates[:, D_MODEL:] * _dot(pool, w_po_ref[...])
    x1 = x + _dot(mix.astype(BF16), w_out_ref[...])

    hn = _rmsnorm(x1, gffn_ref[...]).astype(BF16)
    acc = jnp.zeros((N_SIDE, D_MODEL), F32)
    zprev = jnp.zeros((SUBLANES, SIDE_FF_CHUNK), F32)
    for c0 in range(0, D_FF, SIDE_FF_CHUNK):
        halves = []
        for off in (c0, D_FF + c0):
            cs = slice(off, off + SIDE_FF_CHUNK)
            u = _dot(hn, w_up_ref[:, cs])
            up_ref[:, cs] = u
            w, bias = convw_ref[:, cs], convb_ref[:, cs]
            segs = [_conv3(u[:N_META], zprev, w, bias)]
            for b in range(dec_b):
                r0 = N_META + b * t_dec
                segs.append(_conv3(u[r0:r0 + t_dec], sconv_ref[b, :, cs], w, bias))
            halves.append(jnp.concatenate(segs, axis=0))
        act = (_silu(halves[0]) * halves[1]).astype(BF16)
        acc = acc + _dot(act, w_down_ref[c0:c0 + SIDE_FF_CHUNK, :])
    y_ref[...] = _rmsnorm(x1 + acc, gfin_ref[...])


def _side_call(*args):
    out_shape = (
        jax.ShapeDtypeStruct((N_SIDE, D_MODEL), F32),
        jax.ShapeDtypeStruct((N_SIDE, KV_W), F32),
        jax.ShapeDtypeStruct((N_SIDE, KV_W), F32),
        jax.ShapeDtypeStruct((N_SIDE, POOL_WIDTH), F32),
        jax.ShapeDtypeStruct((N_SIDE, 2 * D_FF), F32),
    )
    return pl.pallas_call(
        _side_kernel,
        out_shape=out_shape,
        compiler_params=pltpu.CompilerParams(vmem_limit_bytes=VMEM_LIMIT),
        name="side_rows",
    )(*args)


TM_IN = 512
IN_SUB = 256
GATE_CHUNK = 512


def _inproj_kernel(x_ref, cos_ref, sin_ref, gmix_ref, w_in_ref, bgate_ref, q_ref, k_ref, v_ref, p_ref, g_ref):
    subs = [slice(r0, r0 + IN_SUB) for r0 in range(0, x_ref.shape[0], IN_SUB)]
    hs = [_rmsnorm(x_ref[rows], gmix_ref[...]).astype(BF16) for rows in subs]
    for rows, h in zip(subs, hs):
        cos, sin = cos_ref[rows], sin_ref[rows]
        for c0 in range(0, GATE_W, GATE_CHUNK):
            z = _dot(h, w_in_ref[:, G_OFF + c0:G_OFF + c0 + GATE_CHUNK]) + bgate_ref[:, c0:c0 + GATE_CHUNK]
            g_ref[rows, c0:c0 + GATE_CHUNK] = _sigmoid(z).astype(BF16)
        q_ref[rows] = _rope(_dot(h, w_in_ref[:, 0:Q_W]) * (HEAD_DIM ** -0.5), cos, sin).astype(BF16)
        p_ref[rows] = _dot(h, w_in_ref[:, P_OFF:P_OFF + POOL_WIDTH])
        kv = _dot(h, w_in_ref[:, K_OFF:K_OFF + 2 * KV_W])
        k_ref[rows] = _rope(kv[:, :KV_W], cos, sin)
        v_ref[rows] = kv[:, KV_W:]


def _const_spec(shape):
    nd = len(shape)
    return pl.BlockSpec(shape, lambda *_: (0,) * nd)


def _inproj_call(x, cos, sin, gmix, w_in, bgate):
    bsz, seq, _ = x.shape
    tm = TM_IN
    row = lambda w: pl.BlockSpec((None, tm, w), lambda b, i: (b, i, 0))
    return pl.pallas_call(
        _inproj_kernel,
        grid=(bsz, seq // tm),
        in_specs=[row(D_MODEL), pl.BlockSpec((tm, LANES), lambda b, i: (i, 0)),
                  pl.BlockSpec((tm, LANES), lambda b, i: (i, 0)),
                  _const_spec(gmix.shape), _const_spec(w_in.shape), _const_spec(bgate.shape)],
        out_specs=(row(Q_W), row(KV_W), row(KV_W), row(POOL_WIDTH), row(GATE_W)),
        out_shape=(jax.ShapeDtypeStruct((bsz, seq, Q_W), BF16), jax.ShapeDtypeStruct((bsz, seq, KV_W), F32),
                   jax.ShapeDtypeStruct((bsz, seq, KV_W), F32), jax.ShapeDtypeStruct((bsz, seq, POOL_WIDTH), F32),
                   jax.ShapeDtypeStruct((bsz, seq, GATE_W), BF16)),
        compiler_params=pltpu.CompilerParams(dimension_semantics=("arbitrary", "arbitrary"),
                                             vmem_limit_bytes=VMEM_LIMIT),
        name="in_proj",
    )(x, cos, sin, gmix, w_in, bgate)


TQ = 256
BAND = WINDOW + CHUNK
KEYS_PAD = 2 * LANES
N_TABS = WINDOW // CHUNK + 1


def _attn_kernel(q_ref, kc_ref, kp_ref, vc_ref, vp_ref, mk_ref, mv_ref, tab_ref, o_ref):
    i = pl.program_id(1)
    kvar = _kv_variants(jnp.concatenate([kp_ref[...], kc_ref[...]], axis=0))
    vvar = _kv_variants(jnp.concatenate([vp_ref[...], vc_ref[...]], axis=0))
    mkvar, mvvar = _kv_variants(mk_ref[...]), _kv_variants(mv_ref[...])
    zpad = jnp.zeros((KEYS_PAD - BAND - N_META, LANES), BF16)
    for j in range(TQ // CHUNK):
        tsel = jnp.minimum(i * (TQ // CHUNK) + j, N_TABS - 1)
        rows = slice(j * CHUNK, (j + 1) * CHUNK)
        band = slice(j * CHUNK, j * CHUNK + BAND)
        for n in range(N_KV_HEADS):
            kst = jnp.concatenate([kvar[n][0][band], mkvar[n][0], zpad, kvar[n][1][band], mkvar[n][1], zpad], axis=0)
            vst = jnp.concatenate([vvar[n][0][band], mvvar[n][0], zpad, vvar[n][1][band], mvvar[n][1], zpad], axis=0)
            qg = jnp.concatenate([q_ref[rows, (BLOCKS_PER_KV * n + bi) * LANES:(BLOCKS_PER_KV * n + bi + 1) * LANES]
                                  for bi in range(BLOCKS_PER_KV)], axis=0)
            o = _attend_group(qg, kst, vst, tab_ref[tsel, n])
            for bi in range(BLOCKS_PER_KV):
                blk = BLOCKS_PER_KV * n + bi
                o_ref[rows, blk * LANES:(blk + 1) * LANES] = o[bi * CHUNK:(bi + 1) * CHUNK].astype(BF16)


def _attn_call(q, k, v, mk, mv, tab):
    bsz, seq, _ = q.shape
    cur = lambda w: pl.BlockSpec((None, TQ, w), lambda b, i: (b, i, 0))
    prev = pl.BlockSpec((None, WINDOW, KV_W), lambda b, i: (b, jnp.maximum(i * (TQ // WINDOW) - 1, 0), 0))
    return pl.pallas_call(
        _attn_kernel,
        grid=(bsz, seq // TQ),
        in_specs=[cur(Q_W), cur(KV_W), prev, cur(KV_W), prev, _const_spec(mk.shape), _const_spec(mv.shape),
                  _const_spec(tab.shape)],
        out_specs=cur(Q_W),
        out_shape=jax.ShapeDtypeStruct((bsz, seq, Q_W), BF16),
        compiler_params=pltpu.CompilerParams(dimension_semantics=("arbitrary", "arbitrary"),
                                             vmem_limit_bytes=VMEM_LIMIT),
        name="swa_attention",
    )(q, k, k, v, v, mk, mv, tab)


TM_MIX = 512
MIX_SUB = 256


def _mixer_kernel(x_ref, o_ref, p_ref, pprev_ref, pmeta_ref, g_ref, w_ao_ref, wgrp_ref, pscale_ref, w_po_ref,
                  w_out_ref, x1_ref, pext_ref):
    i = pl.program_id(1)

    @pl.when(i == 0)
    def _():
        pext_ref[0:N_META] = pmeta_ref[...]

    @pl.when(i > 0)
    def _():
        pext_ref[0:N_META] = pprev_ref[...]

    pext_ref[N_META:] = p_ref[...]
    subs = [slice(r0, r0 + MIX_SUB) for r0 in range(0, x_ref.shape[0], MIX_SUB)]
    mixes = []
    for rows in subs:
        attn = _dot(o_ref[rows], w_ao_ref[...])
        ext_rows = slice(rows.start, rows.stop + N_META)
        pm = jnp.concatenate([_pool_group(pext_ref[ext_rows, g * POOL_GROUP_W:(g + 1) * POOL_GROUP_W], g)
                              for g in range(N_POOL_GROUPS)], axis=1)
        pool = _dot(_pool_project(pm, wgrp_ref, pscale_ref[...]), w_po_ref[...])
        mix = g_ref[rows, :D_MODEL].astype(F32) * attn + g_ref[rows, D_MODEL:].astype(F32) * pool
        mixes.append(mix.astype(BF16))
    for rows, mix in zip(subs, mixes):
        x1_ref[rows] = x_ref[rows] + _dot(mix, w_out_ref[...])


def _mixer_call(x, o_attn, p, p_meta, gates, w_ao, wgrp, pscale, w_po, w_out):
    bsz, seq, _ = x.shape
    tm = TM_MIX
    row = lambda w: pl.BlockSpec((None, tm, w), lambda b, i: (b, i, 0))
    pprev = pl.BlockSpec((None, N_META, POOL_WIDTH), lambda b, i: (b, jnp.maximum(i * (tm // N_META) - 1, 0), 0))
    return pl.pallas_call(
        _mixer_kernel,
        grid=(bsz, seq // tm),
        in_specs=[row(D_MODEL), row(Q_W), row(POOL_WIDTH), pprev, _const_spec(p_meta.shape), row(GATE_W),
                  _const_spec(w_ao.shape), _const_spec(wgrp.shape), _const_spec(pscale.shape),
                  _const_spec(w_po.shape), _const_spec(w_out.shape)],
        out_specs=row(D_MODEL),
        out_shape=jax.ShapeDtypeStruct((bsz, seq, D_MODEL), F32),
        scratch_shapes=[pltpu.VMEM((N_META + tm, POOL_WIDTH), F32)],
        compiler_params=pltpu.CompilerParams(dimension_semantics=("arbitrary", "arbitrary"),
                                             vmem_limit_bytes=VMEM_LIMIT),
        name="pool_mixer",
    )(x, o_attn, p, p, p_meta, gates, w_ao, wgrp, pscale, w_po, w_out)


TM_FFN = 256
FF_CHUNK = 256
FF_SKEW = 2


def _ffn_kernel(x1_ref, upmeta_ref, gffn_ref, w_up_ref, convw_ref, convb_ref, w_down_ref, gfin_ref, y_ref, carry_ref,
                *uext_refs):
    @pl.when(pl.program_id(1) == 0)
    def _():
        carry_ref[...] = upmeta_ref[...]

    x1 = x1_ref[...]
    tm = x1.shape[0]
    hn = _rmsnorm(x1, gffn_ref[...]).astype(BF16)
    n_chunks = D_FF // FF_CHUNK

    lane_blocks = FF_CHUNK // LANES

    def up_project(c):
        for half, off in enumerate((c * FF_CHUNK, D_FF + c * FF_CHUNK)):
            cs = slice(off, off + FF_CHUNK)
            u = _dot(hn, w_up_ref[:, cs])
            for lb in range(lane_blocks):
                ls = slice(lb * LANES, (lb + 1) * LANES)
                uext_refs[c][half * lane_blocks + lb, 0:SUBLANES] = carry_ref[:, off + lb * LANES:off + (lb + 1) * LANES]
                uext_refs[c][half * lane_blocks + lb, SUBLANES:] = u[:, ls]
            carry_ref[:, cs] = u[tm - SUBLANES:]

    def conv(c, half):
        outs = []
        for lb in range(lane_blocks):
            off = half * D_FF + c * FF_CHUNK + lb * LANES
            w, b = convw_ref[:, off:off + LANES], convb_ref[:, off:off + LANES]
            ext = uext_refs[c].at[half * lane_blocks + lb]
            cv = b + ext[SUBLANES - 2:SUBLANES - 2 + tm] * w[0:1]
            cv = cv + ext[SUBLANES - 1:SUBLANES - 1 + tm] * w[1:2]
            outs.append(cv + ext[SUBLANES:] * w[2:3])
        return jnp.concatenate(outs, axis=1)

    for c in range(FF_SKEW):
        up_project(c)
    acc = None
    for c in range(n_chunks):
        act = (_silu(conv(c, 0)) * conv(c, 1)).astype(BF16)
        if c + FF_SKEW < n_chunks:
            up_project(c + FF_SKEW)
        d = _dot(act, w_down_ref[c * FF_CHUNK:(c + 1) * FF_CHUNK, :])
        acc = d if acc is None else acc + d
    y_ref[...] = _rmsnorm(x1 + acc, gfin_ref[...])


def _ffn_call(x1, up_meta8, gffn, w_up, convw, convb, w_down, gfin):
    bsz, seq, _ = x1.shape
    tm = TM_FFN
    row = pl.BlockSpec((None, tm, D_MODEL), lambda b, i: (b, i, 0))
    return pl.pallas_call(
        _ffn_kernel,
        grid=(bsz, seq // tm),
        in_specs=[row, _const_spec(up_meta8.shape), _const_spec(gffn.shape), _const_spec(w_up.shape),
                  _const_spec(convw.shape), _const_spec(convb.shape), _const_spec(w_down.shape),
                  _const_spec(gfin.shape)],
        out_specs=(row, pl.BlockSpec((None, SUBLANES, 2 * D_FF), lambda b, i: (b, 0, 0))),
        out_shape=(jax.ShapeDtypeStruct((bsz, seq, D_MODEL), F32),
                   jax.ShapeDtypeStruct((bsz, SUBLANES, 2 * D_FF), F32)),
        scratch_shapes=[pltpu.VMEM((2 * FF_CHUNK // LANES, SUBLANES + tm, LANES), F32)
                        for _ in range(D_FF // FF_CHUNK)],
        compiler_params=pltpu.CompilerParams(dimension_semantics=("arbitrary", "arbitrary"),
                                             vmem_limit_bytes=VMEM_LIMIT),
        name="conv_ffn",
    )(x1, up_meta8, gffn, w_up, convw, convb, w_down, gfin)


def _rope_tables(pos):
    half = HEAD_DIM // 2
    inv = ROPE_THETA ** (-jnp.arange(half, dtype=F32) / half)
    ang = pos.astype(F32)[:, None] * inv[None, :]
    cos, sin = jnp.cos(ang), jnp.sin(ang)
    return jnp.tile(cos, (1, LANES // half)), jnp.tile(jnp.concatenate([-sin, sin], axis=1), (1, LANES // HEAD_DIM))


def _score_table(sinks, rows_per_block, n_keys, keys_pad, masked_prefix=0):
    col = jnp.arange(keys_pad)
    base = jnp.where((col >= masked_prefix) & (col < n_keys), 0.0, NEG).astype(F32)
    head = (2 * jnp.arange(Q_BLOCKS)[:, None] + jnp.arange(2)[None, :])
    tab = jnp.where(col[None, None, :] == n_keys, sinks.astype(F32)[head][:, :, None], base[None, None, :])
    tab = tab.reshape(N_KV_HEADS, BLOCKS_PER_KV, 1, 2 * keys_pad)
    tab = jnp.broadcast_to(tab, (N_KV_HEADS, BLOCKS_PER_KV, rows_per_block, 2 * keys_pad))
    return tab.reshape(N_KV_HEADS, BLOCKS_PER_KV * rows_per_block, 2 * keys_pad)


def kernel(x_prompt, x_sample, cache_swa_k, cache_swa_v, cache_meta_k, cache_meta_v, state_pool, state_conv,
           meta_tokens, g_norm_mix, w_in, b_gate, sinks, w_attn_o, w_pool_grp, pool_scale, w_pool_o, w_out,
           g_norm_ffn, w_up, conv_w, conv_b, w_down, g_norm_final):
    bsz, seq, _ = x_prompt.shape
    dbsz, t_dec, _ = x_sample.shape
    row2 = lambda a: a.reshape(1, -1)
    w_in_b, w_ao_b, wgrp_b, w_po_b = w_in.astype(BF16), w_attn_o.astype(BF16), w_pool_grp.astype(BF16), w_pool_o.astype(BF16)
    w_out_b, w_up_b, w_down_b = w_out.astype(BF16), w_up.astype(BF16), w_down.astype(BF16)
    gmix, gffn, gfin = row2(g_norm_mix), row2(g_norm_ffn), row2(g_norm_final)
    bgate, pscale, convb = row2(b_gate), row2(pool_scale), row2(conv_b)

    pos_side = jnp.concatenate([jnp.arange(N_META, dtype=jnp.int32),
                                jnp.tile(N_META + PAST_LEN + jnp.arange(t_dec, dtype=jnp.int32), dbsz)])
    cos_s, sin_s = _rope_tables(pos_side)
    xs = jnp.concatenate([meta_tokens, x_sample.reshape(dbsz * t_dec, D_MODEL)], axis=0)
    spool16 = jnp.pad(state_pool, ((0, 0), (N_META - POOL_HIST, 0), (0, 0)))
    sconv8 = jnp.pad(state_conv, ((0, 0), (SUBLANES - (CONV_W - 1), 0), (0, 0)))
    tab_meta = _score_table(sinks, N_META, N_META, LANES)
    tab_dec = _score_table(sinks, t_dec, N_META + WINDOW + t_dec, 2 * LANES)
    y_side, k_side, v_side, p_side, up_side = _side_call(
        xs, cache_swa_k.reshape(dbsz, WINDOW, KV_W), cache_swa_v.reshape(dbsz, WINDOW, KV_W),
        cache_meta_k.reshape(dbsz, N_META, KV_W), cache_meta_v.reshape(dbsz, N_META, KV_W), spool16, sconv8,
        cos_s, sin_s, tab_meta, tab_dec, gmix, w_in_b, bgate, w_ao_b, wgrp_b, pscale, w_po_b, w_out_b,
        gffn, w_up_b, conv_w, convb, w_down_b, gfin)
    km, vm, p_meta = k_side[:N_META], v_side[:N_META], p_side[:N_META]
    up_meta8 = up_side[N_META - SUBLANES:N_META]

    cos_p, sin_p = _rope_tables(N_META + jnp.arange(seq, dtype=jnp.int32))
    q, k, v, p, gates = _inproj_call(x_prompt, cos_p, sin_p, gmix, w_in_b, bgate)
    tab = jnp.stack([
        _score_table(sinks, CHUNK, BAND + N_META, KEYS_PAD, masked_prefix=(WINDOW // CHUNK - c) * CHUNK)
        for c in range(N_TABS)])
    o_attn = _attn_call(q, k, v, km, vm, tab)
    x1 = _mixer_call(x_prompt, o_attn, p, p_meta, gates, w_ao_b, wgrp_b, pscale, w_po_b, w_out_b)
    y_prompt, up_tail = _ffn_call(x1, up_meta8, gffn, w_up_b, conv_w, convb, w_down_b, gfin)

    kv4 = lambda a, n: a.reshape(a.shape[0], n, N_KV_HEADS, HEAD_DIM)
    dec = lambda a: a[N_META:].reshape(dbsz, t_dec, -1)
    return (
        y_prompt,
        dec(y_side),
        kv4(k[:, seq - WINDOW:], WINDOW),
        kv4(v[:, seq - WINDOW:], WINDOW),
        jnp.broadcast_to(km.reshape(1, N_META, N_KV_HEADS, HEAD_DIM), (bsz, N_META, N_KV_HEADS, HEAD_DIM)),
        jnp.broadcast_to(vm.reshape(1, N_META, N_KV_HEADS, HEAD_DIM), (bsz, N_META, N_KV_HEADS, HEAD_DIM)),
        p[:, seq - POOL_HIST:],
        up_tail[:, SUBLANES - (CONV_W - 1):],
        kv4(dec(k_side), t_dec),
        kv4(dec(v_side), t_dec),
        dec(p_side)[:, t_dec - POOL_HIST:],
        dec(up_side)[:, t_dec - (CONV_W - 1):],
    )
```

```python
import functools

import jax
import jax.numpy as jnp
from jax import lax
from jax.experimental import pallas as pl
from jax.experimental.pallas import tpu as pltpu

D_MODEL = 1024
N_META = 16
CHUNK = 64
HEAD_DIM = 64
N_Q_HEADS = 16
N_KV_HEADS = 2
WINDOW = 128
ROPE_THETA = 10000.0
POOL_WINDOWS = (2, 4, 8, 16)
N_POOL_GROUPS = 4
POOL_WIDTH = D_MODEL // 2
POOL_GROUP_W = POOL_WIDTH // N_POOL_GROUPS
POOL_HIST = max(POOL_WINDOWS) - 1
Q_W = N_Q_HEADS * HEAD_DIM
KV_W = N_KV_HEADS * HEAD_DIM
GATE_W = 2 * D_MODEL
IN_W = Q_W + 2 * KV_W + POOL_WIDTH + GATE_W
D_FF = ((8 * D_MODEL // 3) + 127) // 128 * 128
CONV_W = 3
RMS_EPS = 1e-6
PAST_LEN = 1024

LANES = 128
SUBLANES = 8
Q_BLOCKS = Q_W // LANES
BLOCKS_PER_KV = Q_BLOCKS // N_KV_HEADS
NEG = -1e30

K_OFF = Q_W
V_OFF = Q_W + KV_W
P_OFF = Q_W + 2 * KV_W
G_OFF = P_OFF + POOL_WIDTH

F32 = jnp.float32
BF16 = jnp.bfloat16

VMEM_LIMIT = 56 * 1024 * 1024


def _rmsnorm(x, g):
    ms = jnp.mean(x * x, axis=-1, keepdims=True)
    return (x * lax.rsqrt(ms + RMS_EPS)) * g


def _dot(a, b):
    return jnp.dot(a, b, preferred_element_type=F32)


def _rope(x, cos, sin_signed):
    lane = lax.broadcasted_iota(jnp.int32, (x.shape[0], LANES), 1)
    first_half = (lane & (HEAD_DIM // 2)) == 0
    outs = []
    for b in range(x.shape[1] // LANES):
        xb = x[:, b * LANES:(b + 1) * LANES]
        partner = jnp.where(first_half, pltpu.roll(xb, LANES - HEAD_DIM // 2, 1), pltpu.roll(xb, HEAD_DIM // 2, 1))
        outs.append(xb * cos + partner * sin_signed)
    return outs[0] if len(outs) == 1 else jnp.concatenate(outs, axis=1)


def _kv_variants(kv):
    lane = lax.broadcasted_iota(jnp.int32, kv.shape, 1)
    low = lane < HEAD_DIM
    swapped = pltpu.roll(kv, HEAD_DIM, 1)
    zero = jnp.zeros_like(kv)
    return (
        (jnp.where(low, kv, zero).astype(BF16), jnp.where(low, zero, swapped).astype(BF16)),
        (jnp.where(low, swapped, zero).astype(BF16), jnp.where(low, zero, kv).astype(BF16)),
    )


def _attend_group(qg, kst, vst, tab):
    s = lax.dot_general(qg, kst, (((1,), (1,)), ((), ())), preferred_element_type=F32) + tab
    npad = kst.shape[0] // 2
    se, so = s[:, :npad], s[:, npad:]
    pe = jnp.exp(se - jnp.max(se, axis=1, keepdims=True))
    po = jnp.exp(so - jnp.max(so, axis=1, keepdims=True))
    le = jnp.sum(pe, axis=1, keepdims=True)
    lo = jnp.sum(po, axis=1, keepdims=True)
    p = jnp.concatenate([pe, po], axis=1).astype(BF16)
    o = _dot(p, vst)
    lane = lax.broadcasted_iota(jnp.int32, o.shape, 1)
    return o * jnp.where(lane < HEAD_DIM, 1.0 / le, 1.0 / lo)


def _pool_group(ext, g, cnt=None):
    w = POOL_WINDOWS[g]
    s = ext
    k = 1
    while k < w:
        s = s + pltpu.roll(s, k, 0)
        k *= 2
    cur = ext[N_META:]
    return s[N_META:] / (float(w) if cnt is None else cnt) - cur


def _pool_project(pm, wgrp_ref, pscale):
    outs = []
    for g in range(N_POOL_GROUPS):
        sl = slice(g * POOL_GROUP_W, (g + 1) * POOL_GROUP_W)
        outs.append(_dot(pm[:, sl].astype(BF16), wgrp_ref[g]))
    return (jnp.concatenate(outs, axis=1) * pscale).astype(BF16)


def _conv3(u, prev8, w, b):
    t = u.shape[0]
    ext = jnp.concatenate([prev8, u], axis=0)
    u1 = pltpu.roll(ext, 1, 0)[SUBLANES:SUBLANES + t]
    u2 = pltpu.roll(ext, 2, 0)[SUBLANES:SUBLANES + t]
    c = b + u2 * w[0:1]
    c = c + u1 * w[1:2]
    return c + u * w[2:3]


def _sigmoid(x):
    return 0.5 * jnp.tanh(0.5 * x) + 0.5


def _silu(x):
    half = 0.5 * x
    return half + half * jnp.tanh(half)


N_SIDE = N_META + 8 * 16
SIDE_FF_CHUNK = 256


def _side_kernel(xs_ref, ck_ref, cv_ref, cmk_ref, cmv_ref, spool_ref, sconv_ref, cos_ref, sin_ref, tabm_ref,
                 tabs_ref, gmix_ref, w_in_ref, bgate_ref, w_ao_ref, wgrp_ref, pscale_ref, w_po_ref, w_out_ref,
                 gffn_ref, w_up_ref, convw_ref, convb_ref, w_down_ref, gfin_ref,
                 y_ref, k_ref, v_ref, p_ref, up_ref):
    dec_b = ck_ref.shape[0]
    t_dec = (N_SIDE - N_META) // dec_b
    x = xs_ref[...]
    h = _rmsnorm(x, gmix_ref[...]).astype(BF16)
    cos, sin = cos_ref[...], sin_ref[...]
    q = _rope(_dot(h, w_in_ref[:, 0:Q_W]) * (HEAD_DIM ** -0.5), cos, sin).astype(BF16)
    k = _rope(_dot(h, w_in_ref[:, K_OFF:K_OFF + KV_W]), cos, sin)
    v = _dot(h, w_in_ref[:, V_OFF:V_OFF + KV_W])
    p = _dot(h, w_in_ref[:, P_OFF:P_OFF + POOL_WIDTH])
    gates = _sigmoid(_dot(h, w_in_ref[:, G_OFF:G_OFF + GATE_W]) + bgate_ref[...])
    k_ref[...] = k
    v_ref[...] = v
    p_ref[...] = p

    def attend_rows(r0, nrows, kx, vx, tab_ref):
        kvar, vvar = _kv_variants(kx), _kv_variants(vx)
        blocks = [None] * Q_BLOCKS
        for n in range(N_KV_HEADS):
            qg = jnp.concatenate([q[r0:r0 + nrows, (BLOCKS_PER_KV * n + bi) * LANES:(BLOCKS_PER_KV * n + bi + 1) * LANES]
                                  for bi in range(BLOCKS_PER_KV)], axis=0)
            kst = jnp.concatenate(kvar[n], axis=0)
            vst = jnp.concatenate(vvar[n], axis=0)
            o = _attend_group(qg, kst, vst, tab_ref[n])
            for bi in range(BLOCKS_PER_KV):
                blocks[BLOCKS_PER_KV * n + bi] = o[bi * nrows:(bi + 1) * nrows]
        return jnp.concatenate(blocks, axis=1)

    zpad = jnp.zeros((LANES - N_META, LANES), F32)
    o_rows = [attend_rows(0, N_META, jnp.concatenate([k[:N_META], zpad], axis=0),
                          jnp.concatenate([v[:N_META], zpad], axis=0), tabm_ref)]
    n_keys = N_META + WINDOW + t_dec
    zpad = jnp.zeros((2 * LANES - n_keys, LANES), F32)
    for b in range(dec_b):
        r0 = N_META + b * t_dec
        kx = jnp.concatenate([cmk_ref[b], ck_ref[b], k[r0:r0 + t_dec], zpad], axis=0)
        vx = jnp.concatenate([cmv_ref[b], cv_ref[b], v[r0:r0 + t_dec], zpad], axis=0)
        o_rows.append(attend_rows(r0, t_dec, kx, vx, tabs_ref))
    o_attn = jnp.concatenate(o_rows, axis=0).astype(BF16)

    row = lax.broadcasted_iota(jnp.int32, (N_META, LANES), 0)
    pm_rows = []
    for seg in range(1 + dec_b):
        if seg == 0:
            ext = jnp.concatenate([jnp.zeros((N_META, POOL_WIDTH), F32), p[:N_META]], axis=0)
        else:
            r0 = N_META + (seg - 1) * t_dec
            ext = jnp.concatenate([spool_ref[seg - 1], p[r0:r0 + t_dec]], axis=0)
        groups = []
        for g in range(N_POOL_GROUPS):
            cnt = jnp.minimum(POOL_WINDOWS[g], row + 1).astype(F32) if seg == 0 else None
            groups.append(_pool_group(ext[:, g * POOL_GROUP_W:(g + 1) * POOL_GROUP_W], g, cnt))
        pm_rows.append(jnp.concatenate(groups, axis=1))
    pm = jnp.concatenate(pm_rows, axis=0)

    pool = _pool_project(pm, wgrp_ref, pscale_ref[...])
    mix = gates[:, :D_MODEL] * _dot(o_attn, w_ao_ref[...]) + gates[:, D_MODEL:] * _dot(pool, w_po_ref[...])
    x1 = x + _dot(mix.astype(BF16), w_out_ref[...])

    hn = _rmsnorm(x1, gffn_ref[...]).astype(BF16)
    acc = jnp.zeros((N_SIDE, D_MODEL), F32)
    zprev = jnp.zeros((SUBLANES, SIDE_FF_CHUNK), F32)
    for c0 in range(0, D_FF, SIDE_FF_CHUNK):
        halves = []
        for off in (c0, D_FF + c0):
            cs = slice(off, off + SIDE_FF_CHUNK)
            u = _dot(hn, w_up_ref[:, cs])
            up_ref[:, cs] = u
            w, bias = convw_ref[:, cs], convb_ref[:, cs]
            segs = [_conv3(u[:N_META], zprev, w, bias)]
            for b in range(dec_b):
                r0 = N_META + b * t_dec
                segs.append(_conv3(u[r0:r0 + t_dec], sconv_ref[b, :, cs], w, bias))
            halves.append(jnp.concatenate(segs, axis=0))
        act = (_silu(halves[0]) * halves[1]).astype(BF16)
        acc = acc + _dot(act, w_down_ref[c0:c0 + SIDE_FF_CHUNK, :])
    y_ref[...] = _rmsnorm(x1 + acc, gfin_ref[...])


def _side_call(*args):
    out_shape = (
        jax.ShapeDtypeStruct((N_SIDE, D_MODEL), F32),
        jax.ShapeDtypeStruct((N_SIDE, KV_W), F32),
        jax.ShapeDtypeStruct((N_SIDE, KV_W), F32),
        jax.ShapeDtypeStruct((N_SIDE, POOL_WIDTH), F32),
        jax.ShapeDtypeStruct((N_SIDE, 2 * D_FF), F32),
    )
    return pl.pallas_call(
        _side_kernel,
        out_shape=out_shape,
        compiler_params=pltpu.CompilerParams(vmem_limit_bytes=VMEM_LIMIT),
        name="side_rows",
    )(*args)


TM_IN = 512
IN_SUB = 256
GATE_CHUNK = 512


def _inproj_kernel(x_ref, cos_ref, sin_ref, gmix_ref, w_in_ref, bgate_ref, q_ref, k_ref, v_ref, p_ref, g_ref):
    subs = [slice(r0, r0 + IN_SUB) for r0 in range(0, x_ref.shape[0], IN_SUB)]
    hs = [_rmsnorm(x_ref[rows], gmix_ref[...]).astype(BF16) for rows in subs]
    for rows, h in zip(subs, hs):
        cos, sin = cos_ref[rows], sin_ref[rows]
        for c0 in range(0, GATE_W, GATE_CHUNK):
            z = _dot(h, w_in_ref[:, G_OFF + c0:G_OFF + c0 + GATE_CHUNK]) + bgate_ref[:, c0:c0 + GATE_CHUNK]
            g_ref[rows, c0:c0 + GATE_CHUNK] = _sigmoid(z).astype(BF16)
        q_ref[rows] = _rope(_dot(h, w_in_ref[:, 0:Q_W]) * (HEAD_DIM ** -0.5), cos, sin).astype(BF16)
        p_ref[rows] = _dot(h, w_in_ref[:, P_OFF:P_OFF + POOL_WIDTH])
        kv = _dot(h, w_in_ref[:, K_OFF:K_OFF + 2 * KV_W])
        k_ref[rows] = _rope(kv[:, :KV_W], cos, sin)
        v_ref[rows] = kv[:, KV_W:]


def _const_spec(shape):
    nd = len(shape)
    return pl.BlockSpec(shape, lambda *_: (0,) * nd)


def _inproj_call(x, cos, sin, gmix, w_in, bgate):
    bsz, seq, _ = x.shape
    tm = TM_IN
    row = lambda w: pl.BlockSpec((None, tm, w), lambda b, i: (b, i, 0))
    return pl.pallas_call(
        _inproj_kernel,
        grid=(bsz, seq // tm),
        in_specs=[row(D_MODEL), pl.BlockSpec((tm, LANES), lambda b, i: (i, 0)),
                  pl.BlockSpec((tm, LANES), lambda b, i: (i, 0)),
                  _const_spec(gmix.shape), _const_spec(w_in.shape), _const_spec(bgate.shape)],
        out_specs=(row(Q_W), row(KV_W), row(KV_W), row(POOL_WIDTH), row(GATE_W)),
        out_shape=(jax.ShapeDtypeStruct((bsz, seq, Q_W), BF16), jax.ShapeDtypeStruct((bsz, seq, KV_W), F32),
                   jax.ShapeDtypeStruct((bsz, seq, KV_W), F32), jax.ShapeDtypeStruct((bsz, seq, POOL_WIDTH), F32),
                   jax.ShapeDtypeStruct((bsz, seq, GATE_W), BF16)),
        compiler_params=pltpu.CompilerParams(dimension_semantics=("arbitrary", "arbitrary"),
                                             vmem_limit_bytes=VMEM_LIMIT),
        name="in_proj",
    )(x, cos, sin, gmix, w_in, bgate)


TQ = 256
BAND = WINDOW + CHUNK
KEYS_PAD = 2 * LANES
N_TABS = WINDOW // CHUNK + 1


def _attn_kernel(q_ref, kc_ref, kp_ref, vc_ref, vp_ref, mk_ref, mv_ref, tab_ref, o_ref):
    i = pl.program_id(1)
    kvar = _kv_variants(jnp.concatenate([kp_ref[...], kc_ref[...]], axis=0))
    vvar = _kv_variants(jnp.concatenate([vp_ref[...], vc_ref[...]], axis=0))
    mkvar, mvvar = _kv_variants(mk_ref[...]), _kv_variants(mv_ref[...])
    zpad = jnp.zeros((KEYS_PAD - BAND - N_META, LANES), BF16)
    for j in range(TQ // CHUNK):
        tsel = jnp.minimum(i * (TQ // CHUNK) + j, N_TABS - 1)
        rows = slice(j * CHUNK, (j + 1) * CHUNK)
        band = slice(j * CHUNK, j * CHUNK + BAND)
        for n in range(N_KV_HEADS):
            kst = jnp.concatenate([kvar[n][0][band], mkvar[n][0], zpad, kvar[n][1][band], mkvar[n][1], zpad], axis=0)
            vst = jnp.concatenate([vvar[n][0][band], mvvar[n][0], zpad, vvar[n][1][band], mvvar[n][1], zpad], axis=0)
            qg = jnp.concatenate([q_ref[rows, (BLOCKS_PER_KV * n + bi) * LANES:(BLOCKS_PER_KV * n + bi + 1) * LANES]
                                  for bi in range(BLOCKS_PER_KV)], axis=0)
            o = _attend_group(qg, kst, vst, tab_ref[tsel, n])
            for bi in range(BLOCKS_PER_KV):
                blk = BLOCKS_PER_KV * n + bi
                o_ref[rows, blk * LANES:(blk + 1) * LANES] = o[bi * CHUNK:(bi + 1) * CHUNK].astype(BF16)


def _attn_call(q, k, v, mk, mv, tab):
    bsz, seq, _ = q.shape
    cur = lambda w: pl.BlockSpec((None, TQ, w), lambda b, i: (b, i, 0))
    prev = pl.BlockSpec((None, WINDOW, KV_W), lambda b, i: (b, jnp.maximum(i * (TQ // WINDOW) - 1, 0), 0))
    return pl.pallas_call(
        _attn_kernel,
        grid=(bsz, seq // TQ),
        in_specs=[cur(Q_W), cur(KV_W), prev, cur(KV_W), prev, _const_spec(mk.shape), _const_spec(mv.shape),
                  _const_spec(tab.shape)],
        out_specs=cur(Q_W),
        out_shape=jax.ShapeDtypeStruct((bsz, seq, Q_W), BF16),
        compiler_params=pltpu.CompilerParams(dimension_semantics=("arbitrary", "arbitrary"),
                                             vmem_limit_bytes=VMEM_LIMIT),
        name="swa_attention",
    )(q, k, k, v, v, mk, mv, tab)


TM_POST = 512
POST_SUB = 256
FF_CHUNK = 256
FF_SKEW = 2
FF_LANE_BLOCKS = FF_CHUNK // LANES


def _post_kernel(x_ref, o_ref, p_ref, pprev_ref, pmeta_ref, g_ref, upmeta_ref, w_ao_ref, wgrp_ref, pscale_ref,
                 w_po_ref, w_out_ref, gffn_ref, w_up_ref, convw_ref, convb_ref, w_down_ref, gfin_ref,
                 y_ref, carry_ref, pext_ref, x1_ref, hn_ref, *uext_refs):
    i = pl.program_id(1)
    tm = x_ref.shape[0]
    n_chunks = D_FF // FF_CHUNK

    @pl.when(i == 0)
    def _():
        pext_ref[0:N_META] = pmeta_ref[...]
        carry_ref[...] = upmeta_ref[...]

    @pl.when(i > 0)
    def _():
        pext_ref[0:N_META] = pprev_ref[...]

    pext_ref[N_META:] = p_ref[...]
    subs = [slice(r0, r0 + POST_SUB) for r0 in range(0, tm, POST_SUB)]

    mixes = []
    for rows in subs:
        attn = _dot(o_ref[rows], w_ao_ref[...])
        ext_rows = slice(rows.start, rows.stop + N_META)
        pm = jnp.concatenate([_pool_group(pext_ref[ext_rows, g * POOL_GROUP_W:(g + 1) * POOL_GROUP_W], g)
                              for g in range(N_POOL_GROUPS)], axis=1)
        pool = _dot(_pool_project(pm, wgrp_ref, pscale_ref[...]), w_po_ref[...])
        mix = g_ref[rows, :D_MODEL].astype(F32) * attn + g_ref[rows, D_MODEL:].astype(F32) * pool
        mixes.append(mix.astype(BF16))
    for rows, mix in zip(subs, mixes):
        x1 = x_ref[rows] + _dot(mix, w_out_ref[...])
        x1_ref[rows] = x1
        hn_ref[rows] = _rmsnorm(x1, gffn_ref[...]).astype(BF16)

    def up_project(rows, c):
        for half, off in enumerate((c * FF_CHUNK, D_FF + c * FF_CHUNK)):
            u = _dot(hn_ref[rows], w_up_ref[:, off:off + FF_CHUNK])
            for lb in range(FF_LANE_BLOCKS):
                blk = uext_refs[c].at[half * FF_LANE_BLOCKS + lb]
                if rows.start == 0:
                    blk[0:SUBLANES] = carry_ref[:, off + lb * LANES:off + (lb + 1) * LANES]
                blk[SUBLANES + rows.start:SUBLANES + rows.stop] = u[:, lb * LANES:(lb + 1) * LANES]
            if rows.stop == tm:
                carry_ref[:, off:off + FF_CHUNK] = u[POST_SUB - SUBLANES:]

    def conv(rows, c, half):
        outs = []
        for lb in range(FF_LANE_BLOCKS):
            off = half * D_FF + c * FF_CHUNK + lb * LANES
            w, b = convw_ref[:, off:off + LANES], convb_ref[:, off:off + LANES]
            ext = uext_refs[c].at[half * FF_LANE_BLOCKS + lb]
            base = SUBLANES + rows.start
            cv = b + ext[base - 2:base - 2 + POST_SUB] * w[0:1]
            cv = cv + ext[base - 1:base - 1 + POST_SUB] * w[1:2]
            outs.append(cv + ext[base:base + POST_SUB] * w[2:3])
        return jnp.concatenate(outs, axis=1)

    steps = [(rows, c) for rows in subs for c in range(n_chunks)]
    for step in steps[:FF_SKEW]:
        up_project(*step)
    acc = None
    for j, (rows, c) in enumerate(steps):
        act = (_silu(conv(rows, c, 0)) * conv(rows, c, 1)).astype(BF16)
        if j + FF_SKEW < len(steps):
            up_project(*steps[j + FF_SKEW])
        d = _dot(act, w_down_ref[c * FF_CHUNK:(c + 1) * FF_CHUNK, :])
        acc = d if c == 0 else acc + d
        if c == n_chunks - 1:
            y_ref[rows] = _rmsnorm(x1_ref[rows] + acc, gfin_ref[...])


def _post_call(x, o_attn, p, p_meta, gates, up_meta8, w_ao, wgrp, pscale, w_po, w_out, gffn, w_up, convw, convb,
               w_down, gfin):
    bsz, seq, _ = x.shape
    tm = TM_POST
    row = lambda w: pl.BlockSpec((None, tm, w), lambda b, i: (b, i, 0))
    pprev = pl.BlockSpec((None, N_META, POOL_WIDTH), lambda b, i: (b, jnp.maximum(i * (tm // N_META) - 1, 0), 0))
    consts = (p_meta,), (up_meta8, w_ao, wgrp, pscale, w_po, w_out, gffn, w_up, convw, convb, w_down, gfin)
    return pl.pallas_call(
        _post_kernel,
        grid=(bsz, seq // tm),
        in_specs=[row(D_MODEL), row(Q_W), row(POOL_WIDTH), pprev, _const_spec(p_meta.shape), row(GATE_W)]
                 + [_const_spec(a.shape) for a in consts[1]],
        out_specs=(row(D_MODEL), pl.BlockSpec((None, SUBLANES, 2 * D_FF), lambda b, i: (b, 0, 0))),
        out_shape=(jax.ShapeDtypeStruct((bsz, seq, D_MODEL), F32),
                   jax.ShapeDtypeStruct((bsz, SUBLANES, 2 * D_FF), F32)),
        scratch_shapes=[pltpu.VMEM((N_META + tm, POOL_WIDTH), F32), pltpu.VMEM((tm, D_MODEL), F32),
                        pltpu.VMEM((tm, D_MODEL), BF16)]
                       + [pltpu.VMEM((2 * FF_LANE_BLOCKS, SUBLANES + tm, LANES), F32)
                          for _ in range(D_FF // FF_CHUNK)],
        compiler_params=pltpu.CompilerParams(dimension_semantics=("arbitrary", "arbitrary"),
                                             vmem_limit_bytes=VMEM_LIMIT),
        name="mixer_ffn",
    )(x, o_attn, p, p, p_meta, gates, *consts[1])


def _rope_tables(pos):
    half = HEAD_DIM // 2
    inv = ROPE_THETA ** (-jnp.arange(half, dtype=F32) / half)
    ang = pos.astype(F32)[:, None] * inv[None, :]
    cos, sin = jnp.cos(ang), jnp.sin(ang)
    return jnp.tile(cos, (1, LANES // half)), jnp.tile(jnp.concatenate([-sin, sin], axis=1), (1, LANES // HEAD_DIM))


def _score_table(sinks, rows_per_block, n_keys, keys_pad, masked_prefix=0):
    col = jnp.arange(keys_pad)
    base = jnp.where((col >= masked_prefix) & (col < n_keys), 0.0, NEG).astype(F32)
    head = (2 * jnp.arange(Q_BLOCKS)[:, None] + jnp.arange(2)[None, :])
    tab = jnp.where(col[None, None, :] == n_keys, sinks.astype(F32)[head][:, :, None], base[None, None, :])
    tab = tab.reshape(N_KV_HEADS, BLOCKS_PER_KV, 1, 2 * keys_pad)
    tab = jnp.broadcast_to(tab, (N_KV_HEADS, BLOCKS_PER_KV, rows_per_block, 2 * keys_pad))
    return tab.reshape(N_KV_HEADS, BLOCKS_PER_KV * rows_per_block, 2 * keys_pad)


def kernel(x_prompt, x_sample, cache_swa_k, cache_swa_v, cache_meta_k, cache_meta_v, state_pool, state_conv,
           meta_tokens, g_norm_mix, w_in, b_gate, sinks, w_attn_o, w_pool_grp, pool_scale, w_pool_o, w_out,
           g_norm_ffn, w_up, conv_w, conv_b, w_down, g_norm_final):
    bsz, seq, _ = x_prompt.shape
    dbsz, t_dec, _ = x_sample.shape
    row2 = lambda a: a.reshape(1, -1)
    w_in_b, w_ao_b, wgrp_b, w_po_b = w_in.astype(BF16), w_attn_o.astype(BF16), w_pool_grp.astype(BF16), w_pool_o.astype(BF16)
    w_out_b, w_up_b, w_down_b = w_out.astype(BF16), w_up.astype(BF16), w_down.astype(BF16)
    gmix, gffn, gfin = row2(g_norm_mix), row2(g_norm_ffn), row2(g_norm_final)
    bgate, pscale, convb = row2(b_gate), row2(pool_scale), row2(conv_b)

    pos_side = jnp.concatenate([jnp.arange(N_META, dtype=jnp.int32),
                                jnp.tile(N_META + PAST_LEN + jnp.arange(t_dec, dtype=jnp.int32), dbsz)])
    cos_s, sin_s = _rope_tables(pos_side)
    xs = jnp.concatenate([meta_tokens, x_sample.reshape(dbsz * t_dec, D_MODEL)], axis=0)
    spool16 = jnp.pad(state_pool, ((0, 0), (N_META - POOL_HIST, 0), (0, 0)))
    sconv8 = jnp.pad(state_conv, ((0, 0), (SUBLANES - (CONV_W - 1), 0), (0, 0)))
    tab_meta = _score_table(sinks, N_META, N_META, LANES)
    tab_dec = _score_table(sinks, t_dec, N_META + WINDOW + t_dec, 2 * LANES)
    y_side, k_side, v_side, p_side, up_side = _side_call(
        xs, cache_swa_k.reshape(dbsz, WINDOW, KV_W), cache_swa_v.reshape(dbsz, WINDOW, KV_W),
        cache_meta_k.reshape(dbsz, N_META, KV_W), cache_meta_v.reshape(dbsz, N_META, KV_W), spool16, sconv8,
        cos_s, sin_s, tab_meta, tab_dec, gmix, w_in_b, bgate, w_ao_b, wgrp_b, pscale, w_po_b, w_out_b,
        gffn, w_up_b, conv_w, convb, w_down_b, gfin)
    km, vm, p_meta = k_side[:N_META], v_side[:N_META], p_side[:N_META]
    up_meta8 = up_side[N_META - SUBLANES:N_META]

    cos_p, sin_p = _rope_tables(N_META + jnp.arange(seq, dtype=jnp.int32))
    q, k, v, p, gates = _inproj_call(x_prompt, cos_p, sin_p, gmix, w_in_b, bgate)
    tab = jnp.stack([
        _score_table(sinks, CHUNK, BAND + N_META, KEYS_PAD, masked_prefix=(WINDOW // CHUNK - c) * CHUNK)
        for c in range(N_TABS)])
    o_attn = _attn_call(q, k, v, km, vm, tab)
    y_prompt, up_tail = _post_call(x_prompt, o_attn, p, p_meta, gates, up_meta8, w_ao_b, wgrp_b, pscale, w_po_b,
                                   w_out_b, gffn, w_up_b, conv_w, convb, w_down_b, gfin)

    kv4 = lambda a, n: a.reshape(a.shape[0], n, N_KV_HEADS, HEAD_DIM)
    dec = lambda a: a[N_META:].reshape(dbsz, t_dec, -1)
    return (
        y_prompt,
        dec(y_side),
        kv4(k[:, seq - WINDOW:], WINDOW),
        kv4(v[:, seq - WINDOW:], WINDOW),
        jnp.broadcast_to(km.reshape(1, N_META, N_KV_HEADS, HEAD_DIM), (bsz, N_META, N_KV_HEADS, HEAD_DIM)),
        jnp.broadcast_to(vm.reshape(1, N_META, N_KV_HEADS, HEAD_DIM), (bsz, N_META, N_KV_HEADS, HEAD_DIM)),
        p[:, seq - POOL_HIST:],
        up_tail[:, SUBLANES - (CONV_W - 1):],
        kv4(dec(k_side), t_dec),
        kv4(dec(v_side), t_dec),
        dec(p_side)[:, t_dec - POOL_HIST:],
        dec(up_side)[:, t_dec - (CONV_W - 1):],
    )
```

```python
import functools

import jax
import jax.numpy as jnp
from jax import lax
from jax.experimental import pallas as pl
from jax.experimental.pallas import tpu as pltpu

D_MODEL = 1024
N_META = 16
CHUNK = 64
HEAD_DIM = 64
N_Q_HEADS = 16
N_KV_HEADS = 2
WINDOW = 128
ROPE_THETA = 10000.0
POOL_WINDOWS = (2, 4, 8, 16)
N_POOL_GROUPS = 4
POOL_WIDTH = D_MODEL // 2
POOL_GROUP_W = POOL_WIDTH // N_POOL_GROUPS
POOL_HIST = max(POOL_WINDOWS) - 1
Q_W = N_Q_HEADS * HEAD_DIM
KV_W = N_KV_HEADS * HEAD_DIM
GATE_W = 2 * D_MODEL
IN_W = Q_W + 2 * KV_W + POOL_WIDTH + GATE_W
D_FF = ((8 * D_MODEL // 3) + 127) // 128 * 128
CONV_W = 3
RMS_EPS = 1e-6
PAST_LEN = 1024

LANES = 128
SUBLANES = 8
Q_BLOCKS = Q_W // LANES
BLOCKS_PER_KV = Q_BLOCKS // N_KV_HEADS
NEG = -1e30
LOG2E = 1.4426950408889634

K_OFF = Q_W
V_OFF = Q_W + KV_W
P_OFF = Q_W + 2 * KV_W
G_OFF = P_OFF + POOL_WIDTH

F32 = jnp.float32
BF16 = jnp.bfloat16

VMEM_LIMIT = 56 * 1024 * 1024


def _rmsnorm(x, g):
    ms = jnp.mean(x * x, axis=-1, keepdims=True)
    return (x * lax.rsqrt(ms + RMS_EPS)) * g


def _dot(a, b):
    return jnp.dot(a, b, preferred_element_type=F32)


def _rope(x, cos, sin_signed):
    lane = lax.broadcasted_iota(jnp.int32, (x.shape[0], LANES), 1)
    first_half = (lane & (HEAD_DIM // 2)) == 0
    outs = []
    for b in range(x.shape[1] // LANES):
        xb = x[:, b * LANES:(b + 1) * LANES]
        partner = jnp.where(first_half, pltpu.roll(xb, LANES - HEAD_DIM // 2, 1), pltpu.roll(xb, HEAD_DIM // 2, 1))
        outs.append(xb * cos + partner * sin_signed)
    return outs[0] if len(outs) == 1 else jnp.concatenate(outs, axis=1)


def _kv_variants(kv, dtype=BF16):
    lane = lax.broadcasted_iota(jnp.int32, kv.shape, 1)
    low = lane < HEAD_DIM
    swapped = pltpu.roll(kv, HEAD_DIM, 1)
    zero = jnp.zeros_like(kv)
    return (
        (jnp.where(low, kv, zero).astype(dtype), jnp.where(low, zero, swapped).astype(dtype)),
        (jnp.where(low, swapped, zero).astype(dtype), jnp.where(low, zero, kv).astype(dtype)),
    )


def _attend_group(qg, kst, vst, tab):
    s = lax.dot_general(qg, kst, (((1,), (1,)), ((), ())), preferred_element_type=F32) + tab
    npad = kst.shape[0] // 2
    se, so = s[:, :npad], s[:, npad:]
    pe = jnp.exp(se - jnp.max(se, axis=1, keepdims=True))
    po = jnp.exp(so - jnp.max(so, axis=1, keepdims=True))
    le = jnp.sum(pe, axis=1, keepdims=True)
    lo = jnp.sum(po, axis=1, keepdims=True)
    p = jnp.concatenate([pe, po], axis=1).astype(BF16)
    o = _dot(p, vst)
    lane = lax.broadcasted_iota(jnp.int32, o.shape, 1)
    return o * jnp.where(lane < HEAD_DIM, 1.0 / le, 1.0 / lo)


def _pool_group(ext, g, cnt=None):
    w = POOL_WINDOWS[g]
    s = ext
    k = 1
    while k < w:
        s = s + pltpu.roll(s, k, 0)
        k *= 2
    cur = ext[N_META:]
    return s[N_META:] / (float(w) if cnt is None else cnt) - cur


def _pool_project(pm, wgrp_ref, pscale):
    outs = []
    for g in range(N_POOL_GROUPS):
        sl = slice(g * POOL_GROUP_W, (g + 1) * POOL_GROUP_W)
        outs.append(_dot(pm[:, sl].astype(BF16), wgrp_ref[g]))
    return (jnp.concatenate(outs, axis=1) * pscale).astype(BF16)


def _conv3(u, prev8, w, b):
    t = u.shape[0]
    ext = jnp.concatenate([prev8, u], axis=0)
    u1 = pltpu.roll(ext, 1, 0)[SUBLANES:SUBLANES + t]
    u2 = pltpu.roll(ext, 2, 0)[SUBLANES:SUBLANES + t]
    c = b + u2 * w[0:1]
    c = c + u1 * w[1:2]
    return c + u * w[2:3]


def _sigmoid(x):
    return 0.5 * jnp.tanh(0.5 * x) + 0.5


def _silu(x):
    half = 0.5 * x
    return half + half * jnp.tanh(half)


N_SIDE = N_META + 8 * 16
SIDE_FF_CHUNK = 256


def _side_kernel(xs_ref, ck_ref, cv_ref, cmk_ref, cmv_ref, spool_ref, sconv_ref, cos_ref, sin_ref, tabm_ref,
                 tabs_ref, gmix_ref, w_in_ref, bgate_ref, w_ao_ref, wgrp_ref, pscale_ref, w_po_ref, w_out_ref,
                 gffn_ref, w_up_ref, convw_ref, convb_ref, w_down_ref, gfin_ref,
                 y_ref, k_ref, v_ref, p_ref, up_ref):
    dec_b = ck_ref.shape[0]
    t_dec = (N_SIDE - N_META) // dec_b
    x = xs_ref[...]
    h = _rmsnorm(x, gmix_ref[...]).astype(BF16)
    cos, sin = cos_ref[...], sin_ref[...]
    q = _rope(_dot(h, w_in_ref[:, 0:Q_W]) * (HEAD_DIM ** -0.5), cos, sin).astype(BF16)
    k = _rope(_dot(h, w_in_ref[:, K_OFF:K_OFF + KV_W]), cos, sin)
    v = _dot(h, w_in_ref[:, V_OFF:V_OFF + KV_W])
    p = _dot(h, w_in_ref[:, P_OFF:P_OFF + POOL_WIDTH])
    gates = _sigmoid(_dot(h, w_in_ref[:, G_OFF:G_OFF + GATE_W]) + bgate_ref[...])
    k_ref[...] = k
    v_ref[...] = v
    p_ref[...] = p

    def attend_rows(r0, nrows, kx, vx, tab_ref):
        kvar, vvar = _kv_variants(kx), _kv_variants(vx)
        blocks = [None] * Q_BLOCKS
        for n in range(N_KV_HEADS):
            qg = jnp.concatenate([q[r0:r0 + nrows, (BLOCKS_PER_KV * n + bi) * LANES:(BLOCKS_PER_KV * n + bi + 1) * LANES]
                                  for bi in range(BLOCKS_PER_KV)], axis=0)
            kst = jnp.concatenate(kvar[n], axis=0)
            vst = jnp.concatenate(vvar[n], axis=0)
            o = _attend_group(qg, kst, vst, tab_ref[n])
            for bi in range(BLOCKS_PER_KV):
                blocks[BLOCKS_PER_KV * n + bi] = o[bi * nrows:(bi + 1) * nrows]
        return jnp.concatenate(blocks, axis=1)

    zpad = jnp.zeros((LANES - N_META, LANES), F32)
    o_rows = [attend_rows(0, N_META, jnp.concatenate([k[:N_META], zpad], axis=0),
                          jnp.concatenate([v[:N_META], zpad], axis=0), tabm_ref)]
    n_keys = N_META + WINDOW + t_dec
    zpad = jnp.zeros((2 * LANES - n_keys, LANES), F32)
    for b in range(dec_b):
        r0 = N_META + b * t_dec
        kx = jnp.concatenate([cmk_ref[b], ck_ref[b], k[r0:r0 + t_dec], zpad], axis=0)
        vx = jnp.concatenate([cmv_ref[b], cv_ref[b], v[r0:r0 + t_dec], zpad], axis=0)
        o_rows.append(attend_rows(r0, t_dec, kx, vx, tabs_ref))
    o_attn = jnp.concatenate(o_rows, axis=0).astype(BF16)

    row = lax.broadcasted_iota(jnp.int32, (N_META, LANES), 0)
    pm_rows = []
    for seg in range(1 + dec_b):
        if seg == 0:
            ext = jnp.concatenate([jnp.zeros((N_META, POOL_WIDTH), F32), p[:N_META]], axis=0)
        else:
            r0 = N_META + (seg - 1) * t_dec
            ext = jnp.concatenate([spool_ref[seg - 1], p[r0:r0 + t_dec]], axis=0)
        groups = []
        for g in range(N_POOL_GROUPS):
            cnt = jnp.minimum(POOL_WINDOWS[g], row + 1).astype(F32) if seg == 0 else None
            groups.append(_pool_group(ext[:, g * POOL_GROUP_W:(g + 1) * POOL_GROUP_W], g, cnt))
        pm_rows.append(jnp.concatenate(groups, axis=1))
    pm = jnp.concatenate(pm_rows, axis=0)

    pool = _pool_project(pm, wgrp_ref, pscale_ref[...])
    mix = gates[:, :D_MODEL] * _dot(o_attn, w_ao_ref[...]) + gates[:, D_MODEL:] * _dot(pool, w_po_ref[...])
    x1 = x + _dot(mix.astype(BF16), w_out_ref[...])

    hn = _rmsnorm(x1, gffn_ref[...]).astype(BF16)
    acc = jnp.zeros((N_SIDE, D_MODEL), F32)
    zprev = jnp.zeros((SUBLANES, SIDE_FF_CHUNK), F32)
    for c0 in range(0, D_FF, SIDE_FF_CHUNK):
        halves = []
        for off in (c0, D_FF + c0):
            cs = slice(off, off + SIDE_FF_CHUNK)
            u = _dot(hn, w_up_ref[:, cs])
            up_ref[:, cs] = u
            w, bias = convw_ref[:, cs], convb_ref[:, cs]
            segs = [_conv3(u[:N_META], zprev, w, bias)]
            for b in range(dec_b):
                r0 = N_META + b * t_dec
                segs.append(_conv3(u[r0:r0 + t_dec], sconv_ref[b, :, cs], w, bias))
            halves.append(jnp.concatenate(segs, axis=0))
        act = (_silu(halves[0]) * halves[1]).astype(BF16)
        acc = acc + _dot(act, w_down_ref[c0:c0 + SIDE_FF_CHUNK, :])
    y_ref[...] = _rmsnorm(x1 + acc, gfin_ref[...])


def _side_call(*args):
    out_shape = (
        jax.ShapeDtypeStruct((N_SIDE, D_MODEL), F32),
        jax.ShapeDtypeStruct((N_SIDE, KV_W), F32),
        jax.ShapeDtypeStruct((N_SIDE, KV_W), F32),
        jax.ShapeDtypeStruct((N_SIDE, POOL_WIDTH), F32),
        jax.ShapeDtypeStruct((N_SIDE, 2 * D_FF), F32),
    )
    return pl.pallas_call(
        _side_kernel,
        out_shape=out_shape,
        compiler_params=pltpu.CompilerParams(vmem_limit_bytes=VMEM_LIMIT),
        name="side_rows",
    )(*args)


TM_IN = 512
IN_SUB = 256
GATE_CHUNK = 512


def _inproj_kernel(x_ref, cos_ref, sin_ref, gmix_ref, w_in_ref, bgate_ref, q_ref, k_ref, v_ref, p_ref, g_ref):
    subs = [slice(r0, r0 + IN_SUB) for r0 in range(0, x_ref.shape[0], IN_SUB)]
    hs = [_rmsnorm(x_ref[rows], gmix_ref[...]).astype(BF16) for rows in subs]
    for rows, h in zip(subs, hs):
        cos, sin = cos_ref[rows], sin_ref[rows]
        for c0 in range(0, GATE_W, GATE_CHUNK):
            z = _dot(h, w_in_ref[:, G_OFF + c0:G_OFF + c0 + GATE_CHUNK]) + bgate_ref[:, c0:c0 + GATE_CHUNK]
            g_ref[rows, c0:c0 + GATE_CHUNK] = _sigmoid(z).astype(BF16)
        q_ref[rows] = _rope(_dot(h, w_in_ref[:, 0:Q_W]) * (HEAD_DIM ** -0.5 * LOG2E), cos, sin).astype(BF16)
        p_ref[rows] = _dot(h, w_in_ref[:, P_OFF:P_OFF + POOL_WIDTH])
        kv = _dot(h, w_in_ref[:, K_OFF:K_OFF + 2 * KV_W])
        k_ref[rows] = _rope(kv[:, :KV_W], cos, sin)
        v_ref[rows] = kv[:, KV_W:]


def _const_spec(shape):
    nd = len(shape)
    return pl.BlockSpec(shape, lambda *_: (0,) * nd)


def _inproj_call(x, cos, sin, gmix, w_in, bgate):
    bsz, seq, _ = x.shape
    tm = TM_IN
    row = lambda w: pl.BlockSpec((None, tm, w), lambda b, i: (b, i, 0))
    return pl.pallas_call(
        _inproj_kernel,
        grid=(bsz, seq // tm),
        in_specs=[row(D_MODEL), pl.BlockSpec((tm, LANES), lambda b, i: (i, 0)),
                  pl.BlockSpec((tm, LANES), lambda b, i: (i, 0)),
                  _const_spec(gmix.shape), _const_spec(w_in.shape), _const_spec(bgate.shape)],
        out_specs=(row(Q_W), row(KV_W), row(KV_W), row(POOL_WIDTH), row(GATE_W)),
        out_shape=(jax.ShapeDtypeStruct((bsz, seq, Q_W), BF16), jax.ShapeDtypeStruct((bsz, seq, KV_W), F32),
                   jax.ShapeDtypeStruct((bsz, seq, KV_W), F32), jax.ShapeDtypeStruct((bsz, seq, POOL_WIDTH), F32),
                   jax.ShapeDtypeStruct((bsz, seq, GATE_W), BF16)),
        compiler_params=pltpu.CompilerParams(dimension_semantics=("arbitrary", "arbitrary"),
                                             vmem_limit_bytes=VMEM_LIMIT),
        name="in_proj",
    )(x, cos, sin, gmix, w_in, bgate)


TQ = 512
BAND = WINDOW + CHUNK
KEYS_PAD = BAND + N_META + 16
N_TABS = WINDOW // CHUNK + 1
SUM_ROWS = 16


def _attn_kernel(q_ref, kc_ref, kp_ref, vc_ref, vp_ref, mk_ref, mv_ref, kaug_ref, qaug_ref, o_ref):
    i = pl.program_id(1)
    kvar = _kv_variants(jnp.concatenate([kp_ref[...], kc_ref[...]], axis=0), BF16)
    vvar = _kv_variants(jnp.concatenate([vp_ref[...], vc_ref[...]], axis=0), F32)
    mkvar, mvvar = _kv_variants(mk_ref[...], BF16), _kv_variants(mv_ref[...], F32)
    kpad = jnp.zeros((KEYS_PAD - BAND - N_META, LANES), BF16)
    vpad = jnp.zeros((KEYS_PAD - BAND - N_META, LANES), F32)
    row = lax.broadcasted_iota(jnp.int32, (SUM_ROWS, 2 * KEYS_PAD), 0)
    col = lax.broadcasted_iota(jnp.int32, (SUM_ROWS, 2 * KEYS_PAD), 1)
    ones = jnp.where((row == 0) & (col < KEYS_PAD) | (row == 1) & (col >= KEYS_PAD), 1.0, 0.0).astype(BF16)

    def scores(j, n):
        tsel = jnp.minimum(i * (TQ // CHUNK) + j, N_TABS - 1)
        rows = slice(j * CHUNK, (j + 1) * CHUNK)
        band = slice(j * CHUNK, j * CHUNK + BAND)
        kst = jnp.concatenate([kvar[n][0][band], mkvar[n][0], kpad, kvar[n][1][band], mkvar[n][1], kpad], axis=0)
        vst = jnp.concatenate([vvar[n][0][band], mvvar[n][0], vpad, vvar[n][1][band], mvvar[n][1], vpad], axis=0)
        qg = jnp.concatenate([q_ref[rows, (BLOCKS_PER_KV * n + bi) * LANES:(BLOCKS_PER_KV * n + bi + 1) * LANES]
                              for bi in range(BLOCKS_PER_KV)], axis=0)
        s = lax.dot_general(jnp.concatenate([kst, kaug_ref[tsel]], axis=1),
                            jnp.concatenate([qg, qaug_ref[n]], axis=1),
                            (((1,), (1,)), ((), ())), preferred_element_type=F32)
        vt = jnp.concatenate([vst.T.astype(BF16), ones], axis=0)
        return s, vt

    def finish(j, n, s, vt):
        se, so = s[:KEYS_PAD], s[KEYS_PAD:]
        pe = jnp.exp2(se - jnp.max(se, axis=0, keepdims=True))
        po = jnp.exp2(so - jnp.max(so, axis=0, keepdims=True))
        o = _dot(vt, jnp.concatenate([pe, po], axis=0).astype(BF16))
        o = jnp.concatenate([o[:HEAD_DIM] / o[LANES:LANES + 1], o[HEAD_DIM:LANES] / o[LANES + 1:LANES + 2]], axis=0).T
        for bi in range(BLOCKS_PER_KV):
            blk = BLOCKS_PER_KV * n + bi
            o_ref[j * CHUNK:(j + 1) * CHUNK, blk * LANES:(blk + 1) * LANES] = o[bi * CHUNK:(bi + 1) * CHUNK].astype(BF16)

    groups = [(j, n) for j in range(TQ // CHUNK) for n in range(N_KV_HEADS)]
    pending = scores(*groups[0])
    for g, (j, n) in enumerate(groups):
        current = pending
        if g + 1 < len(groups):
            pending = scores(*groups[g + 1])
        finish(j, n, *current)


def _attn_call(q, k, v, mk, mv, kaug, qaug):
    bsz, seq, _ = q.shape
    cur = lambda w: pl.BlockSpec((None, TQ, w), lambda b, i: (b, i, 0))
    prev = pl.BlockSpec((None, WINDOW, KV_W), lambda b, i: (b, jnp.maximum(i * (TQ // WINDOW) - 1, 0), 0))
    return pl.pallas_call(
        _attn_kernel,
        grid=(bsz, seq // TQ),
        in_specs=[cur(Q_W), cur(KV_W), prev, cur(KV_W), prev, _const_spec(mk.shape), _const_spec(mv.shape),
                  _const_spec(kaug.shape), _const_spec(qaug.shape)],
        out_specs=cur(Q_W),
        out_shape=jax.ShapeDtypeStruct((bsz, seq, Q_W), BF16),
        compiler_params=pltpu.CompilerParams(dimension_semantics=("arbitrary", "arbitrary"),
                                             vmem_limit_bytes=VMEM_LIMIT),
        name="swa_attention",
    )(q, k, k, v, v, mk, mv, kaug, qaug)


def _attn_aug_tables(sinks):
    n_keys = BAND + N_META
    r = jnp.arange(KEYS_PAD)
    lane = jnp.arange(LANES)[None, :]
    kaug = []
    for c in range(N_TABS):
        masked = (r < (WINDOW // CHUNK - c) * CHUNK) | (r > n_keys)
        mask_col = jnp.where(masked, NEG, 0.0)[:, None] * (lane == 0)
        sink_row = (r == n_keys)[:, None]
        even = mask_col + jnp.where(sink_row & ((lane == 1) | (lane == 2)), 1.0, 0.0)
        odd = mask_col + jnp.where(sink_row & ((lane == 3) | (lane == 4)), 1.0, 0.0)
        kaug.append(jnp.concatenate([even, odd], axis=0))
    s2 = sinks.astype(F32) * LOG2E
    hi = s2.astype(BF16).astype(F32)
    lo = s2 - hi
    head = 2 * jnp.arange(Q_BLOCKS)
    cols = [jnp.ones((Q_BLOCKS,), F32), hi[head], lo[head], hi[head + 1], lo[head + 1]]
    qaug = jnp.stack(cols, axis=1)
    qaug = jnp.pad(qaug, ((0, 0), (0, LANES - qaug.shape[1])))
    qaug = jnp.broadcast_to(qaug[:, None, :], (Q_BLOCKS, CHUNK, LANES)).reshape(N_KV_HEADS, BLOCKS_PER_KV * CHUNK, LANES)
    return jnp.stack(kaug).astype(BF16), qaug.astype(BF16)


TM_POST = 512
POST_SUB = 256
FF_CHUNK = 256
FF_SKEW = 2
FF_LANE_BLOCKS = FF_CHUNK // LANES


def _post_kernel(x_ref, o_ref, p_ref, pprev_ref, pmeta_ref, g_ref, upmeta_ref, w_ao_ref, wgrp_ref, pscale_ref,
                 w_po_ref, w_out_ref, gffn_ref, w_up_ref, convw_ref, convb_ref, w_down_ref, gfin_ref,
                 y_ref, carry_ref, pext_ref, x1_ref, hn_ref, *uext_refs):
    i = pl.program_id(1)
    tm = x_ref.shape[0]
    n_chunks = D_FF // FF_CHUNK

    @pl.when(i == 0)
    def _():
        pext_ref[0:N_META] = pmeta_ref[...]
        carry_ref[...] = upmeta_ref[...]

    @pl.when(i > 0)
    def _():
        pext_ref[0:N_META] = pprev_ref[...]

    pext_ref[N_META:] = p_ref[...]
    subs = [slice(r0, r0 + POST_SUB) for r0 in range(0, tm, POST_SUB)]

    mixes = []
    for rows in subs:
        attn = _dot(o_ref[rows], w_ao_ref[...])
        ext_rows = slice(rows.start, rows.stop + N_META)
        pm = jnp.concatenate([_pool_group(pext_ref[ext_rows, g * POOL_GROUP_W:(g + 1) * POOL_GROUP_W], g)
                              for g in range(N_POOL_GROUPS)], axis=1)
        pool = _dot(_pool_project(pm, wgrp_ref, pscale_ref[...]), w_po_ref[...])
        mix = g_ref[rows, :D_MODEL].astype(F32) * attn + g_ref[rows, D_MODEL:].astype(F32) * pool
        mixes.append(mix.astype(BF16))
    for rows, mix in zip(subs, mixes):
        x1 = x_ref[rows] + _dot(mix, w_out_ref[...])
        x1_ref[rows] = x1
        hn_ref[rows] = _rmsnorm(x1, gffn_ref[...]).astype(BF16)

    def up_project(rows, c):
        for half, off in enumerate((c * FF_CHUNK, D_FF + c * FF_CHUNK)):
            u = _dot(hn_ref[rows], w_up_ref[:, off:off + FF_CHUNK])
            for lb in range(FF_LANE_BLOCKS):
                blk = uext_refs[c].at[half * FF_LANE_BLOCKS + lb]
                if rows.start == 0:
                    blk[0:SUBLANES] = carry_ref[:, off + lb * LANES:off + (lb + 1) * LANES]
                blk[SUBLANES + rows.start:SUBLANES + rows.stop] = u[:, lb * LANES:(lb + 1) * LANES]
            if rows.stop == tm:
                carry_ref[:, off:off + FF_CHUNK] = u[POST_SUB - SUBLANES:]

    def conv(rows, c, half):
        outs = []
        for lb in range(FF_LANE_BLOCKS):
            off = half * D_FF + c * FF_CHUNK + lb * LANES
            w, b = convw_ref[:, off:off + LANES], convb_ref[:, off:off + LANES]
            ext = uext_refs[c].at[half * FF_LANE_BLOCKS + lb]
            base = SUBLANES + rows.start
            cv = b + ext[base - 2:base - 2 + POST_SUB] * w[0:1]
            cv = cv + ext[base - 1:base - 1 + POST_SUB] * w[1:2]
            outs.append(cv + ext[base:base + POST_SUB] * w[2:3])
        return jnp.concatenate(outs, axis=1)

    steps = [(rows, c) for rows in subs for c in range(n_chunks)]
    for step in steps[:FF_SKEW]:
        up_project(*step)
    acc = None
    for j, (rows, c) in enumerate(steps):
        act = (_silu(conv(rows, c, 0)) * conv(rows, c, 1)).astype(BF16)
        if j + FF_SKEW < len(steps):
            up_project(*steps[j + FF_SKEW])
        d = _dot(act, w_down_ref[c * FF_CHUNK:(c + 1) * FF_CHUNK, :])
        acc = d if c == 0 else acc + d
        if c == n_chunks - 1:
            y_ref[rows] = _rmsnorm(x1_ref[rows] + acc, gfin_ref[...])


def _post_call(x, o_attn, p, p_meta, gates, up_meta8, w_ao, wgrp, pscale, w_po, w_out, gffn, w_up, convw, convb,
               w_down, gfin):
    bsz, seq, _ = x.shape
    tm = TM_POST
    row = lambda w: pl.BlockSpec((None, tm, w), lambda b, i: (b, i, 0))
    pprev = pl.BlockSpec((None, N_META, POOL_WIDTH), lambda b, i: (b, jnp.maximum(i * (tm // N_META) - 1, 0), 0))
    consts = (p_meta,), (up_meta8, w_ao, wgrp, pscale, w_po, w_out, gffn, w_up, convw, convb, w_down, gfin)
    return pl.pallas_call(
        _post_kernel,
        grid=(bsz, seq // tm),
        in_specs=[row(D_MODEL), row(Q_W), row(POOL_WIDTH), pprev, _const_spec(p_meta.shape), row(GATE_W)]
                 + [_const_spec(a.shape) for a in consts[1]],
        out_specs=(row(D_MODEL), pl.BlockSpec((None, SUBLANES, 2 * D_FF), lambda b, i: (b, 0, 0))),
        out_shape=(jax.ShapeDtypeStruct((bsz, seq, D_MODEL), F32),
                   jax.ShapeDtypeStruct((bsz, SUBLANES, 2 * D_FF), F32)),
        scratch_shapes=[pltpu.VMEM((N_META + tm, POOL_WIDTH), F32), pltpu.VMEM((tm, D_MODEL), F32),
                        pltpu.VMEM((tm, D_MODEL), BF16)]
                       + [pltpu.VMEM((2 * FF_LANE_BLOCKS, SUBLANES + tm, LANES), F32)
                          for _ in range(D_FF // FF_CHUNK)],
        compiler_params=pltpu.CompilerParams(dimension_semantics=("arbitrary", "arbitrary"),
                                             vmem_limit_bytes=VMEM_LIMIT),
        name="mixer_ffn",
    )(x, o_attn, p, p, p_meta, gates, *consts[1])


def _rope_tables(pos):
    half = HEAD_DIM // 2
    inv = ROPE_THETA ** (-jnp.arange(half, dtype=F32) / half)
    ang = pos.astype(F32)[:, None] * inv[None, :]
    cos, sin = jnp.cos(ang), jnp.sin(ang)
    return jnp.tile(cos, (1, LANES // half)), jnp.tile(jnp.concatenate([-sin, sin], axis=1), (1, LANES // HEAD_DIM))


def _score_table(sinks, rows_per_block, n_keys, keys_pad, masked_prefix=0):
    col = jnp.arange(keys_pad)
    base = jnp.where((col >= masked_prefix) & (col < n_keys), 0.0, NEG).astype(F32)
    head = (2 * jnp.arange(Q_BLOCKS)[:, None] + jnp.arange(2)[None, :])
    tab = jnp.where(col[None, None, :] == n_keys, sinks.astype(F32)[head][:, :, None], base[None, None, :])
    tab = tab.reshape(N_KV_HEADS, BLOCKS_PER_KV, 1, 2 * keys_pad)
    tab = jnp.broadcast_to(tab, (N_KV_HEADS, BLOCKS_PER_KV, rows_per_block, 2 * keys_pad))
    return tab.reshape(N_KV_HEADS, BLOCKS_PER_KV * rows_per_block, 2 * keys_pad)


def kernel(x_prompt, x_sample, cache_swa_k, cache_swa_v, cache_meta_k, cache_meta_v, state_pool, state_conv,
           meta_tokens, g_norm_mix, w_in, b_gate, sinks, w_attn_o, w_pool_grp, pool_scale, w_pool_o, w_out,
           g_norm_ffn, w_up, conv_w, conv_b, w_down, g_norm_final):
    bsz, seq, _ = x_prompt.shape
    dbsz, t_dec, _ = x_sample.shape
    row2 = lambda a: a.reshape(1, -1)
    w_in_b, w_ao_b, wgrp_b, w_po_b = w_in.astype(BF16), w_attn_o.astype(BF16), w_pool_grp.astype(BF16), w_pool_o.astype(BF16)
    w_out_b, w_up_b, w_down_b = w_out.astype(BF16), w_up.astype(BF16), w_down.astype(BF16)
    gmix, gffn, gfin = row2(g_norm_mix), row2(g_norm_ffn), row2(g_norm_final)
    bgate, pscale, convb = row2(b_gate), row2(pool_scale), row2(conv_b)

    pos_side = jnp.concatenate([jnp.arange(N_META, dtype=jnp.int32),
                                jnp.tile(N_META + PAST_LEN + jnp.arange(t_dec, dtype=jnp.int32), dbsz)])
    cos_s, sin_s = _rope_tables(pos_side)
    xs = jnp.concatenate([meta_tokens, x_sample.reshape(dbsz * t_dec, D_MODEL)], axis=0)
    spool16 = jnp.pad(state_pool, ((0, 0), (N_META - POOL_HIST, 0), (0, 0)))
    sconv8 = jnp.pad(state_conv, ((0, 0), (SUBLANES - (CONV_W - 1), 0), (0, 0)))
    tab_meta = _score_table(sinks, N_META, N_META, LANES)
    tab_dec = _score_table(sinks, t_dec, N_META + WINDOW + t_dec, 2 * LANES)
    y_side, k_side, v_side, p_side, up_side = _side_call(
        xs, cache_swa_k.reshape(dbsz, WINDOW, KV_W), cache_swa_v.reshape(dbsz, WINDOW, KV_W),
        cache_meta_k.reshape(dbsz, N_META, KV_W), cache_meta_v.reshape(dbsz, N_META, KV_W), spool16, sconv8,
        cos_s, sin_s, tab_meta, tab_dec, gmix, w_in_b, bgate, w_ao_b, wgrp_b, pscale, w_po_b, w_out_b,
        gffn, w_up_b, conv_w, convb, w_down_b, gfin)
    km, vm, p_meta = k_side[:N_META], v_side[:N_META], p_side[:N_META]
    up_meta8 = up_side[N_META - SUBLANES:N_META]

    cos_p, sin_p = _rope_tables(N_META + jnp.arange(seq, dtype=jnp.int32))
    q, k, v, p, gates = _inproj_call(x_prompt, cos_p, sin_p, gmix, w_in_b, bgate)
    kaug, qaug = _attn_aug_tables(sinks)
    o_attn = _attn_call(q, k, v, km, vm, kaug, qaug)
    y_prompt, up_tail = _post_call(x_prompt, o_attn, p, p_meta, gates, up_meta8, w_ao_b, wgrp_b, pscale, w_po_b,
                                   w_out_b, gffn, w_up_b, conv_w, convb, w_down_b, gfin)

    kv4 = lambda a, n: a.reshape(a.shape[0], n, N_KV_HEADS, HEAD_DIM)
    dec = lambda a: a[N_META:].reshape(dbsz, t_dec, -1)
    return (
        y_prompt,
        dec(y_side),
        kv4(k[:, seq - WINDOW:], WINDOW),
        kv4(v[:, seq - WINDOW:], WINDOW),
        jnp.broadcast_to(km.reshape(1, N_META, N_KV_HEADS, HEAD_DIM), (bsz, N_META, N_KV_HEADS, HEAD_DIM)),
        jnp.broadcast_to(vm.reshape(1, N_META, N_KV_HEADS, HEAD_DIM), (bsz, N_META, N_KV_HEADS, HEAD_DIM)),
        p[:, seq - POOL_HIST:],
        up_tail[:, SUBLANES - (CONV_W - 1):],
        kv4(dec(k_side), t_dec),
        kv4(dec(v_side), t_dec),
        dec(p_side)[:, t_dec - POOL_HIST:],
        dec(up_side)[:, t_dec - (CONV_W - 1):],
    )
```

```python
import functools

import jax
import jax.numpy as jnp
from jax import lax
from jax.experimental import pallas as pl
from jax.experimental.pallas import tpu as pltpu

D_MODEL = 1024
N_META = 16
CHUNK = 64
HEAD_DIM = 64
N_Q_HEADS = 16
N_KV_HEADS = 2
WINDOW = 128
ROPE_THETA = 10000.0
POOL_WINDOWS = (2, 4, 8, 16)
N_POOL_GROUPS = 4
POOL_WIDTH = D_MODEL // 2
POOL_GROUP_W = POOL_WIDTH // N_POOL_GROUPS
POOL_HIST = max(POOL_WINDOWS) - 1
Q_W = N_Q_HEADS * HEAD_DIM
KV_W = N_KV_HEADS * HEAD_DIM
GATE_W = 2 * D_MODEL
IN_W = Q_W + 2 * KV_W + POOL_WIDTH + GATE_W
D_FF = ((8 * D_MODEL // 3) + 127) // 128 * 128
CONV_W = 3
RMS_EPS = 1e-6
PAST_LEN = 1024

LANES = 128
SUBLANES = 8
Q_BLOCKS = Q_W // LANES
BLOCKS_PER_KV = Q_BLOCKS // N_KV_HEADS
NEG = -1e30
LOG2E = 1.4426950408889634

K_OFF = Q_W
V_OFF = Q_W + KV_W
P_OFF = Q_W + 2 * KV_W
G_OFF = P_OFF + POOL_WIDTH

F32 = jnp.float32
BF16 = jnp.bfloat16

VMEM_LIMIT = 56 * 1024 * 1024


def _rmsnorm(x, g):
    ms = jnp.mean(x * x, axis=-1, keepdims=True)
    return (x * lax.rsqrt(ms + RMS_EPS)) * g


def _dot(a, b):
    return jnp.dot(a, b, preferred_element_type=F32)


def _rope(x, cos, sin_signed):
    lane = lax.broadcasted_iota(jnp.int32, (x.shape[0], LANES), 1)
    first_half = (lane & (HEAD_DIM // 2)) == 0
    outs = []
    for b in range(x.shape[1] // LANES):
        xb = x[:, b * LANES:(b + 1) * LANES]
        partner = jnp.where(first_half, pltpu.roll(xb, LANES - HEAD_DIM // 2, 1), pltpu.roll(xb, HEAD_DIM // 2, 1))
        outs.append(xb * cos + partner * sin_signed)
    return outs[0] if len(outs) == 1 else jnp.concatenate(outs, axis=1)


def _kv_variants(kv, dtype=BF16):
    lane = lax.broadcasted_iota(jnp.int32, kv.shape, 1)
    low = lane < HEAD_DIM
    swapped = pltpu.roll(kv, HEAD_DIM, 1)
    zero = jnp.zeros_like(kv)
    return (
        (jnp.where(low, kv, zero).astype(dtype), jnp.where(low, zero, swapped).astype(dtype)),
        (jnp.where(low, swapped, zero).astype(dtype), jnp.where(low, zero, kv).astype(dtype)),
    )


def _attend_group(qg, kst, vst, tab):
    s = lax.dot_general(qg, kst, (((1,), (1,)), ((), ())), preferred_element_type=F32) + tab
    npad = kst.shape[0] // 2
    se, so = s[:, :npad], s[:, npad:]
    pe = jnp.exp(se - jnp.max(se, axis=1, keepdims=True))
    po = jnp.exp(so - jnp.max(so, axis=1, keepdims=True))
    le = jnp.sum(pe, axis=1, keepdims=True)
    lo = jnp.sum(po, axis=1, keepdims=True)
    p = jnp.concatenate([pe, po], axis=1).astype(BF16)
    o = _dot(p, vst)
    lane = lax.broadcasted_iota(jnp.int32, o.shape, 1)
    return o * jnp.where(lane < HEAD_DIM, 1.0 / le, 1.0 / lo)


def _pool_group(ext, g, cnt=None):
    w = POOL_WINDOWS[g]
    s = ext
    k = 1
    while k < w:
        s = s + pltpu.roll(s, k, 0)
        k *= 2
    cur = ext[N_META:]
    return s[N_META:] / (float(w) if cnt is None else cnt) - cur


def _pool_project(pm, wgrp_ref, pscale):
    outs = []
    for g in range(N_POOL_GROUPS):
        sl = slice(g * POOL_GROUP_W, (g + 1) * POOL_GROUP_W)
        outs.append(_dot(pm[:, sl].astype(BF16), wgrp_ref[g]))
    return (jnp.concatenate(outs, axis=1) * pscale).astype(BF16)


def _conv3(u, prev8, w, b):
    t = u.shape[0]
    ext = jnp.concatenate([prev8, u], axis=0)
    u1 = pltpu.roll(ext, 1, 0)[SUBLANES:SUBLANES + t]
    u2 = pltpu.roll(ext, 2, 0)[SUBLANES:SUBLANES + t]
    c = b + u2 * w[0:1]
    c = c + u1 * w[1:2]
    return c + u * w[2:3]


def _sigmoid(x):
    return 0.5 * jnp.tanh(0.5 * x) + 0.5


def _silu(x):
    half = 0.5 * x
    return half + half * jnp.tanh(half)


N_SIDE = N_META + 8 * 16
SIDE_FF_CHUNK = 256


def _side_kernel(xs_ref, ck_ref, cv_ref, cmk_ref, cmv_ref, spool_ref, sconv_ref, cos_ref, sin_ref, tabm_ref,
                 tabs_ref, gmix_ref, w_in_ref, bgate_ref, w_ao_ref, wgrp_ref, pscale_ref, w_po_ref, w_out_ref,
                 gffn_ref, w_up_ref, convw_ref, convb_ref, w_down_ref, gfin_ref,
                 y_ref, k_ref, v_ref, p_ref, up_ref):
    dec_b = ck_ref.shape[0]
    t_dec = (N_SIDE - N_META) // dec_b
    x = xs_ref[...]
    h = _rmsnorm(x, gmix_ref[...]).astype(BF16)
    cos, sin = cos_ref[...], sin_ref[...]
    q = _rope(_dot(h, w_in_ref[:, 0:Q_W]) * (HEAD_DIM ** -0.5), cos, sin).astype(BF16)
    k = _rope(_dot(h, w_in_ref[:, K_OFF:K_OFF + KV_W]), cos, sin)
    v = _dot(h, w_in_ref[:, V_OFF:V_OFF + KV_W])
    p = _dot(h, w_in_ref[:, P_OFF:P_OFF + POOL_WIDTH])
    gates = _sigmoid(_dot(h, w_in_ref[:, G_OFF:G_OFF + GATE_W]) + bgate_ref[...])
    k_ref[...] = k
    v_ref[...] = v
    p_ref[...] = p

    def attend_rows(r0, nrows, kx, vx, tab_ref):
        kvar, vvar = _kv_variants(kx), _kv_variants(vx)
        blocks = [None] * Q_BLOCKS
        for n in range(N_KV_HEADS):
            qg = jnp.concatenate([q[r0:r0 + nrows, (BLOCKS_PER_KV * n + bi) * LANES:(BLOCKS_PER_KV * n + bi + 1) * LANES]
                                  for bi in range(BLOCKS_PER_KV)], axis=0)
            kst = jnp.concatenate(kvar[n], axis=0)
            vst = jnp.concatenate(vvar[n], axis=0)
            o = _attend_group(qg, kst, vst, tab_ref[n])
            for bi in range(BLOCKS_PER_KV):
                blocks[BLOCKS_PER_KV * n + bi] = o[bi * nrows:(bi + 1) * nrows]
        return jnp.concatenate(blocks, axis=1)

    zpad = jnp.zeros((LANES - N_META, LANES), F32)
    o_rows = [attend_rows(0, N_META, jnp.concatenate([k[:N_META], zpad], axis=0),
                          jnp.concatenate([v[:N_META], zpad], axis=0), tabm_ref)]
    n_keys = N_META + WINDOW + t_dec
    zpad = jnp.zeros((2 * LANES - n_keys, LANES), F32)
    for b in range(dec_b):
        r0 = N_META + b * t_dec
        kx = jnp.concatenate([cmk_ref[b], ck_ref[b], k[r0:r0 + t_dec], zpad], axis=0)
        vx = jnp.concatenate([cmv_ref[b], cv_ref[b], v[r0:r0 + t_dec], zpad], axis=0)
        o_rows.append(attend_rows(r0, t_dec, kx, vx, tabs_ref))
    o_attn = jnp.concatenate(o_rows, axis=0).astype(BF16)

    row = lax.broadcasted_iota(jnp.int32, (N_META, LANES), 0)
    pm_rows = []
    for seg in range(1 + dec_b):
        if seg == 0:
            ext = jnp.concatenate([jnp.zeros((N_META, POOL_WIDTH), F32), p[:N_META]], axis=0)
        else:
            r0 = N_META + (seg - 1) * t_dec
            ext = jnp.concatenate([spool_ref[seg - 1], p[r0:r0 + t_dec]], axis=0)
        groups = []
        for g in range(N_POOL_GROUPS):
            cnt = jnp.minimum(POOL_WINDOWS[g], row + 1).astype(F32) if seg == 0 else None
            groups.append(_pool_group(ext[:, g * POOL_GROUP_W:(g + 1) * POOL_GROUP_W], g, cnt))
        pm_rows.append(jnp.concatenate(groups, axis=1))
    pm = jnp.concatenate(pm_rows, axis=0)

    pool = _pool_project(pm, wgrp_ref, pscale_ref[...])
    mix = gates[:, :D_MODEL] * _dot(o_attn, w_ao_ref[...]) + gates[:, D_MODEL:] * _dot(pool, w_po_ref[...])
    x1 = x + _dot(mix.astype(BF16), w_out_ref[...])

    hn = _rmsnorm(x1, gffn_ref[...]).astype(BF16)
    acc = jnp.zeros((N_SIDE, D_MODEL), F32)
    zprev = jnp.zeros((SUBLANES, SIDE_FF_CHUNK), F32)
    for c0 in range(0, D_FF, SIDE_FF_CHUNK):
        halves = []
        for off in (c0, D_FF + c0):
            cs = slice(off, off + SIDE_FF_CHUNK)
            u = _dot(hn, w_up_ref[:, cs])
            up_ref[:, cs] = u
            w, bias = convw_ref[:, cs], convb_ref[:, cs]
            segs = [_conv3(u[:N_META], zprev, w, bias)]
            for b in range(dec_b):
                r0 = N_META + b * t_dec
                segs.append(_conv3(u[r0:r0 + t_dec], sconv_ref[b, :, cs], w, bias))
            halves.append(jnp.concatenate(segs, axis=0))
        act = (_silu(halves[0]) * halves[1]).astype(BF16)
        acc = acc + _dot(act, w_down_ref[c0:c0 + SIDE_FF_CHUNK, :])
    y_ref[...] = _rmsnorm(x1 + acc, gfin_ref[...])


def _side_call(*args):
    out_shape = (
        jax.ShapeDtypeStruct((N_SIDE, D_MODEL), F32),
        jax.ShapeDtypeStruct((N_SIDE, KV_W), F32),
        jax.ShapeDtypeStruct((N_SIDE, KV_W), F32),
        jax.ShapeDtypeStruct((N_SIDE, POOL_WIDTH), F32),
        jax.ShapeDtypeStruct((N_SIDE, 2 * D_FF), F32),
    )
    return pl.pallas_call(
        _side_kernel,
        out_shape=out_shape,
        compiler_params=pltpu.CompilerParams(vmem_limit_bytes=VMEM_LIMIT),
        name="side_rows",
    )(*args)


TM_IN = 1024
IN_SUB = 256
GATE_CHUNK = 512


def _inproj_kernel(x_ref, cos_ref, sin_ref, gmix_ref, w_in_ref, bgate_ref, q_ref, k_ref, v_ref, p_ref, g_ref):
    subs = [slice(r0, r0 + IN_SUB) for r0 in range(0, x_ref.shape[0], IN_SUB)]
    hs = [_rmsnorm(x_ref[rows], gmix_ref[...]).astype(BF16) for rows in subs]
    for rows, h in zip(subs, hs):
        cos, sin = cos_ref[rows], sin_ref[rows]
        for c0 in range(0, GATE_W, GATE_CHUNK):
            z = _dot(h, w_in_ref[:, G_OFF + c0:G_OFF + c0 + GATE_CHUNK]) + bgate_ref[:, c0:c0 + GATE_CHUNK]
            g_ref[rows, c0:c0 + GATE_CHUNK] = _sigmoid(z).astype(BF16)
        q_ref[rows] = _rope(_dot(h, w_in_ref[:, 0:Q_W]) * (HEAD_DIM ** -0.5 * LOG2E), cos, sin).astype(BF16)
        p_ref[rows] = _dot(h, w_in_ref[:, P_OFF:P_OFF + POOL_WIDTH])
        kv = _dot(h, w_in_ref[:, K_OFF:K_OFF + 2 * KV_W])
        k_ref[rows] = _rope(kv[:, :KV_W], cos, sin)
        v_ref[rows] = kv[:, KV_W:]


def _const_spec(shape):
    nd = len(shape)
    return pl.BlockSpec(shape, lambda *_: (0,) * nd)


def _inproj_call(x, cos, sin, gmix, w_in, bgate):
    bsz, seq, _ = x.shape
    tm = TM_IN
    row = lambda w: pl.BlockSpec((None, tm, w), lambda b, i: (b, i, 0))
    return pl.pallas_call(
        _inproj_kernel,
        grid=(bsz, seq // tm),
        in_specs=[row(D_MODEL), pl.BlockSpec((tm, LANES), lambda b, i: (i, 0)),
                  pl.BlockSpec((tm, LANES), lambda b, i: (i, 0)),
                  _const_spec(gmix.shape), _const_spec(w_in.shape), _const_spec(bgate.shape)],
        out_specs=(row(Q_W), row(KV_W), row(KV_W), row(POOL_WIDTH), row(GATE_W)),
        out_shape=(jax.ShapeDtypeStruct((bsz, seq, Q_W), BF16), jax.ShapeDtypeStruct((bsz, seq, KV_W), F32),
                   jax.ShapeDtypeStruct((bsz, seq, KV_W), F32), jax.ShapeDtypeStruct((bsz, seq, POOL_WIDTH), F32),
                   jax.ShapeDtypeStruct((bsz, seq, GATE_W), BF16)),
        compiler_params=pltpu.CompilerParams(dimension_semantics=("arbitrary", "arbitrary"),
                                             vmem_limit_bytes=VMEM_LIMIT),
        name="in_proj",
    )(x, cos, sin, gmix, w_in, bgate)


TQ = 512
BAND = WINDOW + CHUNK
KEYS_PAD = BAND + N_META + 16
N_TABS = WINDOW // CHUNK + 1
SUM_ROWS = 16


def _attn_kernel(q_ref, kc_ref, kp_ref, vc_ref, vp_ref, mk_ref, mv_ref, kaug_ref, qaug_ref, o_ref):
    i = pl.program_id(1)
    kvar = _kv_variants(jnp.concatenate([kp_ref[...], kc_ref[...]], axis=0), BF16)
    vvar = _kv_variants(jnp.concatenate([vp_ref[...], vc_ref[...]], axis=0), F32)
    mkvar, mvvar = _kv_variants(mk_ref[...], BF16), _kv_variants(mv_ref[...], F32)
    kpad = jnp.zeros((KEYS_PAD - BAND - N_META, LANES), BF16)
    vpad = jnp.zeros((KEYS_PAD - BAND - N_META, LANES), F32)
    row = lax.broadcasted_iota(jnp.int32, (SUM_ROWS, 2 * KEYS_PAD), 0)
    col = lax.broadcasted_iota(jnp.int32, (SUM_ROWS, 2 * KEYS_PAD), 1)
    ones = jnp.where((row == 0) & (col < KEYS_PAD) | (row == 1) & (col >= KEYS_PAD), 1.0, 0.0).astype(BF16)

    def scores(j, n):
        tsel = jnp.minimum(i * (TQ // CHUNK) + j, N_TABS - 1)
        rows = slice(j * CHUNK, (j + 1) * CHUNK)
        band = slice(j * CHUNK, j * CHUNK + BAND)
        kst = jnp.concatenate([kvar[n][0][band], mkvar[n][0], kpad, kvar[n][1][band], mkvar[n][1], kpad], axis=0)
        vst = jnp.concatenate([vvar[n][0][band], mvvar[n][0], vpad, vvar[n][1][band], mvvar[n][1], vpad], axis=0)
        qg = jnp.concatenate([q_ref[rows, (BLOCKS_PER_KV * n + bi) * LANES:(BLOCKS_PER_KV * n + bi + 1) * LANES]
                              for bi in range(BLOCKS_PER_KV)], axis=0)
        s = lax.dot_general(jnp.concatenate([kst, kaug_ref[tsel]], axis=1),
                            jnp.concatenate([qg, qaug_ref[n]], axis=1),
                            (((1,), (1,)), ((), ())), preferred_element_type=F32)
        vt = jnp.concatenate([vst.T.astype(BF16), ones], axis=0)
        return s, vt

    def finish(j, n, s, vt):
        se, so = s[:KEYS_PAD], s[KEYS_PAD:]
        pe = jnp.exp2(se - jnp.max(se, axis=0, keepdims=True))
        po = jnp.exp2(so - jnp.max(so, axis=0, keepdims=True))
        o = _dot(vt, jnp.concatenate([pe, po], axis=0).astype(BF16))
        o = jnp.concatenate([o[:HEAD_DIM] / o[LANES:LANES + 1], o[HEAD_DIM:LANES] / o[LANES + 1:LANES + 2]], axis=0).T
        for bi in range(BLOCKS_PER_KV):
            blk = BLOCKS_PER_KV * n + bi
            o_ref[j * CHUNK:(j + 1) * CHUNK, blk * LANES:(blk + 1) * LANES] = o[bi * CHUNK:(bi + 1) * CHUNK].astype(BF16)

    groups = [(j, n) for j in range(TQ // CHUNK) for n in range(N_KV_HEADS)]
    pending = scores(*groups[0])
    for g, (j, n) in enumerate(groups):
        current = pending
        if g + 1 < len(groups):
            pending = scores(*groups[g + 1])
        finish(j, n, *current)


def _attn_call(q, k, v, mk, mv, kaug, qaug):
    bsz, seq, _ = q.shape
    cur = lambda w: pl.BlockSpec((None, TQ, w), lambda b, i: (b, i, 0))
    prev = pl.BlockSpec((None, WINDOW, KV_W), lambda b, i: (b, jnp.maximum(i * (TQ // WINDOW) - 1, 0), 0))
    return pl.pallas_call(
        _attn_kernel,
        grid=(bsz, seq // TQ),
        in_specs=[cur(Q_W), cur(KV_W), prev, cur(KV_W), prev, _const_spec(mk.shape), _const_spec(mv.shape),
                  _const_spec(kaug.shape), _const_spec(qaug.shape)],
        out_specs=cur(Q_W),
        out_shape=jax.ShapeDtypeStruct((bsz, seq, Q_W), BF16),
        compiler_params=pltpu.CompilerParams(dimension_semantics=("arbitrary", "arbitrary"),
                                             vmem_limit_bytes=VMEM_LIMIT),
        name="swa_attention",
    )(q, k, k, v, v, mk, mv, kaug, qaug)


def _attn_aug_tables(sinks):
    n_keys = BAND + N_META
    r = jnp.arange(KEYS_PAD)
    lane = jnp.arange(LANES)[None, :]
    kaug = []
    for c in range(N_TABS):
        masked = (r < (WINDOW // CHUNK - c) * CHUNK) | (r > n_keys)
        mask_col = jnp.where(masked, NEG, 0.0)[:, None] * (lane == 0)
        sink_row = (r == n_keys)[:, None]
        even = mask_col + jnp.where(sink_row & ((lane == 1) | (lane == 2)), 1.0, 0.0)
        odd = mask_col + jnp.where(sink_row & ((lane == 3) | (lane == 4)), 1.0, 0.0)
        kaug.append(jnp.concatenate([even, odd], axis=0))
    s2 = sinks.astype(F32) * LOG2E
    hi = s2.astype(BF16).astype(F32)
    lo = s2 - hi
    head = 2 * jnp.arange(Q_BLOCKS)
    cols = [jnp.ones((Q_BLOCKS,), F32), hi[head], lo[head], hi[head + 1], lo[head + 1]]
    qaug = jnp.stack(cols, axis=1)
    qaug = jnp.pad(qaug, ((0, 0), (0, LANES - qaug.shape[1])))
    qaug = jnp.broadcast_to(qaug[:, None, :], (Q_BLOCKS, CHUNK, LANES)).reshape(N_KV_HEADS, BLOCKS_PER_KV * CHUNK, LANES)
    return jnp.stack(kaug).astype(BF16), qaug.astype(BF16)


TM_POST = 512
POST_SUB = 256
FF_CHUNK = 256
FF_SKEW = 2
FF_LANE_BLOCKS = FF_CHUNK // LANES


def _pool_weight_kernel(wgrp_ref, pscale_ref, w_po_ref, out_ref):
    for g in range(N_POOL_GROUPS):
        sl = slice(g * POOL_GROUP_W, (g + 1) * POOL_GROUP_W)
        out_ref[sl, :] = _dot((wgrp_ref[g] * pscale_ref[:, sl]).astype(BF16), w_po_ref[sl, :]).astype(BF16)


def _pool_weight_call(wgrp, pscale, w_po):
    return pl.pallas_call(
        _pool_weight_kernel,
        out_shape=jax.ShapeDtypeStruct((POOL_WIDTH, D_MODEL), BF16),
        name="pool_weights",
    )(wgrp, pscale, w_po)


def _post_kernel(x_ref, o_ref, p_ref, pprev_ref, pmeta_ref, g_ref, upmeta_ref, w_ao_ref, w_pool_ref,
                 w_out_ref, gffn_ref, w_up_ref, convw_ref, convb_ref, w_down_ref, gfin_ref,
                 y_ref, carry_ref, pext_ref, x1_ref, hn_ref, *uext_refs):
    i = pl.program_id(1)
    tm = x_ref.shape[0]
    n_chunks = D_FF // FF_CHUNK

    @pl.when(i == 0)
    def _():
        pext_ref[0:N_META] = pmeta_ref[...]
        carry_ref[...] = upmeta_ref[...]

    @pl.when(i > 0)
    def _():
        pext_ref[0:N_META] = pprev_ref[...]

    pext_ref[N_META:] = p_ref[...]
    subs = [slice(r0, r0 + POST_SUB) for r0 in range(0, tm, POST_SUB)]

    mixes = []
    for rows in subs:
        attn = _dot(o_ref[rows], w_ao_ref[...])
        ext_rows = slice(rows.start, rows.stop + N_META)
        pm = jnp.concatenate([_pool_group(pext_ref[ext_rows, g * POOL_GROUP_W:(g + 1) * POOL_GROUP_W], g)
                              for g in range(N_POOL_GROUPS)], axis=1)
        pool = _dot(pm.astype(BF16), w_pool_ref[...])
        mix = g_ref[rows, :D_MODEL].astype(F32) * attn + g_ref[rows, D_MODEL:].astype(F32) * pool
        mixes.append(mix.astype(BF16))
    for rows, mix in zip(subs, mixes):
        x1 = x_ref[rows] + _dot(mix, w_out_ref[...])
        x1_ref[rows] = x1
        hn_ref[rows] = _rmsnorm(x1, gffn_ref[...]).astype(BF16)

    def up_project(rows, c):
        for half, off in enumerate((c * FF_CHUNK, D_FF + c * FF_CHUNK)):
            u = _dot(hn_ref[rows], w_up_ref[:, off:off + FF_CHUNK])
            for lb in range(FF_LANE_BLOCKS):
                blk = uext_refs[c].at[half * FF_LANE_BLOCKS + lb]
                if rows.start == 0:
                    blk[0:SUBLANES] = carry_ref[:, off + lb * LANES:off + (lb + 1) * LANES]
                blk[SUBLANES + rows.start:SUBLANES + rows.stop] = u[:, lb * LANES:(lb + 1) * LANES]
            if rows.stop == tm:
                carry_ref[:, off:off + FF_CHUNK] = u[POST_SUB - SUBLANES:]

    def conv(rows, c, half):
        outs = []
        for lb in range(FF_LANE_BLOCKS):
            off = half * D_FF + c * FF_CHUNK + lb * LANES
            w, b = convw_ref[:, off:off + LANES], convb_ref[:, off:off + LANES]
            ext = uext_refs[c].at[half * FF_LANE_BLOCKS + lb]
            base = SUBLANES + rows.start
            cv = b + ext[base - 2:base - 2 + POST_SUB] * w[0:1]
            cv = cv + ext[base - 1:base - 1 + POST_SUB] * w[1:2]
            outs.append(cv + ext[base:base + POST_SUB] * w[2:3])
        return jnp.concatenate(outs, axis=1)

    steps = [(rows, c) for rows in subs for c in range(n_chunks)]
    for step in steps[:FF_SKEW]:
        up_project(*step)
    acc = None
    for j, (rows, c) in enumerate(steps):
        act = (_silu(conv(rows, c, 0)) * conv(rows, c, 1)).astype(BF16)
        if j + FF_SKEW < len(steps):
            up_project(*steps[j + FF_SKEW])
        d = _dot(act, w_down_ref[c * FF_CHUNK:(c + 1) * FF_CHUNK, :])
        acc = d if c == 0 else acc + d
        if c == n_chunks - 1:
            y_ref[rows] = _rmsnorm(x1_ref[rows] + acc, gfin_ref[...])


def _post_call(x, o_attn, p, p_meta, gates, up_meta8, w_ao, w_pool, w_out, gffn, w_up, convw, convb, w_down, gfin):
    bsz, seq, _ = x.shape
    tm = TM_POST
    row = lambda w: pl.BlockSpec((None, tm, w), lambda b, i: (b, i, 0))
    pprev = pl.BlockSpec((None, N_META, POOL_WIDTH), lambda b, i: (b, jnp.maximum(i * (tm // N_META) - 1, 0), 0))
    consts = (p_meta,), (up_meta8, w_ao, w_pool, w_out, gffn, w_up, convw, convb, w_down, gfin)
    return pl.pallas_call(
        _post_kernel,
        grid=(bsz, seq // tm),
        in_specs=[row(D_MODEL), row(Q_W), row(POOL_WIDTH), pprev, _const_spec(p_meta.shape), row(GATE_W)]
                 + [_const_spec(a.shape) for a in consts[1]],
        out_specs=(row(D_MODEL), pl.BlockSpec((None, SUBLANES, 2 * D_FF), lambda b, i: (b, 0, 0))),
        out_shape=(jax.ShapeDtypeStruct((bsz, seq, D_MODEL), F32),
                   jax.ShapeDtypeStruct((bsz, SUBLANES, 2 * D_FF), F32)),
        scratch_shapes=[pltpu.VMEM((N_META + tm, POOL_WIDTH), F32), pltpu.VMEM((tm, D_MODEL), F32),
                        pltpu.VMEM((tm, D_MODEL), BF16)]
                       + [pltpu.VMEM((2 * FF_LANE_BLOCKS, SUBLANES + tm, LANES), F32)
                          for _ in range(D_FF // FF_CHUNK)],
        compiler_params=pltpu.CompilerParams(dimension_semantics=("arbitrary", "arbitrary"),
                                             vmem_limit_bytes=VMEM_LIMIT),
        name="mixer_ffn",
    )(x, o_attn, p, p, p_meta, gates, *consts[1])


def _rope_tables(pos):
    half = HEAD_DIM // 2
    inv = ROPE_THETA ** (-jnp.arange(half, dtype=F32) / half)
    ang = pos.astype(F32)[:, None] * inv[None, :]
    cos, sin = jnp.cos(ang), jnp.sin(ang)
    return jnp.tile(cos, (1, LANES // half)), jnp.tile(jnp.concatenate([-sin, sin], axis=1), (1, LANES // HEAD_DIM))


def _score_table(sinks, rows_per_block, n_keys, keys_pad, masked_prefix=0):
    col = jnp.arange(keys_pad)
    base = jnp.where((col >= masked_prefix) & (col < n_keys), 0.0, NEG).astype(F32)
    head = (2 * jnp.arange(Q_BLOCKS)[:, None] + jnp.arange(2)[None, :])
    tab = jnp.where(col[None, None, :] == n_keys, sinks.astype(F32)[head][:, :, None], base[None, None, :])
    tab = tab.reshape(N_KV_HEADS, BLOCKS_PER_KV, 1, 2 * keys_pad)
    tab = jnp.broadcast_to(tab, (N_KV_HEADS, BLOCKS_PER_KV, rows_per_block, 2 * keys_pad))
    return tab.reshape(N_KV_HEADS, BLOCKS_PER_KV * rows_per_block, 2 * keys_pad)


def kernel(x_prompt, x_sample, cache_swa_k, cache_swa_v, cache_meta_k, cache_meta_v, state_pool, state_conv,
           meta_tokens, g_norm_mix, w_in, b_gate, sinks, w_attn_o, w_pool_grp, pool_scale, w_pool_o, w_out,
           g_norm_ffn, w_up, conv_w, conv_b, w_down, g_norm_final):
    bsz, seq, _ = x_prompt.shape
    dbsz, t_dec, _ = x_sample.shape
    row2 = lambda a: a.reshape(1, -1)
    w_in_b, w_ao_b, wgrp_b, w_po_b = w_in.astype(BF16), w_attn_o.astype(BF16), w_pool_grp.astype(BF16), w_pool_o.astype(BF16)
    w_out_b, w_up_b, w_down_b = w_out.astype(BF16), w_up.astype(BF16), w_down.astype(BF16)
    gmix, gffn, gfin = row2(g_norm_mix), row2(g_norm_ffn), row2(g_norm_final)
    bgate, pscale, convb = row2(b_gate), row2(pool_scale), row2(conv_b)

    pos_side = jnp.concatenate([jnp.arange(N_META, dtype=jnp.int32),
                                jnp.tile(N_META + PAST_LEN + jnp.arange(t_dec, dtype=jnp.int32), dbsz)])
    cos_s, sin_s = _rope_tables(pos_side)
    xs = jnp.concatenate([meta_tokens, x_sample.reshape(dbsz * t_dec, D_MODEL)], axis=0)
    spool16 = jnp.pad(state_pool, ((0, 0), (N_META - POOL_HIST, 0), (0, 0)))
    sconv8 = jnp.pad(state_conv, ((0, 0), (SUBLANES - (CONV_W - 1), 0), (0, 0)))
    tab_meta = _score_table(sinks, N_META, N_META, LANES)
    tab_dec = _score_table(sinks, t_dec, N_META + WINDOW + t_dec, 2 * LANES)
    y_side, k_side, v_side, p_side, up_side = _side_call(
        xs, cache_swa_k.reshape(dbsz, WINDOW, KV_W), cache_swa_v.reshape(dbsz, WINDOW, KV_W),
        cache_meta_k.reshape(dbsz, N_META, KV_W), cache_meta_v.reshape(dbsz, N_META, KV_W), spool16, sconv8,
        cos_s, sin_s, tab_meta, tab_dec, gmix, w_in_b, bgate, w_ao_b, wgrp_b, pscale, w_po_b, w_out_b,
        gffn, w_up_b, conv_w, convb, w_down_b, gfin)
    km, vm, p_meta = k_side[:N_META], v_side[:N_META], p_side[:N_META]
    up_meta8 = up_side[N_META - SUBLANES:N_META]

    cos_p, sin_p = _rope_tables(N_META + jnp.arange(seq, dtype=jnp.int32))
    q, k, v, p, gates = _inproj_call(x_prompt, cos_p, sin_p, gmix, w_in_b, bgate)
    kaug, qaug = _attn_aug_tables(sinks)
    o_attn = _attn_call(q, k, v, km, vm, kaug, qaug)
    w_pool_b = _pool_weight_call(w_pool_grp, pscale, w_po_b)
    y_prompt, up_tail = _post_call(x_prompt, o_attn, p, p_meta, gates, up_meta8, w_ao_b, w_pool_b, w_out_b, gffn,
                                   w_up_b, conv_w, convb, w_down_b, gfin)

    kv4 = lambda a, n: a.reshape(a.shape[0], n, N_KV_HEADS, HEAD_DIM)
    dec = lambda a: a[N_META:].reshape(dbsz, t_dec, -1)
    return (
        y_prompt,
        dec(y_side),
        kv4(k[:, seq - WINDOW:], WINDOW),
        kv4(v[:, seq - WINDOW:], WINDOW),
        jnp.broadcast_to(km.reshape(1, N_META, N_KV_HEADS, HEAD_DIM), (bsz, N_META, N_KV_HEADS, HEAD_DIM)),
        jnp.broadcast_to(vm.reshape(1, N_META, N_KV_HEADS, HEAD_DIM), (bsz, N_META, N_KV_HEADS, HEAD_DIM)),
        p[:, seq - POOL_HIST:],
        up_tail[:, SUBLANES - (CONV_W - 1):],
        kv4(dec(k_side), t_dec),
        kv4(dec(v_side), t_dec),
        dec(p_side)[:, t_dec - POOL_HIST:],
        dec(up_side)[:, t_dec - (CONV_W - 1):],
    )
```

```python
import functools

import jax
import jax.numpy as jnp
from jax import lax
from jax.experimental import pallas as pl
from jax.experimental.pallas import tpu as pltpu

D_MODEL = 1024
N_META = 16
CHUNK = 64
HEAD_DIM = 64
N_Q_HEADS = 16
N_KV_HEADS = 2
WINDOW = 128
ROPE_THETA = 10000.0
POOL_WINDOWS = (2, 4, 8, 16)
N_POOL_GROUPS = 4
POOL_WIDTH = D_MODEL // 2
POOL_GROUP_W = POOL_WIDTH // N_POOL_GROUPS
POOL_HIST = max(POOL_WINDOWS) - 1
Q_W = N_Q_HEADS * HEAD_DIM
KV_W = N_KV_HEADS * HEAD_DIM
GATE_W = 2 * D_MODEL
IN_W = Q_W + 2 * KV_W + POOL_WIDTH + GATE_W
D_FF = ((8 * D_MODEL // 3) + 127) // 128 * 128
CONV_W = 3
RMS_EPS = 1e-6
PAST_LEN = 1024

LANES = 128
SUBLANES = 8
Q_BLOCKS = Q_W // LANES
BLOCKS_PER_KV = Q_BLOCKS // N_KV_HEADS
NEG = -1e30
LOG2E = 1.4426950408889634

K_OFF = Q_W
V_OFF = Q_W + KV_W
P_OFF = Q_W + 2 * KV_W
G_OFF = P_OFF + POOL_WIDTH

F32 = jnp.float32
BF16 = jnp.bfloat16

VMEM_LIMIT = 60 * 1024 * 1024


def _rmsnorm(x, g):
    ms = jnp.mean(x * x, axis=-1, keepdims=True)
    return (x * lax.rsqrt(ms + RMS_EPS)) * g


def _dot(a, b):
    return jnp.dot(a, b, preferred_element_type=F32)


def _rope(x, cos, sin_signed):
    lane = lax.broadcasted_iota(jnp.int32, (x.shape[0], LANES), 1)
    first_half = (lane & (HEAD_DIM // 2)) == 0
    outs = []
    for b in range(x.shape[1] // LANES):
        xb = x[:, b * LANES:(b + 1) * LANES]
        partner = jnp.where(first_half, pltpu.roll(xb, LANES - HEAD_DIM // 2, 1), pltpu.roll(xb, HEAD_DIM // 2, 1))
        outs.append(xb * cos + partner * sin_signed)
    return outs[0] if len(outs) == 1 else jnp.concatenate(outs, axis=1)


def _kv_variants(kv, dtype=BF16):
    lane = lax.broadcasted_iota(jnp.int32, kv.shape, 1)
    low = lane < HEAD_DIM
    swapped = pltpu.roll(kv, HEAD_DIM, 1)
    zero = jnp.zeros_like(kv)
    return (
        (jnp.where(low, kv, zero).astype(dtype), jnp.where(low, zero, swapped).astype(dtype)),
        (jnp.where(low, swapped, zero).astype(dtype), jnp.where(low, zero, kv).astype(dtype)),
    )


def _attend_group(qg, kst, vst, tab):
    s = lax.dot_general(qg, kst, (((1,), (1,)), ((), ())), preferred_element_type=F32) + tab
    npad = kst.shape[0] // 2
    se, so = s[:, :npad], s[:, npad:]
    pe = jnp.exp(se - jnp.max(se, axis=1, keepdims=True))
    po = jnp.exp(so - jnp.max(so, axis=1, keepdims=True))
    le = jnp.sum(pe, axis=1, keepdims=True)
    lo = jnp.sum(po, axis=1, keepdims=True)
    p = jnp.concatenate([pe, po], axis=1).astype(BF16)
    o = _dot(p, vst)
    lane = lax.broadcasted_iota(jnp.int32, o.shape, 1)
    return o * jnp.where(lane < HEAD_DIM, 1.0 / le, 1.0 / lo)


def _pool_group(ext, g, cnt=None):
    w = POOL_WINDOWS[g]
    s = ext
    k = 1
    while k < w:
        s = s + pltpu.roll(s, k, 0)
        k *= 2
    cur = ext[N_META:]
    return s[N_META:] / (float(w) if cnt is None else cnt) - cur


def _pool_project(pm, wgrp_ref, pscale):
    outs = []
    for g in range(N_POOL_GROUPS):
        sl = slice(g * POOL_GROUP_W, (g + 1) * POOL_GROUP_W)
        outs.append(_dot(pm[:, sl].astype(BF16), wgrp_ref[g]))
    return (jnp.concatenate(outs, axis=1) * pscale).astype(BF16)


def _conv3(u, prev8, w, b):
    t = u.shape[0]
    ext = jnp.concatenate([prev8, u], axis=0)
    u1 = pltpu.roll(ext, 1, 0)[SUBLANES:SUBLANES + t]
    u2 = pltpu.roll(ext, 2, 0)[SUBLANES:SUBLANES + t]
    c = b + u2 * w[0:1]
    c = c + u1 * w[1:2]
    return c + u * w[2:3]


def _sigmoid(x):
    return 0.5 * jnp.tanh(0.5 * x) + 0.5


def _silu(x):
    half = 0.5 * x
    return half + half * jnp.tanh(half)


N_SIDE = N_META + 8 * 16
SIDE_FF_CHUNK = 256


def _side_kernel(xs_ref, ck_ref, cv_ref, cmk_ref, cmv_ref, spool_ref, sconv_ref, cos_ref, sin_ref, tabm_ref,
                 tabs_ref, gmix_ref, w_in_ref, bgate_ref, w_ao_ref, wgrp_ref, pscale_ref, w_po_ref, w_out_ref,
                 gffn_ref, w_up_ref, convw_ref, convb_ref, w_down_ref, gfin_ref,
                 y_ref, k_ref, v_ref, p_ref, up_ref):
    dec_b = ck_ref.shape[0]
    t_dec = (N_SIDE - N_META) // dec_b
    x = xs_ref[...]
    h = _rmsnorm(x, gmix_ref[...]).astype(BF16)
    cos, sin = cos_ref[...], sin_ref[...]
    q = _rope(_dot(h, w_in_ref[:, 0:Q_W]) * (HEAD_DIM ** -0.5), cos, sin).astype(BF16)
    k = _rope(_dot(h, w_in_ref[:, K_OFF:K_OFF + KV_W]), cos, sin)
    v = _dot(h, w_in_ref[:, V_OFF:V_OFF + KV_W])
    p = _dot(h, w_in_ref[:, P_OFF:P_OFF + POOL_WIDTH])
    gates = _sigmoid(_dot(h, w_in_ref[:, G_OFF:G_OFF + GATE_W]) + bgate_ref[...])
    k_ref[...] = k
    v_ref[...] = v
    p_ref[...] = p

    def attend_rows(r0, nrows, kx, vx, tab_ref):
        kvar, vvar = _kv_variants(kx), _kv_variants(vx)
        blocks = [None] * Q_BLOCKS
        for n in range(N_KV_HEADS):
            qg = jnp.concatenate([q[r0:r0 + nrows, (BLOCKS_PER_KV * n + bi) * LANES:(BLOCKS_PER_KV * n + bi + 1) * LANES]
                                  for bi in range(BLOCKS_PER_KV)], axis=0)
            kst = jnp.concatenate(kvar[n], axis=0)
            vst = jnp.concatenate(vvar[n], axis=0)
            o = _attend_group(qg, kst, vst, tab_ref[n])
            for bi in range(BLOCKS_PER_KV):
                blocks[BLOCKS_PER_KV * n + bi] = o[bi * nrows:(bi + 1) * nrows]
        return jnp.concatenate(blocks, axis=1)

    zpad = jnp.zeros((LANES - N_META, LANES), F32)
    o_rows = [attend_rows(0, N_META, jnp.concatenate([k[:N_META], zpad], axis=0),
                          jnp.concatenate([v[:N_META], zpad], axis=0), tabm_ref)]
    n_keys = N_META + WINDOW + t_dec
    zpad = jnp.zeros((2 * LANES - n_keys, LANES), F32)
    for b in range(dec_b):
        r0 = N_META + b * t_dec
        kx = jnp.concatenate([cmk_ref[b], ck_ref[b], k[r0:r0 + t_dec], zpad], axis=0)
        vx = jnp.concatenate([cmv_ref[b], cv_ref[b], v[r0:r0 + t_dec], zpad], axis=0)
        o_rows.append(attend_rows(r0, t_dec, kx, vx, tabs_ref))
    o_attn = jnp.concatenate(o_rows, axis=0).astype(BF16)

    row = lax.broadcasted_iota(jnp.int32, (N_META, LANES), 0)
    pm_rows = []
    for seg in range(1 + dec_b):
        if seg == 0:
            ext = jnp.concatenate([jnp.zeros((N_META, POOL_WIDTH), F32), p[:N_META]], axis=0)
        else:
            r0 = N_META + (seg - 1) * t_dec
            ext = jnp.concatenate([spool_ref[seg - 1], p[r0:r0 + t_dec]], axis=0)
        groups = []
        for g in range(N_POOL_GROUPS):
            cnt = jnp.minimum(POOL_WINDOWS[g], row + 1).astype(F32) if seg == 0 else None
            groups.append(_pool_group(ext[:, g * POOL_GROUP_W:(g + 1) * POOL_GROUP_W], g, cnt))
        pm_rows.append(jnp.concatenate(groups, axis=1))
    pm = jnp.concatenate(pm_rows, axis=0)

    pool = _pool_project(pm, wgrp_ref, pscale_ref[...])
    mix = gates[:, :D_MODEL] * _dot(o_attn, w_ao_ref[...]) + gates[:, D_MODEL:] * _dot(pool, w_po_ref[...])
    x1 = x + _dot(mix.astype(BF16), w_out_ref[...])

    hn = _rmsnorm(x1, gffn_ref[...]).astype(BF16)
    acc = jnp.zeros((N_SIDE, D_MODEL), F32)
    zprev = jnp.zeros((SUBLANES, SIDE_FF_CHUNK), F32)
    for c0 in range(0, D_FF, SIDE_FF_CHUNK):
        halves = []
        for off in (c0, D_FF + c0):
            cs = slice(off, off + SIDE_FF_CHUNK)
            u = _dot(hn, w_up_ref[:, cs])
            up_ref[:, cs] = u
            w, bias = convw_ref[:, cs], convb_ref[:, cs]
            segs = [_conv3(u[:N_META], zprev, w, bias)]
            for b in range(dec_b):
                r0 = N_META + b * t_dec
                segs.append(_conv3(u[r0:r0 + t_dec], sconv_ref[b, :, cs], w, bias))
            halves.append(jnp.concatenate(segs, axis=0))
        act = (_silu(halves[0]) * halves[1]).astype(BF16)
        acc = acc + _dot(act, w_down_ref[c0:c0 + SIDE_FF_CHUNK, :])
    y_ref[...] = _rmsnorm(x1 + acc, gfin_ref[...])


def _side_call(*args):
    out_shape = (
        jax.ShapeDtypeStruct((N_SIDE, D_MODEL), F32),
        jax.ShapeDtypeStruct((N_SIDE, KV_W), F32),
        jax.ShapeDtypeStruct((N_SIDE, KV_W), F32),
        jax.ShapeDtypeStruct((N_SIDE, POOL_WIDTH), F32),
        jax.ShapeDtypeStruct((N_SIDE, 2 * D_FF), F32),
    )
    return pl.pallas_call(
        _side_kernel,
        out_shape=out_shape,
        compiler_params=pltpu.CompilerParams(vmem_limit_bytes=VMEM_LIMIT),
        name="side_rows",
    )(*args)


TM_IN = 1024
IN_SUB = 256
GATE_CHUNK = 512


def _inproj_kernel(x_ref, cos_ref, sin_ref, gmix_ref, w_in_ref, bgate_ref, q_ref, k_ref, v_ref, p_ref, g_ref):
    subs = [slice(r0, r0 + IN_SUB) for r0 in range(0, x_ref.shape[0], IN_SUB)]
    hs = [_rmsnorm(x_ref[rows], gmix_ref[...]).astype(BF16) for rows in subs]
    for rows, h in zip(subs, hs):
        cos, sin = cos_ref[rows], sin_ref[rows]
        for c0 in range(0, GATE_W, GATE_CHUNK):
            z = _dot(h, w_in_ref[:, G_OFF + c0:G_OFF + c0 + GATE_CHUNK]) + bgate_ref[:, c0:c0 + GATE_CHUNK]
            g_ref[rows, c0:c0 + GATE_CHUNK] = _sigmoid(z).astype(BF16)
        q_ref[rows] = _rope(_dot(h, w_in_ref[:, 0:Q_W]) * (HEAD_DIM ** -0.5 * LOG2E), cos, sin).astype(BF16)
        p_ref[rows] = _dot(h, w_in_ref[:, P_OFF:P_OFF + POOL_WIDTH])
        kv = _dot(h, w_in_ref[:, K_OFF:K_OFF + 2 * KV_W])
        k_ref[rows] = _rope(kv[:, :KV_W], cos, sin)
        v_ref[rows] = kv[:, KV_W:]


def _const_spec(shape):
    nd = len(shape)
    return pl.BlockSpec(shape, lambda *_: (0,) * nd)


def _inproj_call(x, cos, sin, gmix, w_in, bgate):
    bsz, seq, _ = x.shape
    tm = TM_IN
    row = lambda w: pl.BlockSpec((None, tm, w), lambda b, i: (b, i, 0))
    return pl.pallas_call(
        _inproj_kernel,
        grid=(bsz, seq // tm),
        in_specs=[row(D_MODEL), pl.BlockSpec((tm, LANES), lambda b, i: (i, 0)),
                  pl.BlockSpec((tm, LANES), lambda b, i: (i, 0)),
                  _const_spec(gmix.shape), _const_spec(w_in.shape), _const_spec(bgate.shape)],
        out_specs=(row(Q_W), row(KV_W), row(KV_W), row(POOL_WIDTH), row(GATE_W)),
        out_shape=(jax.ShapeDtypeStruct((bsz, seq, Q_W), BF16), jax.ShapeDtypeStruct((bsz, seq, KV_W), F32),
                   jax.ShapeDtypeStruct((bsz, seq, KV_W), F32), jax.ShapeDtypeStruct((bsz, seq, POOL_WIDTH), F32),
                   jax.ShapeDtypeStruct((bsz, seq, GATE_W), BF16)),
        compiler_params=pltpu.CompilerParams(dimension_semantics=("arbitrary", "arbitrary"),
                                             vmem_limit_bytes=VMEM_LIMIT),
        name="in_proj",
    )(x, cos, sin, gmix, w_in, bgate)


TQ = 512
BAND = WINDOW + CHUNK
KEYS_PAD = BAND + N_META + 16
N_TABS = WINDOW // CHUNK + 1
SUM_ROWS = 16


def _attention_stages(i, q_ref, kc_ref, kp_ref, vc_ref, vp_ref, mk_ref, mv_ref, kaug_ref, qaug_ref, store):
    kvar = _kv_variants(jnp.concatenate([kp_ref[...], kc_ref[...]], axis=0), BF16)
    vvar = _kv_variants(jnp.concatenate([vp_ref[...], vc_ref[...]], axis=0), F32)
    mkvar, mvvar = _kv_variants(mk_ref[...], BF16), _kv_variants(mv_ref[...], F32)
    kpad = jnp.zeros((KEYS_PAD - BAND - N_META, LANES), BF16)
    vpad = jnp.zeros((KEYS_PAD - BAND - N_META, LANES), F32)
    row = lax.broadcasted_iota(jnp.int32, (SUM_ROWS, 2 * KEYS_PAD), 0)
    col = lax.broadcasted_iota(jnp.int32, (SUM_ROWS, 2 * KEYS_PAD), 1)
    ones = jnp.where((row == 0) & (col < KEYS_PAD) | (row == 1) & (col >= KEYS_PAD), 1.0, 0.0).astype(BF16)

    def scores(j, n):
        tsel = jnp.minimum(i * (TQ // CHUNK) + j, N_TABS - 1)
        rows = slice(j * CHUNK, (j + 1) * CHUNK)
        band = slice(j * CHUNK, j * CHUNK + BAND)
        kst = jnp.concatenate([kvar[n][0][band], mkvar[n][0], kpad, kvar[n][1][band], mkvar[n][1], kpad], axis=0)
        vst = jnp.concatenate([vvar[n][0][band], mvvar[n][0], vpad, vvar[n][1][band], mvvar[n][1], vpad], axis=0)
        qg = jnp.concatenate([q_ref[rows, (BLOCKS_PER_KV * n + bi) * LANES:(BLOCKS_PER_KV * n + bi + 1) * LANES]
                              for bi in range(BLOCKS_PER_KV)], axis=0)
        s = lax.dot_general(jnp.concatenate([kst, kaug_ref[tsel]], axis=1),
                            jnp.concatenate([qg, qaug_ref[n]], axis=1),
                            (((1,), (1,)), ((), ())), preferred_element_type=F32)
        vt = jnp.concatenate([vst.T.astype(BF16), ones], axis=0)
        return s, vt

    def finish(j, n, s, vt):
        se, so = s[:KEYS_PAD], s[KEYS_PAD:]
        pe = jnp.exp2(se - jnp.max(se, axis=0, keepdims=True))
        po = jnp.exp2(so - jnp.max(so, axis=0, keepdims=True))
        o = _dot(vt, jnp.concatenate([pe, po], axis=0).astype(BF16))
        o = jnp.concatenate([o[:HEAD_DIM] / o[LANES:LANES + 1], o[HEAD_DIM:LANES] / o[LANES + 1:LANES + 2]], axis=0).T
        for bi in range(BLOCKS_PER_KV):
            store(j, BLOCKS_PER_KV * n + bi, o[bi * CHUNK:(bi + 1) * CHUNK].astype(BF16))

    groups = [(j, n) for j in range(TQ // CHUNK) for n in range(N_KV_HEADS)]
    return groups, scores, finish


def _attn_aug_tables(sinks):
    n_keys = BAND + N_META
    r = jnp.arange(KEYS_PAD)
    lane = jnp.arange(LANES)[None, :]
    kaug = []
    for c in range(N_TABS):
        masked = (r < (WINDOW // CHUNK - c) * CHUNK) | (r > n_keys)
        mask_col = jnp.where(masked, NEG, 0.0)[:, None] * (lane == 0)
        sink_row = (r == n_keys)[:, None]
        even = mask_col + jnp.where(sink_row & ((lane == 1) | (lane == 2)), 1.0, 0.0)
        odd = mask_col + jnp.where(sink_row & ((lane == 3) | (lane == 4)), 1.0, 0.0)
        kaug.append(jnp.concatenate([even, odd], axis=0))
    s2 = sinks.astype(F32) * LOG2E
    hi = s2.astype(BF16).astype(F32)
    lo = s2 - hi
    head = 2 * jnp.arange(Q_BLOCKS)
    cols = [jnp.ones((Q_BLOCKS,), F32), hi[head], lo[head], hi[head + 1], lo[head + 1]]
    qaug = jnp.stack(cols, axis=1)
    qaug = jnp.pad(qaug, ((0, 0), (0, LANES - qaug.shape[1])))
    qaug = jnp.broadcast_to(qaug[:, None, :], (Q_BLOCKS, CHUNK, LANES)).reshape(N_KV_HEADS, BLOCKS_PER_KV * CHUNK, LANES)
    return jnp.stack(kaug).astype(BF16), qaug.astype(BF16)


TM_POST = 512
POST_SUB = 256
FF_CHUNK = 256
FF_SKEW = 2
FF_LANE_BLOCKS = FF_CHUNK // LANES


def _pool_weight_kernel(wgrp_ref, pscale_ref, w_po_ref, out_ref):
    for g in range(N_POOL_GROUPS):
        sl = slice(g * POOL_GROUP_W, (g + 1) * POOL_GROUP_W)
        out_ref[sl, :] = _dot((wgrp_ref[g] * pscale_ref[:, sl]).astype(BF16), w_po_ref[sl, :]).astype(BF16)


def _pool_weight_call(wgrp, pscale, w_po):
    return pl.pallas_call(
        _pool_weight_kernel,
        out_shape=jax.ShapeDtypeStruct((POOL_WIDTH, D_MODEL), BF16),
        name="pool_weights",
    )(wgrp, pscale, w_po)


def _stream_kernel(tiles_per_batch, n_tiles,
                   q_ref, kc_ref, kp_ref, vc_ref, vp_ref, mk_ref, mv_ref, kaug_ref, qaug_ref,
                   x_ref, p_ref, pprev_ref, pmeta_ref, g_ref, upmeta_ref, w_ao_ref, w_pool_ref,
                   w_out_ref, gffn_ref, w_up_ref, convw_ref, convb_ref, w_down_ref, gfin_ref,
                   y_ref, carry_ref, oattn_ref, pext_ref, x1_ref, hn_ref, *uext_refs):
    t = pl.program_id(0)
    i_attn = lax.rem(jnp.minimum(t, n_tiles - 1), tiles_per_batch)
    i = lax.rem(jnp.maximum(t - 1, 0), tiles_per_batch)
    slot_new, slot_old = lax.rem(t, 2), lax.rem(t + 1, 2)
    tm = x_ref.shape[0]
    n_chunks = D_FF // FF_CHUNK

    @pl.when(t == 0)
    def _():
        oattn_ref[...] = jnp.zeros_like(oattn_ref)

    @pl.when(i == 0)
    def _():
        pext_ref[0:N_META] = pmeta_ref[...]
        for c in range(n_chunks):
            for half, off in enumerate((c * FF_CHUNK, D_FF + c * FF_CHUNK)):
                for lb in range(FF_LANE_BLOCKS):
                    uext_refs[c][half * FF_LANE_BLOCKS + lb, POST_SUB:POST_SUB + SUBLANES] = (
                        upmeta_ref[:, off + lb * LANES:off + (lb + 1) * LANES])

    @pl.when(i > 0)
    def _():
        pext_ref[0:N_META] = pprev_ref[...]

    def store_attn(j, blk, value):
        oattn_ref[slot_new, j * CHUNK:(j + 1) * CHUNK, blk * LANES:(blk + 1) * LANES] = value

    groups, scores, finish = _attention_stages(i_attn, q_ref, kc_ref, kp_ref, vc_ref, vp_ref, mk_ref, mv_ref,
                                               kaug_ref, qaug_ref, store_attn)
    pending = [None]

    def attention_group(g):
        current = pending[0]
        if g + 1 < len(groups):
            pending[0] = scores(*groups[g + 1])
        finish(*groups[g], *current)

    pext_ref[N_META:] = p_ref[...]
    subs = [slice(r0, r0 + POST_SUB) for r0 in range(0, tm, POST_SUB)]

    mixes = []
    for rows in subs:
        attn = _dot(oattn_ref[slot_old, rows], w_ao_ref[...])
        ext_rows = slice(rows.start, rows.stop + N_META)
        pm = jnp.concatenate([_pool_group(pext_ref[ext_rows, g * POOL_GROUP_W:(g + 1) * POOL_GROUP_W], g)
                              for g in range(N_POOL_GROUPS)], axis=1)
        pool = _dot(pm.astype(BF16), w_pool_ref[...])
        mix = g_ref[rows, :D_MODEL].astype(F32) * attn + g_ref[rows, D_MODEL:].astype(F32) * pool
        mixes.append(mix.astype(BF16))
    for rows, mix in zip(subs, mixes):
        x1 = x_ref[rows] + _dot(mix, w_out_ref[...])
        x1_ref[rows] = x1
        hn_ref[rows] = _rmsnorm(x1, gffn_ref[...]).astype(BF16)

    def up_project(rows, c):
        for half, off in enumerate((c * FF_CHUNK, D_FF + c * FF_CHUNK)):
            u = _dot(hn_ref[rows], w_up_ref[:, off:off + FF_CHUNK])
            for lb in range(FF_LANE_BLOCKS):
                blk = uext_refs[c].at[half * FF_LANE_BLOCKS + lb]
                blk[0:SUBLANES] = blk[POST_SUB:POST_SUB + SUBLANES]
                blk[SUBLANES:] = u[:, lb * LANES:(lb + 1) * LANES]
            if rows.stop == tm:
                carry_ref[:, off:off + FF_CHUNK] = u[POST_SUB - SUBLANES:]

    def conv(rows, c, half):
        outs = []
        for lb in range(FF_LANE_BLOCKS):
            off = half * D_FF + c * FF_CHUNK + lb * LANES
            w, b = convw_ref[:, off:off + LANES], convb_ref[:, off:off + LANES]
            ext = uext_refs[c].at[half * FF_LANE_BLOCKS + lb]
            cv = b + ext[SUBLANES - 2:SUBLANES - 2 + POST_SUB] * w[0:1]
            cv = cv + ext[SUBLANES - 1:SUBLANES - 1 + POST_SUB] * w[1:2]
            outs.append(cv + ext[SUBLANES:] * w[2:3])
        return jnp.concatenate(outs, axis=1)

    steps = [(rows, c) for rows in subs for c in range(n_chunks)]
    for step in steps[:FF_SKEW]:
        up_project(*step)
    pending[0] = scores(*groups[0])
    acc = None
    for j, (rows, c) in enumerate(steps):
        act = (_silu(conv(rows, c, 0)) * conv(rows, c, 1)).astype(BF16)
        if j + FF_SKEW < len(steps):
            up_project(*steps[j + FF_SKEW])
        d = _dot(act, w_down_ref[c * FF_CHUNK:(c + 1) * FF_CHUNK, :])
        acc = d if c == 0 else acc + d
        if c == n_chunks - 1:
            y_ref[rows] = _rmsnorm(x1_ref[rows] + acc, gfin_ref[...])
        if j < len(groups):
            attention_group(j)
    assert len(groups) <= len(steps)


def _stream_call(q, k, v, mk, mv, kaug, qaug, x, p, p_meta, gates, up_meta8, w_ao, w_pool, w_out, gffn, w_up, convw,
                 convb, w_down, gfin):
    bsz, seq, _ = x.shape
    tm = TM_POST
    assert tm == TQ
    tiles_per_batch = seq // tm
    n_tiles = bsz * tiles_per_batch

    def new_tile(t):
        a = jnp.minimum(t, n_tiles - 1)
        return a // tiles_per_batch, lax.rem(a, tiles_per_batch)

    def old_tile(t):
        c = jnp.maximum(t - 1, 0)
        return c // tiles_per_batch, lax.rem(c, tiles_per_batch)

    new_row = lambda w: pl.BlockSpec((None, tm, w), lambda t: (*new_tile(t), 0))
    old_row = lambda w: pl.BlockSpec((None, tm, w), lambda t: (*old_tile(t), 0))

    def kv_prev_map(t):
        b, i = new_tile(t)
        return b, jnp.maximum(i * (tm // WINDOW) - 1, 0), 0

    def p_prev_map(t):
        b, i = old_tile(t)
        return b, jnp.maximum(i * (tm // N_META) - 1, 0), 0

    kv_prev = pl.BlockSpec((None, WINDOW, KV_W), kv_prev_map)
    consts = (up_meta8, w_ao, w_pool, w_out, gffn, w_up, convw, convb, w_down, gfin)
    return pl.pallas_call(
        functools.partial(_stream_kernel, tiles_per_batch, n_tiles),
        grid=(n_tiles + 1,),
        in_specs=[new_row(Q_W), new_row(KV_W), kv_prev, new_row(KV_W), kv_prev, _const_spec(mk.shape),
                  _const_spec(mv.shape), _const_spec(kaug.shape), _const_spec(qaug.shape),
                  old_row(D_MODEL), old_row(POOL_WIDTH), pl.BlockSpec((None, N_META, POOL_WIDTH), p_prev_map),
                  _const_spec(p_meta.shape), old_row(GATE_W)] + [_const_spec(a.shape) for a in consts],
        out_specs=(old_row(D_MODEL), pl.BlockSpec((None, SUBLANES, 2 * D_FF), lambda t: (old_tile(t)[0], 0, 0))),
        out_shape=(jax.ShapeDtypeStruct((bsz, seq, D_MODEL), F32),
                   jax.ShapeDtypeStruct((bsz, SUBLANES, 2 * D_FF), F32)),
        scratch_shapes=[pltpu.VMEM((2, tm, Q_W), BF16), pltpu.VMEM((N_META + tm, POOL_WIDTH), F32),
                        pltpu.VMEM((tm, D_MODEL), F32), pltpu.VMEM((tm, D_MODEL), BF16)]
                       + [pltpu.VMEM((2 * FF_LANE_BLOCKS, SUBLANES + POST_SUB, LANES), F32)
                          for _ in range(D_FF // FF_CHUNK)],
        compiler_params=pltpu.CompilerParams(dimension_semantics=("arbitrary",), vmem_limit_bytes=VMEM_LIMIT),
        name="attn_mixer_ffn",
    )(q, k, k, v, v, mk, mv, kaug, qaug, x, p, p, p_meta, gates, *consts)


def _rope_tables(pos):
    half = HEAD_DIM // 2
    inv = ROPE_THETA ** (-jnp.arange(half, dtype=F32) / half)
    ang = pos.astype(F32)[:, None] * inv[None, :]
    cos, sin = jnp.cos(ang), jnp.sin(ang)
    return jnp.tile(cos, (1, LANES // half)), jnp.tile(jnp.concatenate([-sin, sin], axis=1), (1, LANES // HEAD_DIM))


def _score_table(sinks, rows_per_block, n_keys, keys_pad, masked_prefix=0):
    col = jnp.arange(keys_pad)
    base = jnp.where((col >= masked_prefix) & (col < n_keys), 0.0, NEG).astype(F32)
    head = (2 * jnp.arange(Q_BLOCKS)[:, None] + jnp.arange(2)[None, :])
    tab = jnp.where(col[None, None, :] == n_keys, sinks.astype(F32)[head][:, :, None], base[None, None, :])
    tab = tab.reshape(N_KV_HEADS, BLOCKS_PER_KV, 1, 2 * keys_pad)
    tab = jnp.broadcast_to(tab, (N_KV_HEADS, BLOCKS_PER_KV, rows_per_block, 2 * keys_pad))
    return tab.reshape(N_KV_HEADS, BLOCKS_PER_KV * rows_per_block, 2 * keys_pad)


def kernel(x_prompt, x_sample, cache_swa_k, cache_swa_v, cache_meta_k, cache_meta_v, state_pool, state_conv,
           meta_tokens, g_norm_mix, w_in, b_gate, sinks, w_attn_o, w_pool_grp, pool_scale, w_pool_o, w_out,
           g_norm_ffn, w_up, conv_w, conv_b, w_down, g_norm_final):
    bsz, seq, _ = x_prompt.shape
    dbsz, t_dec, _ = x_sample.shape
    row2 = lambda a: a.reshape(1, -1)
    w_in_b, w_ao_b, wgrp_b, w_po_b = w_in.astype(BF16), w_attn_o.astype(BF16), w_pool_grp.astype(BF16), w_pool_o.astype(BF16)
    w_out_b, w_up_b, w_down_b = w_out.astype(BF16), w_up.astype(BF16), w_down.astype(BF16)
    gmix, gffn, gfin = row2(g_norm_mix), row2(g_norm_ffn), row2(g_norm_final)
    bgate, pscale, convb = row2(b_gate), row2(pool_scale), row2(conv_b)

    pos_side = jnp.concatenate([jnp.arange(N_META, dtype=jnp.int32),
                                jnp.tile(N_META + PAST_LEN + jnp.arange(t_dec, dtype=jnp.int32), dbsz)])
    cos_s, sin_s = _rope_tables(pos_side)
    xs = jnp.concatenate([meta_tokens, x_sample.reshape(dbsz * t_dec, D_MODEL)], axis=0)
    spool16 = jnp.pad(state_pool, ((0, 0), (N_META - POOL_HIST, 0), (0, 0)))
    sconv8 = jnp.pad(state_conv, ((0, 0), (SUBLANES - (CONV_W - 1), 0), (0, 0)))
    tab_meta = _score_table(sinks, N_META, N_META, LANES)
    tab_dec = _score_table(sinks, t_dec, N_META + WINDOW + t_dec, 2 * LANES)
    y_side, k_side, v_side, p_side, up_side = _side_call(
        xs, cache_swa_k.reshape(dbsz, WINDOW, KV_W), cache_swa_v.reshape(dbsz, WINDOW, KV_W),
        cache_meta_k.reshape(dbsz, N_META, KV_W), cache_meta_v.reshape(dbsz, N_META, KV_W), spool16, sconv8,
        cos_s, sin_s, tab_meta, tab_dec, gmix, w_in_b, bgate, w_ao_b, wgrp_b, pscale, w_po_b, w_out_b,
        gffn, w_up_b, conv_w, convb, w_down_b, gfin)
    km, vm, p_meta = k_side[:N_META], v_side[:N_META], p_side[:N_META]
    up_meta8 = up_side[N_META - SUBLANES:N_META]

    cos_p, sin_p = _rope_tables(N_META + jnp.arange(seq, dtype=jnp.int32))
    q, k, v, p, gates = _inproj_call(x_prompt, cos_p, sin_p, gmix, w_in_b, bgate)
    kaug, qaug = _attn_aug_tables(sinks)
    w_pool_b = _pool_weight_call(w_pool_grp, pscale, w_po_b)
    y_prompt, up_tail = _stream_call(q, k, v, km, vm, kaug, qaug, x_prompt, p, p_meta, gates, up_meta8, w_ao_b,
                                     w_pool_b, w_out_b, gffn, w_up_b, conv_w, convb, w_down_b, gfin)

    kv4 = lambda a, n: a.reshape(a.shape[0], n, N_KV_HEADS, HEAD_DIM)
    dec = lambda a: a[N_META:].reshape(dbsz, t_dec, -1)
    return (
        y_prompt,
        dec(y_side),
        kv4(k[:, seq - WINDOW:], WINDOW),
        kv4(v[:, seq - WINDOW:], WINDOW),
        jnp.broadcast_to(km.reshape(1, N_META, N_KV_HEADS, HEAD_DIM), (bsz, N_META, N_KV_HEADS, HEAD_DIM)),
        jnp.broadcast_to(vm.reshape(1, N_META, N_KV_HEADS, HEAD_DIM), (bsz, N_META, N_KV_HEADS, HEAD_DIM)),
        p[:, seq - POOL_HIST:],
        up_tail[:, SUBLANES - (CONV_W - 1):],
        kv4(dec(k_side), t_dec),
        kv4(dec(v_side), t_dec),
        dec(p_side)[:, t_dec - POOL_HIST:],
        dec(up_side)[:, t_dec - (CONV_W - 1):],
    )
```

```python
import functools

import jax
import jax.numpy as jnp
from jax import lax
from jax.experimental import pallas as pl
from jax.experimental.pallas import tpu as pltpu

D_MODEL = 1024
N_META = 16
CHUNK = 64
HEAD_DIM = 64
N_Q_HEADS = 16
N_KV_HEADS = 2
WINDOW = 128
ROPE_THETA = 10000.0
POOL_WINDOWS = (2, 4, 8, 16)
N_POOL_GROUPS = 4
POOL_WIDTH = D_MODEL // 2
POOL_GROUP_W = POOL_WIDTH // N_POOL_GROUPS
POOL_HIST = max(POOL_WINDOWS) - 1
Q_W = N_Q_HEADS * HEAD_DIM
KV_W = N_KV_HEADS * HEAD_DIM
GATE_W = 2 * D_MODEL
IN_W = Q_W + 2 * KV_W + POOL_WIDTH + GATE_W
D_FF = ((8 * D_MODEL // 3) + 127) // 128 * 128
CONV_W = 3
RMS_EPS = 1e-6
PAST_LEN = 1024

LANES = 128
SUBLANES = 8
Q_BLOCKS = Q_W // LANES
BLOCKS_PER_KV = Q_BLOCKS // N_KV_HEADS
NEG = -1e30
LOG2E = 1.4426950408889634

K_OFF = Q_W
V_OFF = Q_W + KV_W
P_OFF = Q_W + 2 * KV_W
G_OFF = P_OFF + POOL_WIDTH

F32 = jnp.float32
BF16 = jnp.bfloat16

VMEM_LIMIT = 60 * 1024 * 1024


def _rmsnorm(x, g):
    ms = jnp.mean(x * x, axis=-1, keepdims=True)
    return (x * lax.rsqrt(ms + RMS_EPS)) * g


def _dot(a, b):
    return jnp.dot(a, b, preferred_element_type=F32)


def _rope(x, cos, sin_signed):
    lane = lax.broadcasted_iota(jnp.int32, (x.shape[0], LANES), 1)
    first_half = (lane & (HEAD_DIM // 2)) == 0
    outs = []
    for b in range(x.shape[1] // LANES):
        xb = x[:, b * LANES:(b + 1) * LANES]
        partner = jnp.where(first_half, pltpu.roll(xb, LANES - HEAD_DIM // 2, 1), pltpu.roll(xb, HEAD_DIM // 2, 1))
        outs.append(xb * cos + partner * sin_signed)
    return outs[0] if len(outs) == 1 else jnp.concatenate(outs, axis=1)


def _kv_variants(kv, dtype=BF16):
    lane = lax.broadcasted_iota(jnp.int32, kv.shape, 1)
    low = lane < HEAD_DIM
    swapped = pltpu.roll(kv, HEAD_DIM, 1)
    zero = jnp.zeros_like(kv)
    return (
        (jnp.where(low, kv, zero).astype(dtype), jnp.where(low, zero, swapped).astype(dtype)),
        (jnp.where(low, swapped, zero).astype(dtype), jnp.where(low, zero, kv).astype(dtype)),
    )


def _attend_group(qg, kst, vst, tab):
    s = lax.dot_general(qg, kst, (((1,), (1,)), ((), ())), preferred_element_type=F32) + tab
    npad = kst.shape[0] // 2
    se, so = s[:, :npad], s[:, npad:]
    pe = jnp.exp(se - jnp.max(se, axis=1, keepdims=True))
    po = jnp.exp(so - jnp.max(so, axis=1, keepdims=True))
    le = jnp.sum(pe, axis=1, keepdims=True)
    lo = jnp.sum(po, axis=1, keepdims=True)
    p = jnp.concatenate([pe, po], axis=1).astype(BF16)
    o = _dot(p, vst)
    lane = lax.broadcasted_iota(jnp.int32, o.shape, 1)
    return o * jnp.where(lane < HEAD_DIM, 1.0 / le, 1.0 / lo)


def _pool_group(ext, g, cnt=None):
    w = POOL_WINDOWS[g]
    s = ext
    k = 1
    while k < w:
        s = s + pltpu.roll(s, k, 0)
        k *= 2
    cur = ext[N_META:]
    return s[N_META:] / (float(w) if cnt is None else cnt) - cur


def _pool_project(pm, wgrp_ref, pscale):
    outs = []
    for g in range(N_POOL_GROUPS):
        sl = slice(g * POOL_GROUP_W, (g + 1) * POOL_GROUP_W)
        outs.append(_dot(pm[:, sl].astype(BF16), wgrp_ref[g].astype(BF16)))
    return (jnp.concatenate(outs, axis=1) * pscale).astype(BF16)


def _conv3(u, prev8, w, b):
    t = u.shape[0]
    ext = jnp.concatenate([prev8, u], axis=0)
    u1 = pltpu.roll(ext, 1, 0)[SUBLANES:SUBLANES + t]
    u2 = pltpu.roll(ext, 2, 0)[SUBLANES:SUBLANES + t]
    c = b + u2 * w[0:1]
    c = c + u1 * w[1:2]
    return c + u * w[2:3]


def _sigmoid(x):
    return 0.5 * jnp.tanh(0.5 * x) + 0.5


def _silu(x):
    half = 0.5 * x
    return half + half * jnp.tanh(half)


N_SIDE = N_META + 8 * 16
SIDE_FF_CHUNK = 256


def _side_kernel(xs_ref, ck_ref, cv_ref, cmk_ref, cmv_ref, spool_ref, sconv_ref, cos_ref, sin_ref, tabm_ref,
                 tabs_ref, gmix_ref, w_in_ref, bgate_ref, w_ao_ref, wgrp_ref, pscale_ref, w_po_ref, w_out_ref,
                 gffn_ref, w_up_ref, convw_ref, convb_ref, w_down_ref, gfin_ref,
                 y_ref, k_ref, v_ref, p_ref, up_ref):
    dec_b = ck_ref.shape[0]
    t_dec = (N_SIDE - N_META) // dec_b
    x = xs_ref[...]
    h = _rmsnorm(x, gmix_ref[...]).astype(BF16)
    cos, sin = cos_ref[...], sin_ref[...]
    w_in = lambda c0, c1: w_in_ref[:, c0:c1].astype(BF16)
    q = _rope(_dot(h, w_in(0, Q_W)) * (HEAD_DIM ** -0.5), cos, sin).astype(BF16)
    k = _rope(_dot(h, w_in(K_OFF, K_OFF + KV_W)), cos, sin)
    v = _dot(h, w_in(V_OFF, V_OFF + KV_W))
    p = _dot(h, w_in(P_OFF, P_OFF + POOL_WIDTH))
    gates = _sigmoid(jnp.concatenate([_dot(h, w_in(G_OFF + c0, G_OFF + c0 + GATE_CHUNK))
                                      for c0 in range(0, GATE_W, GATE_CHUNK)], axis=1) + bgate_ref[...])
    k_ref[...] = k
    v_ref[...] = v
    p_ref[...] = p

    def attend_rows(r0, nrows, kx, vx, tab_ref):
        kvar, vvar = _kv_variants(kx), _kv_variants(vx)
        blocks = [None] * Q_BLOCKS
        for n in range(N_KV_HEADS):
            qg = jnp.concatenate([q[r0:r0 + nrows, (BLOCKS_PER_KV * n + bi) * LANES:(BLOCKS_PER_KV * n + bi + 1) * LANES]
                                  for bi in range(BLOCKS_PER_KV)], axis=0)
            kst = jnp.concatenate(kvar[n], axis=0)
            vst = jnp.concatenate(vvar[n], axis=0)
            o = _attend_group(qg, kst, vst, tab_ref[n])
            for bi in range(BLOCKS_PER_KV):
                blocks[BLOCKS_PER_KV * n + bi] = o[bi * nrows:(bi + 1) * nrows]
        return jnp.concatenate(blocks, axis=1)

    zpad = jnp.zeros((LANES - N_META, LANES), F32)
    o_rows = [attend_rows(0, N_META, jnp.concatenate([k[:N_META], zpad], axis=0),
                          jnp.concatenate([v[:N_META], zpad], axis=0), tabm_ref)]
    n_keys = N_META + WINDOW + t_dec
    zpad = jnp.zeros((2 * LANES - n_keys, LANES), F32)
    for b in range(dec_b):
        r0 = N_META + b * t_dec
        kx = jnp.concatenate([cmk_ref[b], ck_ref[b], k[r0:r0 + t_dec], zpad], axis=0)
        vx = jnp.concatenate([cmv_ref[b], cv_ref[b], v[r0:r0 + t_dec], zpad], axis=0)
        o_rows.append(attend_rows(r0, t_dec, kx, vx, tabs_ref))
    o_attn = jnp.concatenate(o_rows, axis=0).astype(BF16)

    row = lax.broadcasted_iota(jnp.int32, (N_META, LANES), 0)
    pm_rows = []
    for seg in range(1 + dec_b):
        if seg == 0:
            ext = jnp.concatenate([jnp.zeros((N_META, POOL_WIDTH), F32), p[:N_META]], axis=0)
        else:
            r0 = N_META + (seg - 1) * t_dec
            ext = jnp.concatenate([spool_ref[seg - 1], p[r0:r0 + t_dec]], axis=0)
        groups = []
        for g in range(N_POOL_GROUPS):
            cnt = jnp.minimum(POOL_WINDOWS[g], row + 1).astype(F32) if seg == 0 else None
            groups.append(_pool_group(ext[:, g * POOL_GROUP_W:(g + 1) * POOL_GROUP_W], g, cnt))
        pm_rows.append(jnp.concatenate(groups, axis=1))
    pm = jnp.concatenate(pm_rows, axis=0)

    pool = _pool_project(pm, wgrp_ref, pscale_ref[...])
    mix = gates[:, :D_MODEL] * _dot(o_attn, w_ao_ref[...]) + gates[:, D_MODEL:] * _dot(pool, w_po_ref[...])
    x1 = x + _dot(mix.astype(BF16), w_out_ref[...])

    hn = _rmsnorm(x1, gffn_ref[...]).astype(BF16)
    acc = jnp.zeros((N_SIDE, D_MODEL), F32)
    zprev = jnp.zeros((SUBLANES, SIDE_FF_CHUNK), F32)
    for c0 in range(0, D_FF, SIDE_FF_CHUNK):
        halves = []
        for off in (c0, D_FF + c0):
            cs = slice(off, off + SIDE_FF_CHUNK)
            u = _dot(hn, w_up_ref[:, cs])
            up_ref[:, cs] = u
            w, bias = convw_ref[:, cs], convb_ref[:, cs]
            segs = [_conv3(u[:N_META], zprev, w, bias)]
            for b in range(dec_b):
                r0 = N_META + b * t_dec
                segs.append(_conv3(u[r0:r0 + t_dec], sconv_ref[b, :, cs], w, bias))
            halves.append(jnp.concatenate(segs, axis=0))
        act = (_silu(halves[0]) * halves[1]).astype(BF16)
        acc = acc + _dot(act, w_down_ref[c0:c0 + SIDE_FF_CHUNK, :])
    y_ref[...] = _rmsnorm(x1 + acc, gfin_ref[...])


def _side_call(*args):
    out_shape = (
        jax.ShapeDtypeStruct((N_SIDE, D_MODEL), F32),
        jax.ShapeDtypeStruct((N_SIDE, KV_W), F32),
        jax.ShapeDtypeStruct((N_SIDE, KV_W), F32),
        jax.ShapeDtypeStruct((N_SIDE, POOL_WIDTH), F32),
        jax.ShapeDtypeStruct((N_SIDE, 2 * D_FF), F32),
    )
    return pl.pallas_call(
        _side_kernel,
        out_shape=out_shape,
        compiler_params=pltpu.CompilerParams(vmem_limit_bytes=VMEM_LIMIT),
        name="side_rows",
    )(*args)


TM_IN = 1024
IN_SUB = 256
GATE_CHUNK = 512


W_IN_CAST_CHUNK = 768


def _inproj_kernel(n_cast, x_ref, cos_ref, sin_ref, gmix_ref, w_in_f32_ref, bgate_ref, *refs):
    cast_in, (q_ref, k_ref, v_ref, p_ref, g_ref) = refs[:n_cast], refs[n_cast:n_cast + 5]
    cast_out, w_in_ref = refs[n_cast + 5:2 * n_cast + 5], refs[2 * n_cast + 5]

    @pl.when((pl.program_id(0) == 0) & (pl.program_id(1) == 0))
    def _():
        for c0 in range(0, IN_W, W_IN_CAST_CHUNK):
            w_in_ref[:, c0:c0 + W_IN_CAST_CHUNK] = w_in_f32_ref[:, c0:c0 + W_IN_CAST_CHUNK].astype(BF16)

    for src, dst in zip(cast_in, cast_out):
        dst[...] = src[...].astype(BF16)

    subs = [slice(r0, r0 + IN_SUB) for r0 in range(0, x_ref.shape[0], IN_SUB)]
    hs = [_rmsnorm(x_ref[rows], gmix_ref[...]).astype(BF16) for rows in subs]
    for rows, h in zip(subs, hs):
        cos, sin = cos_ref[rows], sin_ref[rows]
        for c0 in range(0, GATE_W, GATE_CHUNK):
            z = _dot(h, w_in_ref[:, G_OFF + c0:G_OFF + c0 + GATE_CHUNK]) + bgate_ref[:, c0:c0 + GATE_CHUNK]
            g_ref[rows, c0:c0 + GATE_CHUNK] = _sigmoid(z).astype(BF16)
        q_ref[rows] = _rope(_dot(h, w_in_ref[:, 0:Q_W]) * (HEAD_DIM ** -0.5 * LOG2E), cos, sin).astype(BF16)
        p_ref[rows] = _dot(h, w_in_ref[:, P_OFF:P_OFF + POOL_WIDTH])
        kv = _dot(h, w_in_ref[:, K_OFF:K_OFF + 2 * KV_W])
        k_ref[rows] = _rope(kv[:, :KV_W], cos, sin)
        v_ref[rows] = kv[:, KV_W:]


def _const_spec(shape):
    nd = len(shape)
    return pl.BlockSpec(shape, lambda *_: (0,) * nd)


BF16_SUBLANES = 16


def _inproj_call(x, cos, sin, gmix, w_in, bgate, other_weights):
    bsz, seq, _ = x.shape
    tm = TM_IN
    tiles = seq // tm
    n_steps = bsz * tiles
    row = lambda w: pl.BlockSpec((None, tm, w), lambda b, i: (b, i, 0))

    def slab_spec(w):
        n_slabs = n_steps
        while w.shape[0] % (n_slabs * BF16_SUBLANES):
            n_slabs //= 2
        return pl.BlockSpec((w.shape[0] // n_slabs, w.shape[1]),
                            lambda b, i: (jnp.minimum(b * tiles + i, n_slabs - 1), 0))

    slabs = [slab_spec(w) for w in other_weights]
    outs = pl.pallas_call(
        functools.partial(_inproj_kernel, len(other_weights)),
        grid=(bsz, tiles),
        in_specs=[row(D_MODEL), pl.BlockSpec((tm, LANES), lambda b, i: (i, 0)),
                  pl.BlockSpec((tm, LANES), lambda b, i: (i, 0)),
                  _const_spec(gmix.shape), _const_spec(w_in.shape), _const_spec(bgate.shape)] + slabs,
        out_specs=(row(Q_W), row(KV_W), row(KV_W), row(POOL_WIDTH), row(GATE_W), *slabs),
        out_shape=(jax.ShapeDtypeStruct((bsz, seq, Q_W), BF16), jax.ShapeDtypeStruct((bsz, seq, KV_W), F32),
                   jax.ShapeDtypeStruct((bsz, seq, KV_W), F32), jax.ShapeDtypeStruct((bsz, seq, POOL_WIDTH), F32),
                   jax.ShapeDtypeStruct((bsz, seq, GATE_W), BF16),
                   *[jax.ShapeDtypeStruct(w.shape, BF16) for w in other_weights]),
        scratch_shapes=[pltpu.VMEM(w_in.shape, BF16)],
        compiler_params=pltpu.CompilerParams(dimension_semantics=("arbitrary", "arbitrary"),
                                             vmem_limit_bytes=VMEM_LIMIT),
        name="in_proj",
    )(x, cos, sin, gmix, w_in, bgate, *other_weights)
    return outs[:5], outs[5:]


TQ = 512
BAND = WINDOW + CHUNK
KEYS_PAD = BAND + N_META + 16
N_TABS = WINDOW // CHUNK + 1
SUM_ROWS = 16


def _attention_stages(i, q_ref, kc_ref, kp_ref, vc_ref, vp_ref, mk_ref, mv_ref, kaug_ref, qaug_ref, store):
    kvar = _kv_variants(jnp.concatenate([kp_ref[...], kc_ref[...]], axis=0), BF16)
    vvar = _kv_variants(jnp.concatenate([vp_ref[...], vc_ref[...]], axis=0), F32)
    mkvar, mvvar = _kv_variants(mk_ref[...], BF16), _kv_variants(mv_ref[...], F32)
    kpad = jnp.zeros((KEYS_PAD - BAND - N_META, LANES), BF16)
    vpad = jnp.zeros((KEYS_PAD - BAND - N_META, LANES), F32)
    row = lax.broadcasted_iota(jnp.int32, (SUM_ROWS, 2 * KEYS_PAD), 0)
    col = lax.broadcasted_iota(jnp.int32, (SUM_ROWS, 2 * KEYS_PAD), 1)
    ones = jnp.where((row == 0) & (col < KEYS_PAD) | (row == 1) & (col >= KEYS_PAD), 1.0, 0.0).astype(BF16)

    def scores(j, n):
        tsel = jnp.minimum(i * (TQ // CHUNK) + j, N_TABS - 1)
        rows = slice(j * CHUNK, (j + 1) * CHUNK)
        band = slice(j * CHUNK, j * CHUNK + BAND)
        kst = jnp.concatenate([kvar[n][0][band], mkvar[n][0], kpad, kvar[n][1][band], mkvar[n][1], kpad], axis=0)
        vst = jnp.concatenate([vvar[n][0][band], mvvar[n][0], vpad, vvar[n][1][band], mvvar[n][1], vpad], axis=0)
        qg = jnp.concatenate([q_ref[rows, (BLOCKS_PER_KV * n + bi) * LANES:(BLOCKS_PER_KV * n + bi + 1) * LANES]
                              for bi in range(BLOCKS_PER_KV)], axis=0)
        s = lax.dot_general(jnp.concatenate([kst, kaug_ref[tsel]], axis=1),
                            jnp.concatenate([qg, qaug_ref[n]], axis=1),
                            (((1,), (1,)), ((), ())), preferred_element_type=F32)
        vt = jnp.concatenate([vst.T.astype(BF16), ones], axis=0)
        return s, vt

    def finish(j, n, s, vt):
        se, so = s[:KEYS_PAD], s[KEYS_PAD:]
        pe = jnp.exp2(se - jnp.max(se, axis=0, keepdims=True))
        po = jnp.exp2(so - jnp.max(so, axis=0, keepdims=True))
        o = _dot(vt, jnp.concatenate([pe, po], axis=0).astype(BF16))
        o = jnp.concatenate([o[:HEAD_DIM] / o[LANES:LANES + 1], o[HEAD_DIM:LANES] / o[LANES + 1:LANES + 2]], axis=0).T
        for bi in range(BLOCKS_PER_KV):
            store(j, BLOCKS_PER_KV * n + bi, o[bi * CHUNK:(bi + 1) * CHUNK].astype(BF16))

    groups = [(j, n) for j in range(TQ // CHUNK) for n in range(N_KV_HEADS)]
    return groups, scores, finish


def _attn_aug_tables(sinks):
    n_keys = BAND + N_META
    r = jnp.arange(KEYS_PAD)
    lane = jnp.arange(LANES)[None, :]
    kaug = []
    for c in range(N_TABS):
        masked = (r < (WINDOW // CHUNK - c) * CHUNK) | (r > n_keys)
        mask_col = jnp.where(masked, NEG, 0.0)[:, None] * (lane == 0)
        sink_row = (r == n_keys)[:, None]
        even = mask_col + jnp.where(sink_row & ((lane == 1) | (lane == 2)), 1.0, 0.0)
        odd = mask_col + jnp.where(sink_row & ((lane == 3) | (lane == 4)), 1.0, 0.0)
        kaug.append(jnp.concatenate([even, odd], axis=0))
    s2 = sinks.astype(F32) * LOG2E
    hi = s2.astype(BF16).astype(F32)
    lo = s2 - hi
    head = 2 * jnp.arange(Q_BLOCKS)
    cols = [jnp.ones((Q_BLOCKS,), F32), hi[head], lo[head], hi[head + 1], lo[head + 1]]
    qaug = jnp.stack(cols, axis=1)
    qaug = jnp.pad(qaug, ((0, 0), (0, LANES - qaug.shape[1])))
    qaug = jnp.broadcast_to(qaug[:, None, :], (Q_BLOCKS, CHUNK, LANES)).reshape(N_KV_HEADS, BLOCKS_PER_KV * CHUNK, LANES)
    return jnp.stack(kaug).astype(BF16), qaug.astype(BF16)


TM_POST = 512
POST_SUB = 256
FF_CHUNK = 256
FF_SKEW = 3
FF_LANE_BLOCKS = FF_CHUNK // LANES


def _pool_weight_kernel(wgrp_ref, pscale_ref, w_po_ref, out_ref):
    for g in range(N_POOL_GROUPS):
        sl = slice(g * POOL_GROUP_W, (g + 1) * POOL_GROUP_W)
        out_ref[sl, :] = _dot((wgrp_ref[g] * pscale_ref[:, sl]).astype(BF16), w_po_ref[sl, :]).astype(BF16)


def _pool_weight_call(wgrp, pscale, w_po):
    return pl.pallas_call(
        _pool_weight_kernel,
        out_shape=jax.ShapeDtypeStruct((POOL_WIDTH, D_MODEL), BF16),
        name="pool_weights",
    )(wgrp, pscale, w_po)


def _stream_kernel(tiles_per_batch, n_tiles,
                   q_ref, kc_ref, kp_ref, vc_ref, vp_ref, mk_ref, mv_ref, kaug_ref, qaug_ref,
                   x_ref, p_ref, pprev_ref, pmeta_ref, g_ref, upmeta_ref, w_ao_ref, w_pool_ref,
                   w_out_ref, gffn_ref, w_up_ref, convw_ref, convb_ref, w_down_ref, gfin_ref,
                   y_ref, carry_ref, oattn_ref, pext_ref, x1_ref, hn_ref, *uext_refs):
    t = pl.program_id(0)
    i_attn = lax.rem(jnp.minimum(t, n_tiles - 1), tiles_per_batch)
    i = lax.rem(jnp.maximum(t - 1, 0), tiles_per_batch)
    slot_new, slot_old = lax.rem(t, 2), lax.rem(t + 1, 2)
    tm = x_ref.shape[0]
    n_chunks = D_FF // FF_CHUNK

    @pl.when(t == 0)
    def _():
        oattn_ref[...] = jnp.zeros_like(oattn_ref)

    @pl.when(i == 0)
    def _():
        pext_ref[0:N_META] = pmeta_ref[...]
        for c in range(n_chunks):
            for half, off in enumerate((c * FF_CHUNK, D_FF + c * FF_CHUNK)):
                for lb in range(FF_LANE_BLOCKS):
                    uext_refs[c][half * FF_LANE_BLOCKS + lb, POST_SUB:POST_SUB + SUBLANES] = (
                        upmeta_ref[:, off + lb * LANES:off + (lb + 1) * LANES])

    @pl.when(i > 0)
    def _():
        pext_ref[0:N_META] = pprev_ref[...]

    def store_attn(j, blk, value):
        oattn_ref[slot_new, j * CHUNK:(j + 1) * CHUNK, blk * LANES:(blk + 1) * LANES] = value

    groups, scores, finish = _attention_stages(i_attn, q_ref, kc_ref, kp_ref, vc_ref, vp_ref, mk_ref, mv_ref,
                                               kaug_ref, qaug_ref, store_attn)
    pending = [None]

    def attention_group(g):
        current = pending[0]
        if g + 1 < len(groups):
            pending[0] = scores(*groups[g + 1])
        finish(*groups[g], *current)

    pext_ref[N_META:] = p_ref[...]
    subs = [slice(r0, r0 + POST_SUB) for r0 in range(0, tm, POST_SUB)]

    mixes = []
    for rows in subs:
        attn = _dot(oattn_ref[slot_old, rows], w_ao_ref[...])
        ext_rows = slice(rows.start, rows.stop + N_META)
        pm = jnp.concatenate([_pool_group(pext_ref[ext_rows, g * POOL_GROUP_W:(g + 1) * POOL_GROUP_W], g)
                              for g in range(N_POOL_GROUPS)], axis=1)
        pool = _dot(pm.astype(BF16), w_pool_ref[...])
        mix = g_ref[rows, :D_MODEL].astype(F32) * attn + g_ref[rows, D_MODEL:].astype(F32) * pool
        mixes.append(mix.astype(BF16))
    for rows, mix in zip(subs, mixes):
        x1 = x_ref[rows] + _dot(mix, w_out_ref[...])
        x1_ref[rows] = x1
        hn_ref[rows] = _rmsnorm(x1, gffn_ref[...]).astype(BF16)

    def up_project(rows, c):
        for half, off in enumerate((c * FF_CHUNK, D_FF + c * FF_CHUNK)):
            u = _dot(hn_ref[rows], w_up_ref[:, off:off + FF_CHUNK])
            for lb in range(FF_LANE_BLOCKS):
                blk = uext_refs[c].at[half * FF_LANE_BLOCKS + lb]
                blk[0:SUBLANES] = blk[POST_SUB:POST_SUB + SUBLANES]
                blk[SUBLANES:] = u[:, lb * LANES:(lb + 1) * LANES]
            if rows.stop == tm:
                carry_ref[:, off:off + FF_CHUNK] = u[POST_SUB - SUBLANES:]

    def conv(rows, c, half):
        outs = []
        for lb in range(FF_LANE_BLOCKS):
            off = half * D_FF + c * FF_CHUNK + lb * LANES
            w, b = convw_ref[:, off:off + LANES], convb_ref[:, off:off + LANES]
            ext = uext_refs[c].at[half * FF_LANE_BLOCKS + lb]
            cv = b + ext[SUBLANES - 2:SUBLANES - 2 + POST_SUB] * w[0:1]
            cv = cv + ext[SUBLANES - 1:SUBLANES - 1 + POST_SUB] * w[1:2]
            outs.append(cv + ext[SUBLANES:] * w[2:3])
        return jnp.concatenate(outs, axis=1)

    steps = [(rows, c) for rows in subs for c in range(n_chunks)]
    for step in steps[:FF_SKEW]:
        up_project(*step)
    pending[0] = scores(*groups[0])
    acc = None
    for j, (rows, c) in enumerate(steps):
        act = (_silu(conv(rows, c, 0)) * conv(rows, c, 1)).astype(BF16)
        if j + FF_SKEW < len(steps):
            up_project(*steps[j + FF_SKEW])
        d = _dot(act, w_down_ref[c * FF_CHUNK:(c + 1) * FF_CHUNK, :])
        acc = d if c == 0 else acc + d
        if c == n_chunks - 1:
            y_ref[rows] = _rmsnorm(x1_ref[rows] + acc, gfin_ref[...])
        if j < len(groups):
            attention_group(j)
    assert len(groups) <= len(steps)


def _stream_call(q, k, v, mk, mv, kaug, qaug, x, p, p_meta, gates, up_meta8, w_ao, w_pool, w_out, gffn, w_up, convw,
                 convb, w_down, gfin):
    bsz, seq, _ = x.shape
    tm = TM_POST
    assert tm == TQ
    tiles_per_batch = seq // tm
    n_tiles = bsz * tiles_per_batch

    def new_tile(t):
        a = jnp.minimum(t, n_tiles - 1)
        return a // tiles_per_batch, lax.rem(a, tiles_per_batch)

    def old_tile(t):
        c = jnp.maximum(t - 1, 0)
        return c // tiles_per_batch, lax.rem(c, tiles_per_batch)

    new_row = lambda w: pl.BlockSpec((None, tm, w), lambda t: (*new_tile(t), 0))
    old_row = lambda w: pl.BlockSpec((None, tm, w), lambda t: (*old_tile(t), 0))

    def kv_prev_map(t):
        b, i = new_tile(t)
        return b, jnp.maximum(i * (tm // WINDOW) - 1, 0), 0

    def p_prev_map(t):
        b, i = old_tile(t)
        return b, jnp.maximum(i * (tm // N_META) - 1, 0), 0

    kv_prev = pl.BlockSpec((None, WINDOW, KV_W), kv_prev_map)
    consts = (up_meta8, w_ao, w_pool, w_out, gffn, w_up, convw, convb, w_down, gfin)
    return pl.pallas_call(
        functools.partial(_stream_kernel, tiles_per_batch, n_tiles),
        grid=(n_tiles + 1,),
        in_specs=[new_row(Q_W), new_row(KV_W), kv_prev, new_row(KV_W), kv_prev, _const_spec(mk.shape),
                  _const_spec(mv.shape), _const_spec(kaug.shape), _const_spec(qaug.shape),
                  old_row(D_MODEL), old_row(POOL_WIDTH), pl.BlockSpec((None, N_META, POOL_WIDTH), p_prev_map),
                  _const_spec(p_meta.shape), old_row(GATE_W)] + [_const_spec(a.shape) for a in consts],
        out_specs=(old_row(D_MODEL), pl.BlockSpec((None, SUBLANES, 2 * D_FF), lambda t: (old_tile(t)[0], 0, 0))),
        out_shape=(jax.ShapeDtypeStruct((bsz, seq, D_MODEL), F32),
                   jax.ShapeDtypeStruct((bsz, SUBLANES, 2 * D_FF), F32)),
        scratch_shapes=[pltpu.VMEM((2, tm, Q_W), BF16), pltpu.VMEM((N_META + tm, POOL_WIDTH), F32),
                        pltpu.VMEM((tm, D_MODEL), F32), pltpu.VMEM((tm, D_MODEL), BF16)]
                       + [pltpu.VMEM((2 * FF_LANE_BLOCKS, SUBLANES + POST_SUB, LANES), F32)
                          for _ in range(D_FF // FF_CHUNK)],
        compiler_params=pltpu.CompilerParams(dimension_semantics=("arbitrary",), vmem_limit_bytes=VMEM_LIMIT),
        name="attn_mixer_ffn",
    )(q, k, k, v, v, mk, mv, kaug, qaug, x, p, p, p_meta, gates, *consts)


def _rope_tables(pos):
    half = HEAD_DIM // 2
    inv = ROPE_THETA ** (-jnp.arange(half, dtype=F32) / half)
    ang = pos.astype(F32)[:, None] * inv[None, :]
    cos, sin = jnp.cos(ang), jnp.sin(ang)
    return jnp.tile(cos, (1, LANES // half)), jnp.tile(jnp.concatenate([-sin, sin], axis=1), (1, LANES // HEAD_DIM))


def _score_table(sinks, rows_per_block, n_keys, keys_pad, masked_prefix=0):
    col = jnp.arange(keys_pad)
    base = jnp.where((col >= masked_prefix) & (col < n_keys), 0.0, NEG).astype(F32)
    head = (2 * jnp.arange(Q_BLOCKS)[:, None] + jnp.arange(2)[None, :])
    tab = jnp.where(col[None, None, :] == n_keys, sinks.astype(F32)[head][:, :, None], base[None, None, :])
    tab = tab.reshape(N_KV_HEADS, BLOCKS_PER_KV, 1, 2 * keys_pad)
    tab = jnp.broadcast_to(tab, (N_KV_HEADS, BLOCKS_PER_KV, rows_per_block, 2 * keys_pad))
    return tab.reshape(N_KV_HEADS, BLOCKS_PER_KV * rows_per_block, 2 * keys_pad)


def kernel(x_prompt, x_sample, cache_swa_k, cache_swa_v, cache_meta_k, cache_meta_v, state_pool, state_conv,
           meta_tokens, g_norm_mix, w_in, b_gate, sinks, w_attn_o, w_pool_grp, pool_scale, w_pool_o, w_out,
           g_norm_ffn, w_up, conv_w, conv_b, w_down, g_norm_final):
    bsz, seq, _ = x_prompt.shape
    dbsz, t_dec, _ = x_sample.shape
    row2 = lambda a: a.reshape(1, -1)
    gmix, gffn, gfin = row2(g_norm_mix), row2(g_norm_ffn), row2(g_norm_final)
    bgate, pscale, convb = row2(b_gate), row2(pool_scale), row2(conv_b)

    cos_p, sin_p = _rope_tables(N_META + jnp.arange(seq, dtype=jnp.int32))
    (q, k, v, p, gates), (w_ao_b, w_po_b, w_out_b, w_up_b, w_down_b) = _inproj_call(
        x_prompt, cos_p, sin_p, gmix, w_in, bgate, (w_attn_o, w_pool_o, w_out, w_up, w_down))

    pos_side = jnp.concatenate([jnp.arange(N_META, dtype=jnp.int32),
                                jnp.tile(N_META + PAST_LEN + jnp.arange(t_dec, dtype=jnp.int32), dbsz)])
    cos_s, sin_s = _rope_tables(pos_side)
    xs = jnp.concatenate([meta_tokens, x_sample.reshape(dbsz * t_dec, D_MODEL)], axis=0)
    spool16 = jnp.pad(state_pool, ((0, 0), (N_META - POOL_HIST, 0), (0, 0)))
    sconv8 = jnp.pad(state_conv, ((0, 0), (SUBLANES - (CONV_W - 1), 0), (0, 0)))
    tab_meta = _score_table(sinks, N_META, N_META, LANES)
    tab_dec = _score_table(sinks, t_dec, N_META + WINDOW + t_dec, 2 * LANES)
    y_side, k_side, v_side, p_side, up_side = _side_call(
        xs, cache_swa_k.reshape(dbsz, WINDOW, KV_W), cache_swa_v.reshape(dbsz, WINDOW, KV_W),
        cache_meta_k.reshape(dbsz, N_META, KV_W), cache_meta_v.reshape(dbsz, N_META, KV_W), spool16, sconv8,
        cos_s, sin_s, tab_meta, tab_dec, gmix, w_in, bgate, w_ao_b, w_pool_grp, pscale, w_po_b, w_out_b,
        gffn, w_up_b, conv_w, convb, w_down_b, gfin)
    km, vm, p_meta = k_side[:N_META], v_side[:N_META], p_side[:N_META]
    up_meta8 = up_side[N_META - SUBLANES:N_META]

    kaug, qaug = _attn_aug_tables(sinks)
    w_pool_b = _pool_weight_call(w_pool_grp, pscale, w_po_b)
    y_prompt, up_tail = _stream_call(q, k, v, km, vm, kaug, qaug, x_prompt, p, p_meta, gates, up_meta8, w_ao_b,
                                     w_pool_b, w_out_b, gffn, w_up_b, conv_w, convb, w_down_b, gfin)

    kv4 = lambda a, n: a.reshape(a.shape[0], n, N_KV_HEADS, HEAD_DIM)
    dec = lambda a: a[N_META:].reshape(dbsz, t_dec, -1)
    return (
        y_prompt,
        dec(y_side),
        kv4(k[:, seq - WINDOW:], WINDOW),
        kv4(v[:, seq - WINDOW:], WINDOW),
        jnp.broadcast_to(km.reshape(1, N_META, N_KV_HEADS, HEAD_DIM), (bsz, N_META, N_KV_HEADS, HEAD_DIM)),
        jnp.broadcast_to(vm.reshape(1, N_META, N_KV_HEADS, HEAD_DIM), (bsz, N_META, N_KV_HEADS, HEAD_DIM)),
        p[:, seq - POOL_HIST:],
        up_tail[:, SUBLANES - (CONV_W - 1):],
        kv4(dec(k_side), t_dec),
        kv4(dec(v_side), t_dec),
        dec(p_side)[:, t_dec - POOL_HIST:],
        dec(up_side)[:, t_dec - (CONV_W - 1):],
    )
```

```python
import functools

import jax
import jax.numpy as jnp
from jax import lax
from jax.experimental import pallas as pl
from jax.experimental.pallas import tpu as pltpu

D_MODEL = 1024
N_META = 16
CHUNK = 64
HEAD_DIM = 64
N_Q_HEADS = 16
N_KV_HEADS = 2
WINDOW = 128
ROPE_THETA = 10000.0
POOL_WINDOWS = (2, 4, 8, 16)
N_POOL_GROUPS = 4
POOL_WIDTH = D_MODEL // 2
POOL_GROUP_W = POOL_WIDTH // N_POOL_GROUPS
POOL_HIST = max(POOL_WINDOWS) - 1
Q_W = N_Q_HEADS * HEAD_DIM
KV_W = N_KV_HEADS * HEAD_DIM
GATE_W = 2 * D_MODEL
IN_W = Q_W + 2 * KV_W + POOL_WIDTH + GATE_W
D_FF = ((8 * D_MODEL // 3) + 127) // 128 * 128
CONV_W = 3
RMS_EPS = 1e-6
PAST_LEN = 1024

LANES = 128
SUBLANES = 8
Q_BLOCKS = Q_W // LANES
BLOCKS_PER_KV = Q_BLOCKS // N_KV_HEADS
NEG = -1e30
LOG2E = 1.4426950408889634

K_OFF = Q_W
V_OFF = Q_W + KV_W
P_OFF = Q_W + 2 * KV_W
G_OFF = P_OFF + POOL_WIDTH

F32 = jnp.float32
BF16 = jnp.bfloat16

VMEM_LIMIT = 60 * 1024 * 1024


def _rmsnorm(x, g):
    ms = jnp.mean(x * x, axis=-1, keepdims=True)
    return (x * lax.rsqrt(ms + RMS_EPS)) * g


def _dot(a, b):
    return jnp.dot(a, b, preferred_element_type=F32)


def _rope(x, cos, sin_signed):
    lane = lax.broadcasted_iota(jnp.int32, (x.shape[0], LANES), 1)
    first_half = (lane & (HEAD_DIM // 2)) == 0
    outs = []
    for b in range(x.shape[1] // LANES):
        xb = x[:, b * LANES:(b + 1) * LANES]
        partner = jnp.where(first_half, pltpu.roll(xb, LANES - HEAD_DIM // 2, 1), pltpu.roll(xb, HEAD_DIM // 2, 1))
        outs.append(xb * cos + partner * sin_signed)
    return outs[0] if len(outs) == 1 else jnp.concatenate(outs, axis=1)


def _kv_variants(kv, dtype=BF16):
    lane = lax.broadcasted_iota(jnp.int32, kv.shape, 1)
    low = lane < HEAD_DIM
    swapped = pltpu.roll(kv, HEAD_DIM, 1)
    zero = jnp.zeros_like(kv)
    return (
        (jnp.where(low, kv, zero).astype(dtype), jnp.where(low, zero, swapped).astype(dtype)),
        (jnp.where(low, swapped, zero).astype(dtype), jnp.where(low, zero, kv).astype(dtype)),
    )


def _attend_group(qg, kst, vst, tab):
    s = lax.dot_general(qg, kst, (((1,), (1,)), ((), ())), preferred_element_type=F32) + tab
    npad = kst.shape[0] // 2
    se, so = s[:, :npad], s[:, npad:]
    pe = jnp.exp(se - jnp.max(se, axis=1, keepdims=True))
    po = jnp.exp(so - jnp.max(so, axis=1, keepdims=True))
    le = jnp.sum(pe, axis=1, keepdims=True)
    lo = jnp.sum(po, axis=1, keepdims=True)
    p = jnp.concatenate([pe, po], axis=1).astype(BF16)
    o = _dot(p, vst)
    lane = lax.broadcasted_iota(jnp.int32, o.shape, 1)
    return o * jnp.where(lane < HEAD_DIM, 1.0 / le, 1.0 / lo)


def _pool_group(ext, g, cnt=None):
    w = POOL_WINDOWS[g]
    s = ext
    k = 1
    while k < w:
        s = s + pltpu.roll(s, k, 0)
        k *= 2
    cur = ext[N_META:]
    return s[N_META:] / (float(w) if cnt is None else cnt) - cur


def _pool_project(pm, wgrp_ref, pscale):
    outs = []
    for g in range(N_POOL_GROUPS):
        sl = slice(g * POOL_GROUP_W, (g + 1) * POOL_GROUP_W)
        outs.append(_dot(pm[:, sl].astype(BF16), wgrp_ref[g].astype(BF16)))
    return (jnp.concatenate(outs, axis=1) * pscale).astype(BF16)


def _conv3(u, prev8, w, b):
    t = u.shape[0]
    ext = jnp.concatenate([prev8, u], axis=0)
    u1 = pltpu.roll(ext, 1, 0)[SUBLANES:SUBLANES + t]
    u2 = pltpu.roll(ext, 2, 0)[SUBLANES:SUBLANES + t]
    c = b + u2 * w[0:1]
    c = c + u1 * w[1:2]
    return c + u * w[2:3]


def _sigmoid(x):
    return 0.5 * jnp.tanh(0.5 * x) + 0.5


def _silu(x):
    half = 0.5 * x
    return half + half * jnp.tanh(half)


N_SIDE = N_META + 8 * 16
SIDE_FF_CHUNK = 256


def _side_kernel(xs_ref, ck_ref, cv_ref, cmk_ref, cmv_ref, spool_ref, sconv_ref, cos_ref, sin_ref, tabm_ref,
                 tabs_ref, gmix_ref, w_in_ref, bgate_ref, w_ao_ref, wgrp_ref, pscale_ref, w_po_ref, w_out_ref,
                 gffn_ref, w_up_ref, convw_ref, convb_ref, w_down_ref, gfin_ref,
                 y_ref, k_ref, v_ref, p_ref, up_ref):
    dec_b = ck_ref.shape[0]
    t_dec = (N_SIDE - N_META) // dec_b
    x = xs_ref[...]
    h = _rmsnorm(x, gmix_ref[...]).astype(BF16)
    cos, sin = cos_ref[...], sin_ref[...]
    w_in = lambda c0, c1: w_in_ref[:, c0:c1].astype(BF16)
    q = _rope(_dot(h, w_in(0, Q_W)) * (HEAD_DIM ** -0.5), cos, sin).astype(BF16)
    k = _rope(_dot(h, w_in(K_OFF, K_OFF + KV_W)), cos, sin)
    v = _dot(h, w_in(V_OFF, V_OFF + KV_W))
    p = _dot(h, w_in(P_OFF, P_OFF + POOL_WIDTH))
    gates = _sigmoid(jnp.concatenate([_dot(h, w_in(G_OFF + c0, G_OFF + c0 + GATE_CHUNK))
                                      for c0 in range(0, GATE_W, GATE_CHUNK)], axis=1) + bgate_ref[...])
    k_ref[...] = k
    v_ref[...] = v
    p_ref[...] = p

    def attend_rows(r0, nrows, kx, vx, tab_ref):
        kvar, vvar = _kv_variants(kx), _kv_variants(vx)
        blocks = [None] * Q_BLOCKS
        for n in range(N_KV_HEADS):
            qg = jnp.concatenate([q[r0:r0 + nrows, (BLOCKS_PER_KV * n + bi) * LANES:(BLOCKS_PER_KV * n + bi + 1) * LANES]
                                  for bi in range(BLOCKS_PER_KV)], axis=0)
            kst = jnp.concatenate(kvar[n], axis=0)
            vst = jnp.concatenate(vvar[n], axis=0)
            o = _attend_group(qg, kst, vst, tab_ref[n])
            for bi in range(BLOCKS_PER_KV):
                blocks[BLOCKS_PER_KV * n + bi] = o[bi * nrows:(bi + 1) * nrows]
        return jnp.concatenate(blocks, axis=1)

    zpad = jnp.zeros((LANES - N_META, LANES), F32)
    o_rows = [attend_rows(0, N_META, jnp.concatenate([k[:N_META], zpad], axis=0),
                          jnp.concatenate([v[:N_META], zpad], axis=0), tabm_ref)]
    n_keys = N_META + WINDOW + t_dec
    zpad = jnp.zeros((2 * LANES - n_keys, LANES), F32)
    for b in range(dec_b):
        r0 = N_META + b * t_dec
        kx = jnp.concatenate([cmk_ref[b], ck_ref[b], k[r0:r0 + t_dec], zpad], axis=0)
        vx = jnp.concatenate([cmv_ref[b], cv_ref[b], v[r0:r0 + t_dec], zpad], axis=0)
        o_rows.append(attend_rows(r0, t_dec, kx, vx, tabs_ref))
    o_attn = jnp.concatenate(o_rows, axis=0).astype(BF16)

    row = lax.broadcasted_iota(jnp.int32, (N_META, LANES), 0)
    pm_rows = []
    for seg in range(1 + dec_b):
        if seg == 0:
            ext = jnp.concatenate([jnp.zeros((N_META, POOL_WIDTH), F32), p[:N_META]], axis=0)
        else:
            r0 = N_META + (seg - 1) * t_dec
            ext = jnp.concatenate([spool_ref[seg - 1], p[r0:r0 + t_dec]], axis=0)
        groups = []
        for g in range(N_POOL_GROUPS):
            cnt = jnp.minimum(POOL_WINDOWS[g], row + 1).astype(F32) if seg == 0 else None
            groups.append(_pool_group(ext[:, g * POOL_GROUP_W:(g + 1) * POOL_GROUP_W], g, cnt))
        pm_rows.append(jnp.concatenate(groups, axis=1))
    pm = jnp.concatenate(pm_rows, axis=0)

    pool = _pool_project(pm, wgrp_ref, pscale_ref[...])
    mix = gates[:, :D_MODEL] * _dot(o_attn, w_ao_ref[...]) + gates[:, D_MODEL:] * _dot(pool, w_po_ref[...])
    x1 = x + _dot(mix.astype(BF16), w_out_ref[...])

    hn = _rmsnorm(x1, gffn_ref[...]).astype(BF16)
    acc = jnp.zeros((N_SIDE, D_MODEL), F32)
    zprev = jnp.zeros((SUBLANES, SIDE_FF_CHUNK), F32)
    for c0 in range(0, D_FF, SIDE_FF_CHUNK):
        halves = []
        for off in (c0, D_FF + c0):
            cs = slice(off, off + SIDE_FF_CHUNK)
            u = _dot(hn, w_up_ref[:, cs])
            up_ref[:, cs] = u
            w, bias = convw_ref[:, cs], convb_ref[:, cs]
            segs = [_conv3(u[:N_META], zprev, w, bias)]
            for b in range(dec_b):
                r0 = N_META + b * t_dec
                segs.append(_conv3(u[r0:r0 + t_dec], sconv_ref[b, :, cs], w, bias))
            halves.append(jnp.concatenate(segs, axis=0))
        act = (_silu(halves[0]) * halves[1]).astype(BF16)
        acc = acc + _dot(act, w_down_ref[c0:c0 + SIDE_FF_CHUNK, :])
    y_ref[...] = _rmsnorm(x1 + acc, gfin_ref[...])


def _side_call(*args):
    out_shape = (
        jax.ShapeDtypeStruct((N_SIDE, D_MODEL), F32),
        jax.ShapeDtypeStruct((N_SIDE, KV_W), F32),
        jax.ShapeDtypeStruct((N_SIDE, KV_W), F32),
        jax.ShapeDtypeStruct((N_SIDE, POOL_WIDTH), F32),
        jax.ShapeDtypeStruct((N_SIDE, 2 * D_FF), F32),
    )
    return pl.pallas_call(
        _side_kernel,
        out_shape=out_shape,
        compiler_params=pltpu.CompilerParams(vmem_limit_bytes=VMEM_LIMIT),
        name="side_rows",
    )(*args)


TM_IN = 1024
IN_SUB = 256
GATE_CHUNK = 512


W_IN_CAST_CHUNK = 768


def _inproj_kernel(n_cast, x_ref, cos_ref, sin_ref, gmix_ref, w_in_f32_ref, bgate_ref, *refs):
    cast_in, (q_ref, k_ref, v_ref, p_ref, g_ref) = refs[:n_cast], refs[n_cast:n_cast + 5]
    cast_out, w_in_ref = refs[n_cast + 5:2 * n_cast + 5], refs[2 * n_cast + 5]

    @pl.when((pl.program_id(0) == 0) & (pl.program_id(1) == 0))
    def _():
        for c0 in range(0, IN_W, W_IN_CAST_CHUNK):
            w_in_ref[:, c0:c0 + W_IN_CAST_CHUNK] = w_in_f32_ref[:, c0:c0 + W_IN_CAST_CHUNK].astype(BF16)

    for src, dst in zip(cast_in, cast_out):
        dst[...] = src[...].astype(BF16)

    subs = [slice(r0, r0 + IN_SUB) for r0 in range(0, x_ref.shape[0], IN_SUB)]
    hs = [_rmsnorm(x_ref[rows], gmix_ref[...]).astype(BF16) for rows in subs]
    for rows, h in zip(subs, hs):
        cos, sin = cos_ref[rows], sin_ref[rows]
        for c0 in range(0, GATE_W, GATE_CHUNK):
            z = _dot(h, w_in_ref[:, G_OFF + c0:G_OFF + c0 + GATE_CHUNK]) + bgate_ref[:, c0:c0 + GATE_CHUNK]
            g_ref[rows, c0:c0 + GATE_CHUNK] = _sigmoid(z).astype(BF16)
        q_ref[rows] = _rope(_dot(h, w_in_ref[:, 0:Q_W]) * (HEAD_DIM ** -0.5 * LOG2E), cos, sin).astype(BF16)
        p_ref[rows] = _dot(h, w_in_ref[:, P_OFF:P_OFF + POOL_WIDTH])
        kv = _dot(h, w_in_ref[:, K_OFF:K_OFF + 2 * KV_W])
        k_ref[rows] = _rope(kv[:, :KV_W], cos, sin)
        v_ref[rows] = kv[:, KV_W:]


def _const_spec(shape):
    nd = len(shape)
    return pl.BlockSpec(shape, lambda *_: (0,) * nd)


BF16_SUBLANES = 16


def _inproj_call(x, cos, sin, gmix, w_in, bgate, other_weights):
    bsz, seq, _ = x.shape
    tm = TM_IN
    tiles = seq // tm
    n_steps = bsz * tiles
    row = lambda w: pl.BlockSpec((None, tm, w), lambda b, i: (b, i, 0))

    def slab_spec(w):
        n_slabs = n_steps
        while w.shape[0] % (n_slabs * BF16_SUBLANES):
            n_slabs //= 2
        return pl.BlockSpec((w.shape[0] // n_slabs, w.shape[1]),
                            lambda b, i: (jnp.minimum(b * tiles + i, n_slabs - 1), 0))

    slabs = [slab_spec(w) for w in other_weights]
    outs = pl.pallas_call(
        functools.partial(_inproj_kernel, len(other_weights)),
        grid=(bsz, tiles),
        in_specs=[row(D_MODEL), pl.BlockSpec((tm, LANES), lambda b, i: (i, 0)),
                  pl.BlockSpec((tm, LANES), lambda b, i: (i, 0)),
                  _const_spec(gmix.shape), _const_spec(w_in.shape), _const_spec(bgate.shape)] + slabs,
        out_specs=(row(Q_W), row(KV_W), row(KV_W), row(POOL_WIDTH), row(GATE_W), *slabs),
        out_shape=(jax.ShapeDtypeStruct((bsz, seq, Q_W), BF16), jax.ShapeDtypeStruct((bsz, seq, KV_W), F32),
                   jax.ShapeDtypeStruct((bsz, seq, KV_W), F32), jax.ShapeDtypeStruct((bsz, seq, POOL_WIDTH), F32),
                   jax.ShapeDtypeStruct((bsz, seq, GATE_W), BF16),
                   *[jax.ShapeDtypeStruct(w.shape, BF16) for w in other_weights]),
        scratch_shapes=[pltpu.VMEM(w_in.shape, BF16)],
        compiler_params=pltpu.CompilerParams(dimension_semantics=("arbitrary", "arbitrary"),
                                             vmem_limit_bytes=VMEM_LIMIT),
        name="in_proj",
    )(x, cos, sin, gmix, w_in, bgate, *other_weights)
    return outs[:5], outs[5:]


TQ = 512
BAND = WINDOW + CHUNK
KEYS_PAD = BAND + N_META + 16
N_TABS = WINDOW // CHUNK + 1
SUM_ROWS = 16


def _attention_stages(i, q_ref, kc_ref, kp_ref, vc_ref, vp_ref, mk_ref, mv_ref, kaug_ref, qaug_ref, store):
    kvar = _kv_variants(jnp.concatenate([kp_ref[...], kc_ref[...]], axis=0), BF16)
    vvar = _kv_variants(jnp.concatenate([vp_ref[...], vc_ref[...]], axis=0), F32)
    mkvar, mvvar = _kv_variants(mk_ref[...], BF16), _kv_variants(mv_ref[...], F32)
    kpad = jnp.zeros((KEYS_PAD - BAND - N_META, LANES), BF16)
    vpad = jnp.zeros((KEYS_PAD - BAND - N_META, LANES), F32)
    row = lax.broadcasted_iota(jnp.int32, (SUM_ROWS, 2 * KEYS_PAD), 0)
    col = lax.broadcasted_iota(jnp.int32, (SUM_ROWS, 2 * KEYS_PAD), 1)
    ones = jnp.where((row == 0) & (col < KEYS_PAD) | (row == 1) & (col >= KEYS_PAD), 1.0, 0.0).astype(BF16)

    def scores(j, n):
        tsel = jnp.minimum(i * (TQ // CHUNK) + j, N_TABS - 1)
        rows = slice(j * CHUNK, (j + 1) * CHUNK)
        band = slice(j * CHUNK, j * CHUNK + BAND)
        kst = jnp.concatenate([kvar[n][0][band], mkvar[n][0], kpad, kvar[n][1][band], mkvar[n][1], kpad], axis=0)
        vst = jnp.concatenate([vvar[n][0][band], mvvar[n][0], vpad, vvar[n][1][band], mvvar[n][1], vpad], axis=0)
        qg = jnp.concatenate([q_ref[rows, (BLOCKS_PER_KV * n + bi) * LANES:(BLOCKS_PER_KV * n + bi + 1) * LANES]
                              for bi in range(BLOCKS_PER_KV)], axis=0)
        s = lax.dot_general(jnp.concatenate([kst, kaug_ref[tsel]], axis=1),
                            jnp.concatenate([qg, qaug_ref[n]], axis=1),
                            (((1,), (1,)), ((), ())), preferred_element_type=F32)
        vt = jnp.concatenate([vst.T.astype(BF16), ones], axis=0)
        return s, vt

    def finish(j, n, s, vt):
        se, so = s[:KEYS_PAD], s[KEYS_PAD:]
        pe = jnp.exp2(se - jnp.max(se, axis=0, keepdims=True))
        po = jnp.exp2(so - jnp.max(so, axis=0, keepdims=True))
        o = _dot(vt, jnp.concatenate([pe, po], axis=0).astype(BF16))
        o = jnp.concatenate([o[:HEAD_DIM] / o[LANES:LANES + 1], o[HEAD_DIM:LANES] / o[LANES + 1:LANES + 2]], axis=0).T
        for bi in range(BLOCKS_PER_KV):
            store(j, BLOCKS_PER_KV * n + bi, o[bi * CHUNK:(bi + 1) * CHUNK].astype(BF16))

    groups = [(j, n) for j in range(TQ // CHUNK) for n in range(N_KV_HEADS)]
    return groups, scores, finish


def _attn_aug_tables(sinks):
    n_keys = BAND + N_META
    r = jnp.arange(KEYS_PAD)
    lane = jnp.arange(LANES)[None, :]
    kaug = []
    for c in range(N_TABS):
        masked = (r < (WINDOW // CHUNK - c) * CHUNK) | (r > n_keys)
        mask_col = jnp.where(masked, NEG, 0.0)[:, None] * (lane == 0)
        sink_row = (r == n_keys)[:, None]
        even = mask_col + jnp.where(sink_row & ((lane == 1) | (lane == 2)), 1.0, 0.0)
        odd = mask_col + jnp.where(sink_row & ((lane == 3) | (lane == 4)), 1.0, 0.0)
        kaug.append(jnp.concatenate([even, odd], axis=0))
    s2 = sinks.astype(F32) * LOG2E
    hi = s2.astype(BF16).astype(F32)
    lo = s2 - hi
    head = 2 * jnp.arange(Q_BLOCKS)
    cols = [jnp.ones((Q_BLOCKS,), F32), hi[head], lo[head], hi[head + 1], lo[head + 1]]
    qaug = jnp.stack(cols, axis=1)
    qaug = jnp.pad(qaug, ((0, 0), (0, LANES - qaug.shape[1])))
    qaug = jnp.broadcast_to(qaug[:, None, :], (Q_BLOCKS, CHUNK, LANES)).reshape(N_KV_HEADS, BLOCKS_PER_KV * CHUNK, LANES)
    return jnp.stack(kaug).astype(BF16), qaug.astype(BF16)


TM_POST = 512
POST_SUB = 256
FF_CHUNK = 256
FF_SKEW = 3
FF_LANE_BLOCKS = FF_CHUNK // LANES
DOWN_GROUP = 4


def _pool_weight_kernel(wgrp_ref, pscale_ref, w_po_ref, out_ref):
    for g in range(N_POOL_GROUPS):
        sl = slice(g * POOL_GROUP_W, (g + 1) * POOL_GROUP_W)
        out_ref[sl, :] = _dot((wgrp_ref[g] * pscale_ref[:, sl]).astype(BF16), w_po_ref[sl, :]).astype(BF16)


def _pool_weight_call(wgrp, pscale, w_po):
    return pl.pallas_call(
        _pool_weight_kernel,
        out_shape=jax.ShapeDtypeStruct((POOL_WIDTH, D_MODEL), BF16),
        name="pool_weights",
    )(wgrp, pscale, w_po)


def _stream_kernel(tiles_per_batch, n_tiles,
                   q_ref, kc_ref, kp_ref, vc_ref, vp_ref, mk_ref, mv_ref, kaug_ref, qaug_ref,
                   x_ref, p_ref, pprev_ref, pmeta_ref, g_ref, upmeta_ref, w_ao_ref, w_pool_ref,
                   w_out_ref, gffn_ref, w_up_ref, convw_ref, convb_ref, w_down_ref, gfin_ref,
                   y_ref, carry_ref, oattn_ref, pext_ref, x1_ref, hn_ref, act_ref, *uext_refs):
    t = pl.program_id(0)
    i_attn = lax.rem(jnp.minimum(t, n_tiles - 1), tiles_per_batch)
    i = lax.rem(jnp.maximum(t - 1, 0), tiles_per_batch)
    slot_new, slot_old = lax.rem(t, 2), lax.rem(t + 1, 2)
    tm = x_ref.shape[0]
    n_chunks = D_FF // FF_CHUNK

    @pl.when(t == 0)
    def _():
        oattn_ref[...] = jnp.zeros_like(oattn_ref)

    @pl.when(i == 0)
    def _():
        pext_ref[0:N_META] = pmeta_ref[...]
        for c in range(n_chunks):
            for half, off in enumerate((c * FF_CHUNK, D_FF + c * FF_CHUNK)):
                for lb in range(FF_LANE_BLOCKS):
                    uext_refs[c][half * FF_LANE_BLOCKS + lb, POST_SUB:POST_SUB + SUBLANES] = (
                        upmeta_ref[:, off + lb * LANES:off + (lb + 1) * LANES])

    @pl.when(i > 0)
    def _():
        pext_ref[0:N_META] = pprev_ref[...]

    def store_attn(j, blk, value):
        oattn_ref[slot_new, j * CHUNK:(j + 1) * CHUNK, blk * LANES:(blk + 1) * LANES] = value

    groups, scores, finish = _attention_stages(i_attn, q_ref, kc_ref, kp_ref, vc_ref, vp_ref, mk_ref, mv_ref,
                                               kaug_ref, qaug_ref, store_attn)
    pending = [None]

    def attention_group(g):
        current = pending[0]
        if g + 1 < len(groups):
            pending[0] = scores(*groups[g + 1])
        finish(*groups[g], *current)

    pext_ref[N_META:] = p_ref[...]
    subs = [slice(r0, r0 + POST_SUB) for r0 in range(0, tm, POST_SUB)]

    mixes = []
    for rows in subs:
        attn = _dot(oattn_ref[slot_old, rows], w_ao_ref[...])
        ext_rows = slice(rows.start, rows.stop + N_META)
        pm = jnp.concatenate([_pool_group(pext_ref[ext_rows, g * POOL_GROUP_W:(g + 1) * POOL_GROUP_W], g)
                              for g in range(N_POOL_GROUPS)], axis=1)
        pool = _dot(pm.astype(BF16), w_pool_ref[...])
        mix = g_ref[rows, :D_MODEL].astype(F32) * attn + g_ref[rows, D_MODEL:].astype(F32) * pool
        mixes.append(mix.astype(BF16))
    for rows, mix in zip(subs, mixes):
        x1 = x_ref[rows] + _dot(mix, w_out_ref[...])
        x1_ref[rows] = x1
        hn_ref[rows] = _rmsnorm(x1, gffn_ref[...]).astype(BF16)

    def up_project(rows, c):
        for half, off in enumerate((c * FF_CHUNK, D_FF + c * FF_CHUNK)):
            u = _dot(hn_ref[rows], w_up_ref[:, off:off + FF_CHUNK])
            for lb in range(FF_LANE_BLOCKS):
                blk = uext_refs[c].at[half * FF_LANE_BLOCKS + lb]
                blk[0:SUBLANES] = blk[POST_SUB:POST_SUB + SUBLANES]
                blk[SUBLANES:] = u[:, lb * LANES:(lb + 1) * LANES]
            if rows.stop == tm:
                carry_ref[:, off:off + FF_CHUNK] = u[POST_SUB - SUBLANES:]

    def conv(rows, c, half):
        outs = []
        for lb in range(FF_LANE_BLOCKS):
            off = half * D_FF + c * FF_CHUNK + lb * LANES
            w, b = convw_ref[:, off:off + LANES], convb_ref[:, off:off + LANES]
            ext = uext_refs[c].at[half * FF_LANE_BLOCKS + lb]
            cv = b + ext[SUBLANES - 2:SUBLANES - 2 + POST_SUB] * w[0:1]
            cv = cv + ext[SUBLANES - 1:SUBLANES - 1 + POST_SUB] * w[1:2]
            outs.append(cv + ext[SUBLANES:] * w[2:3])
        return jnp.concatenate(outs, axis=1)

    steps = [(rows, c) for rows in subs for c in range(n_chunks)]
    for step in steps[:FF_SKEW]:
        up_project(*step)
    pending[0] = scores(*groups[0])
    acc = None
    for j, (rows, c) in enumerate(steps):
        act_ref[rows, c * FF_CHUNK:(c + 1) * FF_CHUNK] = (_silu(conv(rows, c, 0)) * conv(rows, c, 1)).astype(BF16)
        if j + FF_SKEW < len(steps):
            up_project(*steps[j + FF_SKEW])
        if (c + 1) % DOWN_GROUP == 0 or c == n_chunks - 1:
            ks = slice((c // DOWN_GROUP) * DOWN_GROUP * FF_CHUNK, (c + 1) * FF_CHUNK)
            d = _dot(act_ref[rows, ks], w_down_ref[ks, :])
            acc = d if c < DOWN_GROUP else acc + d
        if c == n_chunks - 1:
            y_ref[rows] = _rmsnorm(x1_ref[rows] + acc, gfin_ref[...])
        if j < len(groups):
            attention_group(j)
    assert len(groups) <= len(steps)


def _stream_call(q, k, v, mk, mv, kaug, qaug, x, p, p_meta, gates, up_meta8, w_ao, w_pool, w_out, gffn, w_up, convw,
                 convb, w_down, gfin):
    bsz, seq, _ = x.shape
    tm = TM_POST
    assert tm == TQ
    tiles_per_batch = seq // tm
    n_tiles = bsz * tiles_per_batch

    def new_tile(t):
        a = jnp.minimum(t, n_tiles - 1)
        return a // tiles_per_batch, lax.rem(a, tiles_per_batch)

    def old_tile(t):
        c = jnp.maximum(t - 1, 0)
        return c // tiles_per_batch, lax.rem(c, tiles_per_batch)

    new_row = lambda w: pl.BlockSpec((None, tm, w), lambda t: (*new_tile(t), 0))
    old_row = lambda w: pl.BlockSpec((None, tm, w), lambda t: (*old_tile(t), 0))

    def kv_prev_map(t):
        b, i = new_tile(t)
        return b, jnp.maximum(i * (tm // WINDOW) - 1, 0), 0

    def p_prev_map(t):
        b, i = old_tile(t)
        return b, jnp.maximum(i * (tm // N_META) - 1, 0), 0

    kv_prev = pl.BlockSpec((None, WINDOW, KV_W), kv_prev_map)
    consts = (up_meta8, w_ao, w_pool, w_out, gffn, w_up, convw, convb, w_down, gfin)
    return pl.pallas_call(
        functools.partial(_stream_kernel, tiles_per_batch, n_tiles),
        grid=(n_tiles + 1,),
        in_specs=[new_row(Q_W), new_row(KV_W), kv_prev, new_row(KV_W), kv_prev, _const_spec(mk.shape),
                  _const_spec(mv.shape), _const_spec(kaug.shape), _const_spec(qaug.shape),
                  old_row(D_MODEL), old_row(POOL_WIDTH), pl.BlockSpec((None, N_META, POOL_WIDTH), p_prev_map),
                  _const_spec(p_meta.shape), old_row(GATE_W)] + [_const_spec(a.shape) for a in consts],
        out_specs=(old_row(D_MODEL), pl.BlockSpec((None, SUBLANES, 2 * D_FF), lambda t: (old_tile(t)[0], 0, 0))),
        out_shape=(jax.ShapeDtypeStruct((bsz, seq, D_MODEL), F32),
                   jax.ShapeDtypeStruct((bsz, SUBLANES, 2 * D_FF), F32)),
        scratch_shapes=[pltpu.VMEM((2, tm, Q_W), BF16), pltpu.VMEM((N_META + tm, POOL_WIDTH), F32),
                        pltpu.VMEM((tm, D_MODEL), F32), pltpu.VMEM((tm, D_MODEL), BF16),
                        pltpu.VMEM((tm, D_FF), BF16)]
                       + [pltpu.VMEM((2 * FF_LANE_BLOCKS, SUBLANES + POST_SUB, LANES), F32)
                          for _ in range(D_FF // FF_CHUNK)],
        compiler_params=pltpu.CompilerParams(dimension_semantics=("arbitrary",), vmem_limit_bytes=VMEM_LIMIT),
        name="attn_mixer_ffn",
    )(q, k, k, v, v, mk, mv, kaug, qaug, x, p, p, p_meta, gates, *consts)


def _rope_tables(pos):
    half = HEAD_DIM // 2
    inv = ROPE_THETA ** (-jnp.arange(half, dtype=F32) / half)
    ang = pos.astype(F32)[:, None] * inv[None, :]
    cos, sin = jnp.cos(ang), jnp.sin(ang)
    return jnp.tile(cos, (1, LANES // half)), jnp.tile(jnp.concatenate([-sin, sin], axis=1), (1, LANES // HEAD_DIM))


def _score_table(sinks, rows_per_block, n_keys, keys_pad, masked_prefix=0):
    col = jnp.arange(keys_pad)
    base = jnp.where((col >= masked_prefix) & (col < n_keys), 0.0, NEG).astype(F32)
    head = (2 * jnp.arange(Q_BLOCKS)[:, None] + jnp.arange(2)[None, :])
    tab = jnp.where(col[None, None, :] == n_keys, sinks.astype(F32)[head][:, :, None], base[None, None, :])
    tab = tab.reshape(N_KV_HEADS, BLOCKS_PER_KV, 1, 2 * keys_pad)
    tab = jnp.broadcast_to(tab, (N_KV_HEADS, BLOCKS_PER_KV, rows_per_block, 2 * keys_pad))
    return tab.reshape(N_KV_HEADS, BLOCKS_PER_KV * rows_per_block, 2 * keys_pad)


def kernel(x_prompt, x_sample, cache_swa_k, cache_swa_v, cache_meta_k, cache_meta_v, state_pool, state_conv,
           meta_tokens, g_norm_mix, w_in, b_gate, sinks, w_attn_o, w_pool_grp, pool_scale, w_pool_o, w_out,
           g_norm_ffn, w_up, conv_w, conv_b, w_down, g_norm_final):
    bsz, seq, _ = x_prompt.shape
    dbsz, t_dec, _ = x_sample.shape
    row2 = lambda a: a.reshape(1, -1)
    gmix, gffn, gfin = row2(g_norm_mix), row2(g_norm_ffn), row2(g_norm_final)
    bgate, pscale, convb = row2(b_gate), row2(pool_scale), row2(conv_b)

    cos_p, sin_p = _rope_tables(N_META + jnp.arange(seq, dtype=jnp.int32))
    (q, k, v, p, gates), (w_ao_b, w_po_b, w_out_b, w_up_b, w_down_b) = _inproj_call(
        x_prompt, cos_p, sin_p, gmix, w_in, bgate, (w_attn_o, w_pool_o, w_out, w_up, w_down))

    pos_side = jnp.concatenate([jnp.arange(N_META, dtype=jnp.int32),
                                jnp.tile(N_META + PAST_LEN + jnp.arange(t_dec, dtype=jnp.int32), dbsz)])
    cos_s, sin_s = _rope_tables(pos_side)
    xs = jnp.concatenate([meta_tokens, x_sample.reshape(dbsz * t_dec, D_MODEL)], axis=0)
    spool16 = jnp.pad(state_pool, ((0, 0), (N_META - POOL_HIST, 0), (0, 0)))
    sconv8 = jnp.pad(state_conv, ((0, 0), (SUBLANES - (CONV_W - 1), 0), (0, 0)))
    tab_meta = _score_table(sinks, N_META, N_META, LANES)
    tab_dec = _score_table(sinks, t_dec, N_META + WINDOW + t_dec, 2 * LANES)
    y_side, k_side, v_side, p_side, up_side = _side_call(
        xs, cache_swa_k.reshape(dbsz, WINDOW, KV_W), cache_swa_v.reshape(dbsz, WINDOW, KV_W),
        cache_meta_k.reshape(dbsz, N_META, KV_W), cache_meta_v.reshape(dbsz, N_META, KV_W), spool16, sconv8,
        cos_s, sin_s, tab_meta, tab_dec, gmix, w_in, bgate, w_ao_b, w_pool_grp, pscale, w_po_b, w_out_b,
        gffn, w_up_b, conv_w, convb, w_down_b, gfin)
    km, vm, p_meta = k_side[:N_META], v_side[:N_META], p_side[:N_META]
    up_meta8 = up_side[N_META - SUBLANES:N_META]

    kaug, qaug = _attn_aug_tables(sinks)
    w_pool_b = _pool_weight_call(w_pool_grp, pscale, w_po_b)
    y_prompt, up_tail = _stream_call(q, k, v, km, vm, kaug, qaug, x_prompt, p, p_meta, gates, up_meta8, w_ao_b,
                                     w_pool_b, w_out_b, gffn, w_up_b, conv_w, convb, w_down_b, gfin)

    kv4 = lambda a, n: a.reshape(a.shape[0], n, N_KV_HEADS, HEAD_DIM)
    dec = lambda a: a[N_META:].reshape(dbsz, t_dec, -1)
    return (
        y_prompt,
        dec(y_side),
        kv4(k[:, seq - WINDOW:], WINDOW),
        kv4(v[:, seq - WINDOW:], WINDOW),
        jnp.broadcast_to(km.reshape(1, N_META, N_KV_HEADS, HEAD_DIM), (bsz, N_META, N_KV_HEADS, HEAD_DIM)),
        jnp.broadcast_to(vm.reshape(1, N_META, N_KV_HEADS, HEAD_DIM), (bsz, N_META, N_KV_HEADS, HEAD_DIM)),
        p[:, seq - POOL_HIST:],
        up_tail[:, SUBLANES - (CONV_W - 1):],
        kv4(dec(k_side), t_dec),
        kv4(dec(v_side), t_dec),
        dec(p_side)[:, t_dec - POOL_HIST:],
        dec(up_side)[:, t_dec - (CONV_W - 1):],
    )
```

```python
import functools

import jax
import jax.numpy as jnp
from jax import lax
from jax.experimental import pallas as pl
from jax.experimental.pallas import tpu as pltpu

D_MODEL = 1024
N_META = 16
CHUNK = 64
HEAD_DIM = 64
N_Q_HEADS = 16
N_KV_HEADS = 2
WINDOW = 128
ROPE_THETA = 10000.0
POOL_WINDOWS = (2, 4, 8, 16)
N_POOL_GROUPS = 4
POOL_WIDTH = D_MODEL // 2
POOL_GROUP_W = POOL_WIDTH // N_POOL_GROUPS
POOL_HIST = max(POOL_WINDOWS) - 1
Q_W = N_Q_HEADS * HEAD_DIM
KV_W = N_KV_HEADS * HEAD_DIM
GATE_W = 2 * D_MODEL
IN_W = Q_W + 2 * KV_W + POOL_WIDTH + GATE_W
D_FF = ((8 * D_MODEL // 3) + 127) // 128 * 128
CONV_W = 3
RMS_EPS = 1e-6
PAST_LEN = 1024

LANES = 128
SUBLANES = 8
Q_BLOCKS = Q_W // LANES
BLOCKS_PER_KV = Q_BLOCKS // N_KV_HEADS
NEG = -1e30
LOG2E = 1.4426950408889634

K_OFF = Q_W
V_OFF = Q_W + KV_W
P_OFF = Q_W + 2 * KV_W
G_OFF = P_OFF + POOL_WIDTH

F32 = jnp.float32
BF16 = jnp.bfloat16

VMEM_LIMIT = 60 * 1024 * 1024


def _rmsnorm(x, g):
    ms = jnp.mean(x * x, axis=-1, keepdims=True)
    return (x * lax.rsqrt(ms + RMS_EPS)) * g


def _dot(a, b):
    return jnp.dot(a, b, preferred_element_type=F32)


def _rope(x, cos, sin_signed):
    lane = lax.broadcasted_iota(jnp.int32, (x.shape[0], LANES), 1)
    first_half = (lane & (HEAD_DIM // 2)) == 0
    outs = []
    for b in range(x.shape[1] // LANES):
        xb = x[:, b * LANES:(b + 1) * LANES]
        partner = jnp.where(first_half, pltpu.roll(xb, LANES - HEAD_DIM // 2, 1), pltpu.roll(xb, HEAD_DIM // 2, 1))
        outs.append(xb * cos + partner * sin_signed)
    return outs[0] if len(outs) == 1 else jnp.concatenate(outs, axis=1)


def _kv_variants(kv, dtype=BF16):
    lane = lax.broadcasted_iota(jnp.int32, kv.shape, 1)
    low = lane < HEAD_DIM
    swapped = pltpu.roll(kv, HEAD_DIM, 1)
    zero = jnp.zeros_like(kv)
    return (
        (jnp.where(low, kv, zero).astype(dtype), jnp.where(low, zero, swapped).astype(dtype)),
        (jnp.where(low, swapped, zero).astype(dtype), jnp.where(low, zero, kv).astype(dtype)),
    )


def _attend_group(qg, kst, vst, tab):
    s = lax.dot_general(qg, kst, (((1,), (1,)), ((), ())), preferred_element_type=F32) + tab
    npad = kst.shape[0] // 2
    se, so = s[:, :npad], s[:, npad:]
    pe = jnp.exp(se - jnp.max(se, axis=1, keepdims=True))
    po = jnp.exp(so - jnp.max(so, axis=1, keepdims=True))
    le = jnp.sum(pe, axis=1, keepdims=True)
    lo = jnp.sum(po, axis=1, keepdims=True)
    p = jnp.concatenate([pe, po], axis=1).astype(BF16)
    o = _dot(p, vst)
    lane = lax.broadcasted_iota(jnp.int32, o.shape, 1)
    return o * jnp.where(lane < HEAD_DIM, 1.0 / le, 1.0 / lo)


def _pool_group(ext, g, cnt=None):
    w = POOL_WINDOWS[g]
    s = ext
    k = 1
    while k < w:
        s = s + pltpu.roll(s, k, 0)
        k *= 2
    cur = ext[N_META:]
    return s[N_META:] / (float(w) if cnt is None else cnt) - cur


def _pool_project(pm, wgrp_ref, pscale):
    outs = []
    for g in range(N_POOL_GROUPS):
        sl = slice(g * POOL_GROUP_W, (g + 1) * POOL_GROUP_W)
        outs.append(_dot(pm[:, sl].astype(BF16), wgrp_ref[g].astype(BF16)))
    return (jnp.concatenate(outs, axis=1) * pscale).astype(BF16)


def _conv3(u, prev8, w, b):
    t = u.shape[0]
    ext = jnp.concatenate([prev8, u], axis=0)
    u1 = pltpu.roll(ext, 1, 0)[SUBLANES:SUBLANES + t]
    u2 = pltpu.roll(ext, 2, 0)[SUBLANES:SUBLANES + t]
    c = b + u2 * w[0:1]
    c = c + u1 * w[1:2]
    return c + u * w[2:3]


def _sigmoid(x):
    return 0.5 * jnp.tanh(0.5 * x) + 0.5


def _silu(x):
    half = 0.5 * x
    return half + half * jnp.tanh(half)


N_SIDE = N_META + 8 * 16
SIDE_FF_CHUNK = 256


def _side_kernel(xs_ref, ck_ref, cv_ref, cmk_ref, cmv_ref, spool_ref, sconv_ref, cos_ref, sin_ref, tabm_ref,
                 tabs_ref, gmix_ref, w_in_ref, bgate_ref, w_ao_ref, wgrp_ref, pscale_ref, w_po_ref, w_out_ref,
                 gffn_ref, w_up_ref, convw_ref, convb_ref, w_down_ref, gfin_ref,
                 q0_ref, k0_ref, v0_ref, kaug_ref, qaug_ref,
                 y_ref, k_ref, v_ref, p_ref, up_ref, o0_ref):
    dec_b = ck_ref.shape[0]
    t_dec = (N_SIDE - N_META) // dec_b
    x = xs_ref[...]
    h = _rmsnorm(x, gmix_ref[...]).astype(BF16)
    cos, sin = cos_ref[...], sin_ref[...]
    w_in = lambda c0, c1: w_in_ref[:, c0:c1].astype(BF16)
    q = _rope(_dot(h, w_in(0, Q_W)) * (HEAD_DIM ** -0.5), cos, sin).astype(BF16)
    k = _rope(_dot(h, w_in(K_OFF, K_OFF + KV_W)), cos, sin)
    v = _dot(h, w_in(V_OFF, V_OFF + KV_W))
    p = _dot(h, w_in(P_OFF, P_OFF + POOL_WIDTH))
    gates = _sigmoid(jnp.concatenate([_dot(h, w_in(G_OFF + c0, G_OFF + c0 + GATE_CHUNK))
                                      for c0 in range(0, GATE_W, GATE_CHUNK)], axis=1) + bgate_ref[...])
    k_ref[...] = k
    v_ref[...] = v
    p_ref[...] = p

    def attend_rows(r0, nrows, kx, vx, tab_ref):
        kvar, vvar = _kv_variants(kx), _kv_variants(vx)
        blocks = [None] * Q_BLOCKS
        for n in range(N_KV_HEADS):
            qg = jnp.concatenate([q[r0:r0 + nrows, (BLOCKS_PER_KV * n + bi) * LANES:(BLOCKS_PER_KV * n + bi + 1) * LANES]
                                  for bi in range(BLOCKS_PER_KV)], axis=0)
            kst = jnp.concatenate(kvar[n], axis=0)
            vst = jnp.concatenate(vvar[n], axis=0)
            o = _attend_group(qg, kst, vst, tab_ref[n])
            for bi in range(BLOCKS_PER_KV):
                blocks[BLOCKS_PER_KV * n + bi] = o[bi * nrows:(bi + 1) * nrows]
        return jnp.concatenate(blocks, axis=1)

    zpad = jnp.zeros((LANES - N_META, LANES), F32)
    o_rows = [attend_rows(0, N_META, jnp.concatenate([k[:N_META], zpad], axis=0),
                          jnp.concatenate([v[:N_META], zpad], axis=0), tabm_ref)]
    n_keys = N_META + WINDOW + t_dec
    zpad = jnp.zeros((2 * LANES - n_keys, LANES), F32)
    for b in range(dec_b):
        r0 = N_META + b * t_dec
        kx = jnp.concatenate([cmk_ref[b], ck_ref[b], k[r0:r0 + t_dec], zpad], axis=0)
        vx = jnp.concatenate([cmv_ref[b], cv_ref[b], v[r0:r0 + t_dec], zpad], axis=0)
        o_rows.append(attend_rows(r0, t_dec, kx, vx, tabs_ref))
    o_attn = jnp.concatenate(o_rows, axis=0).astype(BF16)

    row = lax.broadcasted_iota(jnp.int32, (N_META, LANES), 0)
    pm_rows = []
    for seg in range(1 + dec_b):
        if seg == 0:
            ext = jnp.concatenate([jnp.zeros((N_META, POOL_WIDTH), F32), p[:N_META]], axis=0)
        else:
            r0 = N_META + (seg - 1) * t_dec
            ext = jnp.concatenate([spool_ref[seg - 1], p[r0:r0 + t_dec]], axis=0)
        groups = []
        for g in range(N_POOL_GROUPS):
            cnt = jnp.minimum(POOL_WINDOWS[g], row + 1).astype(F32) if seg == 0 else None
            groups.append(_pool_group(ext[:, g * POOL_GROUP_W:(g + 1) * POOL_GROUP_W], g, cnt))
        pm_rows.append(jnp.concatenate(groups, axis=1))
    pm = jnp.concatenate(pm_rows, axis=0)

    pool = _pool_project(pm, wgrp_ref, pscale_ref[...])
    mix = gates[:, :D_MODEL] * _dot(o_attn, w_ao_ref[...]) + gates[:, D_MODEL:] * _dot(pool, w_po_ref[...])
    x1 = x + _dot(mix.astype(BF16), w_out_ref[...])

    hn = _rmsnorm(x1, gffn_ref[...]).astype(BF16)
    acc = jnp.zeros((N_SIDE, D_MODEL), F32)
    zprev = jnp.zeros((SUBLANES, SIDE_FF_CHUNK), F32)
    for c0 in range(0, D_FF, SIDE_FF_CHUNK):
        halves = []
        for off in (c0, D_FF + c0):
            cs = slice(off, off + SIDE_FF_CHUNK)
            u = _dot(hn, w_up_ref[:, cs])
            up_ref[:, cs] = u
            w, bias = convw_ref[:, cs], convb_ref[:, cs]
            segs = [_conv3(u[:N_META], zprev, w, bias)]
            for b in range(dec_b):
                r0 = N_META + b * t_dec
                segs.append(_conv3(u[r0:r0 + t_dec], sconv_ref[b, :, cs], w, bias))
            halves.append(jnp.concatenate(segs, axis=0))
        act = (_silu(halves[0]) * halves[1]).astype(BF16)
        acc = acc + _dot(act, w_down_ref[c0:c0 + SIDE_FF_CHUNK, :])
    y_ref[...] = _rmsnorm(x1 + acc, gfin_ref[...])

    def store_o0(j, blk, value):
        o0_ref[j * CHUNK:(j + 1) * CHUNK, blk * LANES:(blk + 1) * LANES] = value

    groups, scores, finish = _attention_stages(0, q0_ref, k0_ref, k0_ref.at[0:WINDOW], v0_ref, v0_ref.at[0:WINDOW],
                                               k_ref.at[0:N_META], v_ref.at[0:N_META], kaug_ref, qaug_ref, store_o0)
    pending = scores(*groups[0])
    for g, group in enumerate(groups):
        current = pending
        if g + 1 < len(groups):
            pending = scores(*groups[g + 1])
        finish(*group, *current)


def _side_call(args, q, k, v, kaug, qaug):
    out_shape = (
        jax.ShapeDtypeStruct((N_SIDE, D_MODEL), F32),
        jax.ShapeDtypeStruct((N_SIDE, KV_W), F32),
        jax.ShapeDtypeStruct((N_SIDE, KV_W), F32),
        jax.ShapeDtypeStruct((N_SIDE, POOL_WIDTH), F32),
        jax.ShapeDtypeStruct((N_SIDE, 2 * D_FF), F32),
        jax.ShapeDtypeStruct((TQ, Q_W), BF16),
    )
    first_tile = lambda w: pl.BlockSpec((None, TQ, w), lambda i: (0, 0, 0))
    return pl.pallas_call(
        _side_kernel,
        grid=(1,),
        in_specs=[_const_spec(a.shape) for a in args] + [first_tile(Q_W), first_tile(KV_W), first_tile(KV_W),
                                                         _const_spec(kaug.shape), _const_spec(qaug.shape)],
        out_specs=tuple(_const_spec(s.shape) for s in out_shape),
        out_shape=out_shape,
        compiler_params=pltpu.CompilerParams(dimension_semantics=("arbitrary",), vmem_limit_bytes=VMEM_LIMIT),
        name="side_rows",
    )(*args, q, k, v, kaug, qaug)


TM_IN = 1024
IN_SUB = 256
GATE_CHUNK = 512


W_IN_CAST_CHUNK = 768


def _inproj_kernel(n_cast, x_ref, cos_ref, sin_ref, gmix_ref, w_in_f32_ref, bgate_ref, *refs):
    cast_in, (q_ref, k_ref, v_ref, p_ref, g_ref) = refs[:n_cast], refs[n_cast:n_cast + 5]
    cast_out, w_in_ref = refs[n_cast + 5:2 * n_cast + 5], refs[2 * n_cast + 5]

    @pl.when((pl.program_id(0) == 0) & (pl.program_id(1) == 0))
    def _():
        for c0 in range(0, IN_W, W_IN_CAST_CHUNK):
            w_in_ref[:, c0:c0 + W_IN_CAST_CHUNK] = w_in_f32_ref[:, c0:c0 + W_IN_CAST_CHUNK].astype(BF16)

    for src, dst in zip(cast_in, cast_out):
        dst[...] = src[...].astype(BF16)

    subs = [slice(r0, r0 + IN_SUB) for r0 in range(0, x_ref.shape[0], IN_SUB)]
    hs = [_rmsnorm(x_ref[rows], gmix_ref[...]).astype(BF16) for rows in subs]
    for rows, h in zip(subs, hs):
        cos, sin = cos_ref[rows], sin_ref[rows]
        for c0 in range(0, GATE_W, GATE_CHUNK):
            z = _dot(h, w_in_ref[:, G_OFF + c0:G_OFF + c0 + GATE_CHUNK]) + bgate_ref[:, c0:c0 + GATE_CHUNK]
            g_ref[rows, c0:c0 + GATE_CHUNK] = _sigmoid(z).astype(BF16)
        q_ref[rows] = _rope(_dot(h, w_in_ref[:, 0:Q_W]) * (HEAD_DIM ** -0.5 * LOG2E), cos, sin).astype(BF16)
        p_ref[rows] = _dot(h, w_in_ref[:, P_OFF:P_OFF + POOL_WIDTH])
        kv = _dot(h, w_in_ref[:, K_OFF:K_OFF + 2 * KV_W])
        k_ref[rows] = _rope(kv[:, :KV_W], cos, sin)
        v_ref[rows] = kv[:, KV_W:]


def _const_spec(shape):
    nd = len(shape)
    return pl.BlockSpec(shape, lambda *_: (0,) * nd)


BF16_SUBLANES = 16


def _inproj_call(x, cos, sin, gmix, w_in, bgate, other_weights):
    bsz, seq, _ = x.shape
    tm = TM_IN
    tiles = seq // tm
    n_steps = bsz * tiles
    row = lambda w: pl.BlockSpec((None, tm, w), lambda b, i: (b, i, 0))

    def slab_spec(w):
        n_slabs = n_steps
        while w.shape[0] % (n_slabs * BF16_SUBLANES):
            n_slabs //= 2
        return pl.BlockSpec((w.shape[0] // n_slabs, w.shape[1]),
                            lambda b, i: (jnp.minimum(b * tiles + i, n_slabs - 1), 0))

    slabs = [slab_spec(w) for w in other_weights]
    outs = pl.pallas_call(
        functools.partial(_inproj_kernel, len(other_weights)),
        grid=(bsz, tiles),
        in_specs=[row(D_MODEL), pl.BlockSpec((tm, LANES), lambda b, i: (i, 0)),
                  pl.BlockSpec((tm, LANES), lambda b, i: (i, 0)),
                  _const_spec(gmix.shape), _const_spec(w_in.shape), _const_spec(bgate.shape)] + slabs,
        out_specs=(row(Q_W), row(KV_W), row(KV_W), row(POOL_WIDTH), row(GATE_W), *slabs),
        out_shape=(jax.ShapeDtypeStruct((bsz, seq, Q_W), BF16), jax.ShapeDtypeStruct((bsz, seq, KV_W), F32),
                   jax.ShapeDtypeStruct((bsz, seq, KV_W), F32), jax.ShapeDtypeStruct((bsz, seq, POOL_WIDTH), F32),
                   jax.ShapeDtypeStruct((bsz, seq, GATE_W), BF16),
                   *[jax.ShapeDtypeStruct(w.shape, BF16) for w in other_weights]),
        scratch_shapes=[pltpu.VMEM(w_in.shape, BF16)],
        compiler_params=pltpu.CompilerParams(dimension_semantics=("arbitrary", "arbitrary"),
                                             vmem_limit_bytes=VMEM_LIMIT),
        name="in_proj",
    )(x, cos, sin, gmix, w_in, bgate, *other_weights)
    return outs[:5], outs[5:]


TQ = 512
BAND = WINDOW + CHUNK
KEYS_PAD = BAND + N_META + 16
N_TABS = WINDOW // CHUNK + 1
SUM_ROWS = 16


def _attention_stages(i, q_ref, kc_ref, kp_ref, vc_ref, vp_ref, mk_ref, mv_ref, kaug_ref, qaug_ref, store):
    kvar = _kv_variants(jnp.concatenate([kp_ref[...], kc_ref[...]], axis=0), BF16)
    vvar = _kv_variants(jnp.concatenate([vp_ref[...], vc_ref[...]], axis=0), F32)
    mkvar, mvvar = _kv_variants(mk_ref[...], BF16), _kv_variants(mv_ref[...], F32)
    kpad = jnp.zeros((KEYS_PAD - BAND - N_META, LANES), BF16)
    vpad = jnp.zeros((KEYS_PAD - BAND - N_META, LANES), F32)
    row = lax.broadcasted_iota(jnp.int32, (SUM_ROWS, 2 * KEYS_PAD), 0)
    col = lax.broadcasted_iota(jnp.int32, (SUM_ROWS, 2 * KEYS_PAD), 1)
    ones = jnp.where((row == 0) & (col < KEYS_PAD) | (row == 1) & (col >= KEYS_PAD), 1.0, 0.0).astype(BF16)

    def scores(j, n):
        tsel = jnp.minimum(i * (TQ // CHUNK) + j, N_TABS - 1)
        rows = slice(j * CHUNK, (j + 1) * CHUNK)
        band = slice(j * CHUNK, j * CHUNK + BAND)
        kst = jnp.concatenate([kvar[n][0][band], mkvar[n][0], kpad, kvar[n][1][band], mkvar[n][1], kpad], axis=0)
        vst = jnp.concatenate([vvar[n][0][band], mvvar[n][0], vpad, vvar[n][1][band], mvvar[n][1], vpad], axis=0)
        qg = jnp.concatenate([q_ref[rows, (BLOCKS_PER_KV * n + bi) * LANES:(BLOCKS_PER_KV * n + bi + 1) * LANES]
                              for bi in range(BLOCKS_PER_KV)], axis=0)
        s = lax.dot_general(jnp.concatenate([kst, kaug_ref[tsel]], axis=1),
                            jnp.concatenate([qg, qaug_ref[n]], axis=1),
                            (((1,), (1,)), ((), ())), preferred_element_type=F32)
        vt = jnp.concatenate([vst.T.astype(BF16), ones], axis=0)
        return s, vt

    def finish(j, n, s, vt):
        se, so = s[:KEYS_PAD], s[KEYS_PAD:]
        pe = jnp.exp2(se - jnp.max(se, axis=0, keepdims=True))
        po = jnp.exp2(so - jnp.max(so, axis=0, keepdims=True))
        o = _dot(vt, jnp.concatenate([pe, po], axis=0).astype(BF16))
        o = jnp.concatenate([o[:HEAD_DIM] / o[LANES:LANES + 1], o[HEAD_DIM:LANES] / o[LANES + 1:LANES + 2]], axis=0).T
        for bi in range(BLOCKS_PER_KV):
            store(j, BLOCKS_PER_KV * n + bi, o[bi * CHUNK:(bi + 1) * CHUNK].astype(BF16))

    groups = [(j, n) for j in range(TQ // CHUNK) for n in range(N_KV_HEADS)]
    return groups, scores, finish


def _attn_aug_tables(sinks):
    n_keys = BAND + N_META
    r = jnp.arange(KEYS_PAD)
    lane = jnp.arange(LANES)[None, :]
    kaug = []
    for c in range(N_TABS):
        masked = (r < (WINDOW // CHUNK - c) * CHUNK) | (r > n_keys)
        mask_col = jnp.where(masked, NEG, 0.0)[:, None] * (lane == 0)
        sink_row = (r == n_keys)[:, None]
        even = mask_col + jnp.where(sink_row & ((lane == 1) | (lane == 2)), 1.0, 0.0)
        odd = mask_col + jnp.where(sink_row & ((lane == 3) | (lane == 4)), 1.0, 0.0)
        kaug.append(jnp.concatenate([even, odd], axis=0))
    s2 = sinks.astype(F32) * LOG2E
    hi = s2.astype(BF16).astype(F32)
    lo = s2 - hi
    head = 2 * jnp.arange(Q_BLOCKS)
    cols = [jnp.ones((Q_BLOCKS,), F32), hi[head], lo[head], hi[head + 1], lo[head + 1]]
    qaug = jnp.stack(cols, axis=1)
    qaug = jnp.pad(qaug, ((0, 0), (0, LANES - qaug.shape[1])))
    qaug = jnp.broadcast_to(qaug[:, None, :], (Q_BLOCKS, CHUNK, LANES)).reshape(N_KV_HEADS, BLOCKS_PER_KV * CHUNK, LANES)
    return jnp.stack(kaug).astype(BF16), qaug.astype(BF16)


TM_POST = 512
POST_SUB = 256
FF_CHUNK = 256
FF_SKEW = 3
FF_LANE_BLOCKS = FF_CHUNK // LANES
DOWN_GROUP = 4


def _pool_weight_kernel(wgrp_ref, pscale_ref, w_po_ref, out_ref):
    for g in range(N_POOL_GROUPS):
        sl = slice(g * POOL_GROUP_W, (g + 1) * POOL_GROUP_W)
        out_ref[sl, :] = _dot((wgrp_ref[g] * pscale_ref[:, sl]).astype(BF16), w_po_ref[sl, :]).astype(BF16)


def _pool_weight_call(wgrp, pscale, w_po):
    return pl.pallas_call(
        _pool_weight_kernel,
        out_shape=jax.ShapeDtypeStruct((POOL_WIDTH, D_MODEL), BF16),
        name="pool_weights",
    )(wgrp, pscale, w_po)


def _stream_kernel(tiles_per_batch, n_tiles,
                   q_ref, kc_ref, kp_ref, vc_ref, vp_ref, mk_ref, mv_ref, kaug_ref, qaug_ref,
                   x_ref, p_ref, pprev_ref, pmeta_ref, g_ref, upmeta_ref, w_ao_ref, w_pool_ref,
                   w_out_ref, gffn_ref, w_up_ref, convw_ref, convb_ref, w_down_ref, gfin_ref,
                   o0_ref, y_ref, carry_ref, oattn_ref, pext_ref, x1_ref, hn_ref, act_ref, *uext_refs):
    t = pl.program_id(0)
    i_attn = lax.rem(jnp.minimum(t + 1, n_tiles - 1), tiles_per_batch)
    i = lax.rem(t, tiles_per_batch)
    slot_new, slot_old = lax.rem(t + 1, 2), lax.rem(t, 2)
    tm = x_ref.shape[0]
    n_chunks = D_FF // FF_CHUNK

    @pl.when(t == 0)
    def _():
        oattn_ref[0] = o0_ref[...]

    @pl.when(i == 0)
    def _():
        pext_ref[0:N_META] = pmeta_ref[...]
        for c in range(n_chunks):
            for half, off in enumerate((c * FF_CHUNK, D_FF + c * FF_CHUNK)):
                for lb in range(FF_LANE_BLOCKS):
                    uext_refs[c][half * FF_LANE_BLOCKS + lb, POST_SUB:POST_SUB + SUBLANES] = (
                        upmeta_ref[:, off + lb * LANES:off + (lb + 1) * LANES])

    @pl.when(i > 0)
    def _():
        pext_ref[0:N_META] = pprev_ref[...]

    def store_attn(j, blk, value):
        oattn_ref[slot_new, j * CHUNK:(j + 1) * CHUNK, blk * LANES:(blk + 1) * LANES] = value

    groups, scores, finish = _attention_stages(i_attn, q_ref, kc_ref, kp_ref, vc_ref, vp_ref, mk_ref, mv_ref,
                                               kaug_ref, qaug_ref, store_attn)
    pending = [None]

    def attention_group(g):
        current = pending[0]
        if g + 1 < len(groups):
            pending[0] = scores(*groups[g + 1])
        finish(*groups[g], *current)

    pext_ref[N_META:] = p_ref[...]
    subs = [slice(r0, r0 + POST_SUB) for r0 in range(0, tm, POST_SUB)]

    mixes = []
    for rows in subs:
        attn = _dot(oattn_ref[slot_old, rows], w_ao_ref[...])
        ext_rows = slice(rows.start, rows.stop + N_META)
        pm = jnp.concatenate([_pool_group(pext_ref[ext_rows, g * POOL_GROUP_W:(g + 1) * POOL_GROUP_W], g)
                              for g in range(N_POOL_GROUPS)], axis=1)
        pool = _dot(pm.astype(BF16), w_pool_ref[...])
        mix = g_ref[rows, :D_MODEL].astype(F32) * attn + g_ref[rows, D_MODEL:].astype(F32) * pool
        mixes.append(mix.astype(BF16))
    for rows, mix in zip(subs, mixes):
        x1 = x_ref[rows] + _dot(mix, w_out_ref[...])
        x1_ref[rows] = x1
        hn_ref[rows] = _rmsnorm(x1, gffn_ref[...]).astype(BF16)

    def up_project(rows, c):
        for half, off in enumerate((c * FF_CHUNK, D_FF + c * FF_CHUNK)):
            u = _dot(hn_ref[rows], w_up_ref[:, off:off + FF_CHUNK])
            for lb in range(FF_LANE_BLOCKS):
                blk = uext_refs[c].at[half * FF_LANE_BLOCKS + lb]
                blk[0:SUBLANES] = blk[POST_SUB:POST_SUB + SUBLANES]
                blk[SUBLANES:] = u[:, lb * LANES:(lb + 1) * LANES]
            if rows.stop == tm:
                carry_ref[:, off:off + FF_CHUNK] = u[POST_SUB - SUBLANES:]

    def conv(rows, c, half):
        outs = []
        for lb in range(FF_LANE_BLOCKS):
            off = half * D_FF + c * FF_CHUNK + lb * LANES
            w, b = convw_ref[:, off:off + LANES], convb_ref[:, off:off + LANES]
            ext = uext_refs[c].at[half * FF_LANE_BLOCKS + lb]
            cv = b + ext[SUBLANES - 2:SUBLANES - 2 + POST_SUB] * w[0:1]
            cv = cv + ext[SUBLANES - 1:SUBLANES - 1 + POST_SUB] * w[1:2]
            outs.append(cv + ext[SUBLANES:] * w[2:3])
        return jnp.concatenate(outs, axis=1)

    steps = [(rows, c) for rows in subs for c in range(n_chunks)]
    for step in steps[:FF_SKEW]:
        up_project(*step)
    pending[0] = scores(*groups[0])
    acc = None
    for j, (rows, c) in enumerate(steps):
        act_ref[rows, c * FF_CHUNK:(c + 1) * FF_CHUNK] = (_silu(conv(rows, c, 0)) * conv(rows, c, 1)).astype(BF16)
        if j + FF_SKEW < len(steps):
            up_project(*steps[j + FF_SKEW])
        if (c + 1) % DOWN_GROUP == 0 or c == n_chunks - 1:
            ks = slice((c // DOWN_GROUP) * DOWN_GROUP * FF_CHUNK, (c + 1) * FF_CHUNK)
            d = _dot(act_ref[rows, ks], w_down_ref[ks, :])
            acc = d if c < DOWN_GROUP else acc + d
        if c == n_chunks - 1:
            y_ref[rows] = _rmsnorm(x1_ref[rows] + acc, gfin_ref[...])
        if j < len(groups):
            attention_group(j)
    assert len(groups) <= len(steps)


def _stream_call(q, k, v, mk, mv, kaug, qaug, x, p, p_meta, gates, o0, up_meta8, w_ao, w_pool, w_out, gffn, w_up,
                 convw, convb, w_down, gfin):
    bsz, seq, _ = x.shape
    tm = TM_POST
    assert tm == TQ
    tiles_per_batch = seq // tm
    n_tiles = bsz * tiles_per_batch

    def new_tile(t):
        a = jnp.minimum(t + 1, n_tiles - 1)
        return a // tiles_per_batch, lax.rem(a, tiles_per_batch)

    def old_tile(t):
        return t // tiles_per_batch, lax.rem(t, tiles_per_batch)

    new_row = lambda w: pl.BlockSpec((None, tm, w), lambda t: (*new_tile(t), 0))
    old_row = lambda w: pl.BlockSpec((None, tm, w), lambda t: (*old_tile(t), 0))

    def kv_prev_map(t):
        b, i = new_tile(t)
        return b, jnp.maximum(i * (tm // WINDOW) - 1, 0), 0

    def p_prev_map(t):
        b, i = old_tile(t)
        return b, jnp.maximum(i * (tm // N_META) - 1, 0), 0

    kv_prev = pl.BlockSpec((None, WINDOW, KV_W), kv_prev_map)
    consts = (up_meta8, w_ao, w_pool, w_out, gffn, w_up, convw, convb, w_down, gfin, o0)
    return pl.pallas_call(
        functools.partial(_stream_kernel, tiles_per_batch, n_tiles),
        grid=(n_tiles,),
        in_specs=[new_row(Q_W), new_row(KV_W), kv_prev, new_row(KV_W), kv_prev, _const_spec(mk.shape),
                  _const_spec(mv.shape), _const_spec(kaug.shape), _const_spec(qaug.shape),
                  old_row(D_MODEL), old_row(POOL_WIDTH), pl.BlockSpec((None, N_META, POOL_WIDTH), p_prev_map),
                  _const_spec(p_meta.shape), old_row(GATE_W)] + [_const_spec(a.shape) for a in consts],
        out_specs=(old_row(D_MODEL), pl.BlockSpec((None, SUBLANES, 2 * D_FF), lambda t: (old_tile(t)[0], 0, 0))),
        out_shape=(jax.ShapeDtypeStruct((bsz, seq, D_MODEL), F32),
                   jax.ShapeDtypeStruct((bsz, SUBLANES, 2 * D_FF), F32)),
        scratch_shapes=[pltpu.VMEM((2, tm, Q_W), BF16), pltpu.VMEM((N_META + tm, POOL_WIDTH), F32),
                        pltpu.VMEM((tm, D_MODEL), F32), pltpu.VMEM((tm, D_MODEL), BF16),
                        pltpu.VMEM((tm, D_FF), BF16)]
                       + [pltpu.VMEM((2 * FF_LANE_BLOCKS, SUBLANES + POST_SUB, LANES), F32)
                          for _ in range(D_FF // FF_CHUNK)],
        compiler_params=pltpu.CompilerParams(dimension_semantics=("arbitrary",), vmem_limit_bytes=VMEM_LIMIT),
        name="attn_mixer_ffn",
    )(q, k, k, v, v, mk, mv, kaug, qaug, x, p, p, p_meta, gates, *consts)


def _rope_tables(pos):
    half = HEAD_DIM // 2
    inv = ROPE_THETA ** (-jnp.arange(half, dtype=F32) / half)
    ang = pos.astype(F32)[:, None] * inv[None, :]
    cos, sin = jnp.cos(ang), jnp.sin(ang)
    return jnp.tile(cos, (1, LANES // half)), jnp.tile(jnp.concatenate([-sin, sin], axis=1), (1, LANES // HEAD_DIM))


def _score_table(sinks, rows_per_block, n_keys, keys_pad, masked_prefix=0):
    col = jnp.arange(keys_pad)
    base = jnp.where((col >= masked_prefix) & (col < n_keys), 0.0, NEG).astype(F32)
    head = (2 * jnp.arange(Q_BLOCKS)[:, None] + jnp.arange(2)[None, :])
    tab = jnp.where(col[None, None, :] == n_keys, sinks.astype(F32)[head][:, :, None], base[None, None, :])
    tab = tab.reshape(N_KV_HEADS, BLOCKS_PER_KV, 1, 2 * keys_pad)
    tab = jnp.broadcast_to(tab, (N_KV_HEADS, BLOCKS_PER_KV, rows_per_block, 2 * keys_pad))
    return tab.reshape(N_KV_HEADS, BLOCKS_PER_KV * rows_per_block, 2 * keys_pad)


def kernel(x_prompt, x_sample, cache_swa_k, cache_swa_v, cache_meta_k, cache_meta_v, state_pool, state_conv,
           meta_tokens, g_norm_mix, w_in, b_gate, sinks, w_attn_o, w_pool_grp, pool_scale, w_pool_o, w_out,
           g_norm_ffn, w_up, conv_w, conv_b, w_down, g_norm_final):
    bsz, seq, _ = x_prompt.shape
    dbsz, t_dec, _ = x_sample.shape
    row2 = lambda a: a.reshape(1, -1)
    gmix, gffn, gfin = row2(g_norm_mix), row2(g_norm_ffn), row2(g_norm_final)
    bgate, pscale, convb = row2(b_gate), row2(pool_scale), row2(conv_b)

    cos_p, sin_p = _rope_tables(N_META + jnp.arange(seq, dtype=jnp.int32))
    (q, k, v, p, gates), (w_ao_b, w_po_b, w_out_b, w_up_b, w_down_b) = _inproj_call(
        x_prompt, cos_p, sin_p, gmix, w_in, bgate, (w_attn_o, w_pool_o, w_out, w_up, w_down))

    pos_side = jnp.concatenate([jnp.arange(N_META, dtype=jnp.int32),
                                jnp.tile(N_META + PAST_LEN + jnp.arange(t_dec, dtype=jnp.int32), dbsz)])
    cos_s, sin_s = _rope_tables(pos_side)
    xs = jnp.concatenate([meta_tokens, x_sample.reshape(dbsz * t_dec, D_MODEL)], axis=0)
    spool16 = jnp.pad(state_pool, ((0, 0), (N_META - POOL_HIST, 0), (0, 0)))
    sconv8 = jnp.pad(state_conv, ((0, 0), (SUBLANES - (CONV_W - 1), 0), (0, 0)))
    tab_meta = _score_table(sinks, N_META, N_META, LANES)
    tab_dec = _score_table(sinks, t_dec, N_META + WINDOW + t_dec, 2 * LANES)
    kaug, qaug = _attn_aug_tables(sinks)
    y_side, k_side, v_side, p_side, up_side, o_first = _side_call(
        (xs, cache_swa_k.reshape(dbsz, WINDOW, KV_W), cache_swa_v.reshape(dbsz, WINDOW, KV_W),
         cache_meta_k.reshape(dbsz, N_META, KV_W), cache_meta_v.reshape(dbsz, N_META, KV_W), spool16, sconv8,
         cos_s, sin_s, tab_meta, tab_dec, gmix, w_in, bgate, w_ao_b, w_pool_grp, pscale, w_po_b, w_out_b,
         gffn, w_up_b, conv_w, convb, w_down_b, gfin), q, k, v, kaug, qaug)
    km, vm, p_meta = k_side[:N_META], v_side[:N_META], p_side[:N_META]
    up_meta8 = up_side[N_META - SUBLANES:N_META]

    w_pool_b = _pool_weight_call(w_pool_grp, pscale, w_po_b)
    y_prompt, up_tail = _stream_call(q, k, v, km, vm, kaug, qaug, x_prompt, p, p_meta, gates, o_first, up_meta8,
                                     w_ao_b, w_pool_b, w_out_b, gffn, w_up_b, conv_w, convb, w_down_b, gfin)

    kv4 = lambda a, n: a.reshape(a.shape[0], n, N_KV_HEADS, HEAD_DIM)
    dec = lambda a: a[N_META:].reshape(dbsz, t_dec, -1)
    return (
        y_prompt,
        dec(y_side),
        kv4(k[:, seq - WINDOW:], WINDOW),
        kv4(v[:, seq - WINDOW:], WINDOW),
        jnp.broadcast_to(km.reshape(1, N_META, N_KV_HEADS, HEAD_DIM), (bsz, N_META, N_KV_HEADS, HEAD_DIM)),
        jnp.broadcast_to(vm.reshape(1, N_META, N_KV_HEADS, HEAD_DIM), (bsz, N_META, N_KV_HEADS, HEAD_DIM)),
        p[:, seq - POOL_HIST:],
        up_tail[:, SUBLANES - (CONV_W - 1):],
        kv4(dec(k_side), t_dec),
        kv4(dec(v_side), t_dec),
        dec(p_side)[:, t_dec - POOL_HIST:],
        dec(up_side)[:, t_dec - (CONV_W - 1):],
    )
```

```python
import functools

import numpy as np
import jax
import jax.numpy as jnp
from jax import lax
from jax.experimental import pallas as pl
from jax.experimental.pallas import tpu as pltpu

D_MODEL = 1024
N_META = 16
CHUNK = 64
HEAD_DIM = 64
N_Q_HEADS = 16
N_KV_HEADS = 2
WINDOW = 128
ROPE_THETA = 10000.0
POOL_WINDOWS = (2, 4, 8, 16)
N_POOL_GROUPS = 4
POOL_WIDTH = D_MODEL // 2
POOL_GROUP_W = POOL_WIDTH // N_POOL_GROUPS
POOL_HIST = max(POOL_WINDOWS) - 1
Q_W = N_Q_HEADS * HEAD_DIM
KV_W = N_KV_HEADS * HEAD_DIM
GATE_W = 2 * D_MODEL
IN_W = Q_W + 2 * KV_W + POOL_WIDTH + GATE_W
D_FF = ((8 * D_MODEL // 3) + 127) // 128 * 128
CONV_W = 3
RMS_EPS = 1e-6
PAST_LEN = 1024

LANES = 128
SUBLANES = 8
Q_BLOCKS = Q_W // LANES
BLOCKS_PER_KV = Q_BLOCKS // N_KV_HEADS
NEG = -1e30
LOG2E = 1.4426950408889634

K_OFF = Q_W
V_OFF = Q_W + KV_W
P_OFF = Q_W + 2 * KV_W
G_OFF = P_OFF + POOL_WIDTH

F32 = jnp.float32
BF16 = jnp.bfloat16

VMEM_LIMIT = 60 * 1024 * 1024


def _rmsnorm(x, g):
    ms = jnp.mean(x * x, axis=-1, keepdims=True)
    return (x * lax.rsqrt(ms + RMS_EPS)) * g


def _dot(a, b):
    return jnp.dot(a, b, preferred_element_type=F32)


def _rope(x, cos, sin_signed):
    lane = lax.broadcasted_iota(jnp.int32, (x.shape[0], LANES), 1)
    first_half = (lane & (HEAD_DIM // 2)) == 0
    outs = []
    for b in range(x.shape[1] // LANES):
        xb = x[:, b * LANES:(b + 1) * LANES]
        partner = jnp.where(first_half, pltpu.roll(xb, LANES - HEAD_DIM // 2, 1), pltpu.roll(xb, HEAD_DIM // 2, 1))
        outs.append(xb * cos + partner * sin_signed)
    return outs[0] if len(outs) == 1 else jnp.concatenate(outs, axis=1)


def _kv_variants(kv, dtype=BF16):
    lane = lax.broadcasted_iota(jnp.int32, kv.shape, 1)
    low = lane < HEAD_DIM
    swapped = pltpu.roll(kv, HEAD_DIM, 1)
    zero = jnp.zeros_like(kv)
    return (
        (jnp.where(low, kv, zero).astype(dtype), jnp.where(low, zero, swapped).astype(dtype)),
        (jnp.where(low, swapped, zero).astype(dtype), jnp.where(low, zero, kv).astype(dtype)),
    )


def _attend_group(qg, kst, vst, tab):
    s = lax.dot_general(qg, kst, (((1,), (1,)), ((), ())), preferred_element_type=F32) + tab
    npad = kst.shape[0] // 2
    se, so = s[:, :npad], s[:, npad:]
    pe = jnp.exp(se - jnp.max(se, axis=1, keepdims=True))
    po = jnp.exp(so - jnp.max(so, axis=1, keepdims=True))
    le = jnp.sum(pe, axis=1, keepdims=True)
    lo = jnp.sum(po, axis=1, keepdims=True)
    p = jnp.concatenate([pe, po], axis=1).astype(BF16)
    o = _dot(p, vst)
    lane = lax.broadcasted_iota(jnp.int32, o.shape, 1)
    return o * jnp.where(lane < HEAD_DIM, 1.0 / le, 1.0 / lo)


def _pool_group(ext, g, cnt=None):
    w = POOL_WINDOWS[g]
    s = ext
    k = 1
    while k < w:
        s = s + pltpu.roll(s, k, 0)
        k *= 2
    cur = ext[N_META:]
    return s[N_META:] / (float(w) if cnt is None else cnt) - cur


def _pool_project(pm, wgrp_ref, pscale):
    outs = []
    for g in range(N_POOL_GROUPS):
        sl = slice(g * POOL_GROUP_W, (g + 1) * POOL_GROUP_W)
        outs.append(_dot(pm[:, sl].astype(BF16), wgrp_ref[g].astype(BF16)))
    return (jnp.concatenate(outs, axis=1) * pscale).astype(BF16)


def _conv3(u, prev8, w, b):
    t = u.shape[0]
    ext = jnp.concatenate([prev8, u], axis=0)
    u1 = pltpu.roll(ext, 1, 0)[SUBLANES:SUBLANES + t]
    u2 = pltpu.roll(ext, 2, 0)[SUBLANES:SUBLANES + t]
    c = b + u2 * w[0:1]
    c = c + u1 * w[1:2]
    return c + u * w[2:3]


def _sigmoid(x):
    return 0.5 * jnp.tanh(0.5 * x) + 0.5


def _silu(x):
    half = 0.5 * x
    return half + half * jnp.tanh(half)


N_SIDE = N_META + 8 * 16
SIDE_FF_CHUNK = 256


def _side_kernel(xs_ref, ck_ref, cv_ref, cmk_ref, cmv_ref, spool_ref, sconv_ref, cos_ref, sin_ref, tabm_ref,
                 tabs_ref, gmix_ref, w_in_ref, bgate_ref, w_ao_ref, wgrp_ref, pscale_ref, w_po_ref, w_out_ref,
                 gffn_ref, w_up_ref, convw_ref, convb_ref, w_down_ref, gfin_ref,
                 q0_ref, k0_ref, v0_ref, kaug_ref, qaug_ref,
                 y_ref, k_ref, v_ref, p_ref, up_ref, o0_ref, km_ref, vm_ref, pm_ref, upm_ref):
    dec_b = ck_ref.shape[0]
    t_dec = (N_SIDE - N_META) // dec_b
    x = xs_ref[...]
    h = _rmsnorm(x, gmix_ref[...]).astype(BF16)
    cos, sin = cos_ref[...], sin_ref[...]
    w_in = lambda c0, c1: w_in_ref[:, c0:c1].astype(BF16)
    q = _rope(_dot(h, w_in(0, Q_W)) * (HEAD_DIM ** -0.5), cos, sin).astype(BF16)
    k = _rope(_dot(h, w_in(K_OFF, K_OFF + KV_W)), cos, sin)
    v = _dot(h, w_in(V_OFF, V_OFF + KV_W))
    p = _dot(h, w_in(P_OFF, P_OFF + POOL_WIDTH))
    gates = _sigmoid(jnp.concatenate([_dot(h, w_in(G_OFF + c0, G_OFF + c0 + GATE_CHUNK))
                                      for c0 in range(0, GATE_W, GATE_CHUNK)], axis=1) + bgate_ref[...])
    k_ref[...] = k
    v_ref[...] = v
    p_ref[...] = p
    km_ref[...] = k[:N_META]
    vm_ref[...] = v[:N_META]
    pm_ref[...] = p[:N_META]

    def attend_rows(r0, nrows, kx, vx, tab_ref):
        kvar, vvar = _kv_variants(kx), _kv_variants(vx)
        blocks = [None] * Q_BLOCKS
        for n in range(N_KV_HEADS):
            qg = jnp.concatenate([q[r0:r0 + nrows, (BLOCKS_PER_KV * n + bi) * LANES:(BLOCKS_PER_KV * n + bi + 1) * LANES]
                                  for bi in range(BLOCKS_PER_KV)], axis=0)
            kst = jnp.concatenate(kvar[n], axis=0)
            vst = jnp.concatenate(vvar[n], axis=0)
            o = _attend_group(qg, kst, vst, tab_ref[n])
            for bi in range(BLOCKS_PER_KV):
                blocks[BLOCKS_PER_KV * n + bi] = o[bi * nrows:(bi + 1) * nrows]
        return jnp.concatenate(blocks, axis=1)

    zpad = jnp.zeros((LANES - N_META, LANES), F32)
    o_rows = [attend_rows(0, N_META, jnp.concatenate([k[:N_META], zpad], axis=0),
                          jnp.concatenate([v[:N_META], zpad], axis=0), tabm_ref)]
    n_keys = N_META + WINDOW + t_dec
    zpad = jnp.zeros((2 * LANES - n_keys, LANES), F32)
    for b in range(dec_b):
        r0 = N_META + b * t_dec
        kx = jnp.concatenate([cmk_ref[b], ck_ref[b], k[r0:r0 + t_dec], zpad], axis=0)
        vx = jnp.concatenate([cmv_ref[b], cv_ref[b], v[r0:r0 + t_dec], zpad], axis=0)
        o_rows.append(attend_rows(r0, t_dec, kx, vx, tabs_ref))
    o_attn = jnp.concatenate(o_rows, axis=0).astype(BF16)

    row = lax.broadcasted_iota(jnp.int32, (N_META, LANES), 0)
    pm_rows = []
    for seg in range(1 + dec_b):
        if seg == 0:
            ext = jnp.concatenate([jnp.zeros((N_META, POOL_WIDTH), F32), p[:N_META]], axis=0)
        else:
            r0 = N_META + (seg - 1) * t_dec
            ext = jnp.concatenate([spool_ref[seg - 1], p[r0:r0 + t_dec]], axis=0)
        groups = []
        for g in range(N_POOL_GROUPS):
            cnt = jnp.minimum(POOL_WINDOWS[g], row + 1).astype(F32) if seg == 0 else None
            groups.append(_pool_group(ext[:, g * POOL_GROUP_W:(g + 1) * POOL_GROUP_W], g, cnt))
        pm_rows.append(jnp.concatenate(groups, axis=1))
    pm = jnp.concatenate(pm_rows, axis=0)

    pool = _pool_project(pm, wgrp_ref, pscale_ref[...])
    mix = gates[:, :D_MODEL] * _dot(o_attn, w_ao_ref[...]) + gates[:, D_MODEL:] * _dot(pool, w_po_ref[...])
    x1 = x + _dot(mix.astype(BF16), w_out_ref[...])

    hn = _rmsnorm(x1, gffn_ref[...]).astype(BF16)
    acc = jnp.zeros((N_SIDE, D_MODEL), F32)
    zprev = jnp.zeros((SUBLANES, SIDE_FF_CHUNK), F32)
    for c0 in range(0, D_FF, SIDE_FF_CHUNK):
        halves = []
        for off in (c0, D_FF + c0):
            cs = slice(off, off + SIDE_FF_CHUNK)
            u = _dot(hn, w_up_ref[:, cs])
            up_ref[:, cs] = u
            w, bias = convw_ref[:, cs], convb_ref[:, cs]
            segs = [_conv3(u[:N_META], zprev, w, bias)]
            for b in range(dec_b):
                r0 = N_META + b * t_dec
                segs.append(_conv3(u[r0:r0 + t_dec], sconv_ref[b, :, cs], w, bias))
            halves.append(jnp.concatenate(segs, axis=0))
        act = (_silu(halves[0]) * halves[1]).astype(BF16)
        acc = acc + _dot(act, w_down_ref[c0:c0 + SIDE_FF_CHUNK, :])
    y_ref[...] = _rmsnorm(x1 + acc, gfin_ref[...])
    upm_ref[...] = up_ref[N_META - SUBLANES:N_META]

    def store_o0(j, blk, value):
        o0_ref[j * CHUNK:(j + 1) * CHUNK, blk * LANES:(blk + 1) * LANES] = value

    groups, scores, finish = _attention_stages(0, q0_ref, k0_ref, k0_ref.at[0:WINDOW], v0_ref, v0_ref.at[0:WINDOW],
                                               k_ref.at[0:N_META], v_ref.at[0:N_META], kaug_ref, qaug_ref, store_o0)
    pending = scores(*groups[0])
    for g, group in enumerate(groups):
        current = pending
        if g + 1 < len(groups):
            pending = scores(*groups[g + 1])
        finish(*group, *current)


def _side_call(args, q, k, v, kaug, qaug):
    out_shape = (
        jax.ShapeDtypeStruct((N_SIDE, D_MODEL), F32),
        jax.ShapeDtypeStruct((N_SIDE, KV_W), F32),
        jax.ShapeDtypeStruct((N_SIDE, KV_W), F32),
        jax.ShapeDtypeStruct((N_SIDE, POOL_WIDTH), F32),
        jax.ShapeDtypeStruct((N_SIDE, 2 * D_FF), F32),
        jax.ShapeDtypeStruct((TQ, Q_W), BF16),
        jax.ShapeDtypeStruct((N_META, KV_W), F32),
        jax.ShapeDtypeStruct((N_META, KV_W), F32),
        jax.ShapeDtypeStruct((N_META, POOL_WIDTH), F32),
        jax.ShapeDtypeStruct((SUBLANES, 2 * D_FF), F32),
    )
    first_tile = lambda w: pl.BlockSpec((None, TQ, w), lambda i: (0, 0, 0))
    return pl.pallas_call(
        _side_kernel,
        grid=(1,),
        in_specs=[_const_spec(a.shape) for a in args] + [first_tile(Q_W), first_tile(KV_W), first_tile(KV_W),
                                                         _const_spec(kaug.shape), _const_spec(qaug.shape)],
        out_specs=tuple(_const_spec(s.shape) for s in out_shape),
        out_shape=out_shape,
        compiler_params=pltpu.CompilerParams(dimension_semantics=("arbitrary",), vmem_limit_bytes=VMEM_LIMIT),
        name="side_rows",
    )(*args, q, k, v, kaug, qaug)


TM_IN = 1024
IN_SUB = 256
GATE_CHUNK = 512


W_IN_CAST_CHUNK = 768


def _inproj_kernel(n_cast, x_ref, cos_ref, sin_ref, gmix_ref, w_in_f32_ref, bgate_ref, *refs):
    cast_in, (q_ref, k_ref, v_ref, p_ref, g_ref) = refs[:n_cast], refs[n_cast:n_cast + 5]
    cast_out, w_in_ref = refs[n_cast + 5:2 * n_cast + 5], refs[2 * n_cast + 5]

    @pl.when((pl.program_id(0) == 0) & (pl.program_id(1) == 0))
    def _():
        for c0 in range(0, IN_W, W_IN_CAST_CHUNK):
            w_in_ref[:, c0:c0 + W_IN_CAST_CHUNK] = w_in_f32_ref[:, c0:c0 + W_IN_CAST_CHUNK].astype(BF16)

    for src, dst in zip(cast_in, cast_out):
        dst[...] = src[...].astype(BF16)

    subs = [slice(r0, r0 + IN_SUB) for r0 in range(0, x_ref.shape[0], IN_SUB)]
    hs = [_rmsnorm(x_ref[rows], gmix_ref[...]).astype(BF16) for rows in subs]
    for rows, h in zip(subs, hs):
        cos, sin = cos_ref[rows], sin_ref[rows]
        for c0 in range(0, GATE_W, GATE_CHUNK):
            z = _dot(h, w_in_ref[:, G_OFF + c0:G_OFF + c0 + GATE_CHUNK]) + bgate_ref[:, c0:c0 + GATE_CHUNK]
            g_ref[rows, c0:c0 + GATE_CHUNK] = _sigmoid(z).astype(BF16)
        q_ref[rows] = _rope(_dot(h, w_in_ref[:, 0:Q_W]) * (HEAD_DIM ** -0.5 * LOG2E), cos, sin).astype(BF16)
        p_ref[rows] = _dot(h, w_in_ref[:, P_OFF:P_OFF + POOL_WIDTH])
        kv = _dot(h, w_in_ref[:, K_OFF:K_OFF + 2 * KV_W])
        k_ref[rows] = _rope(kv[:, :KV_W], cos, sin)
        v_ref[rows] = kv[:, KV_W:]


def _const_spec(shape):
    nd = len(shape)
    return pl.BlockSpec(shape, lambda *_: (0,) * nd)


BF16_SUBLANES = 16


def _inproj_call(x, cos, sin, gmix, w_in, bgate, other_weights):
    bsz, seq, _ = x.shape
    tm = TM_IN
    tiles = seq // tm
    n_steps = bsz * tiles
    row = lambda w: pl.BlockSpec((None, tm, w), lambda b, i: (b, i, 0))

    def slab_spec(w):
        n_slabs = n_steps
        while w.shape[0] % (n_slabs * BF16_SUBLANES):
            n_slabs //= 2
        return pl.BlockSpec((w.shape[0] // n_slabs, w.shape[1]),
                            lambda b, i: (jnp.minimum(b * tiles + i, n_slabs - 1), 0))

    slabs = [slab_spec(w) for w in other_weights]
    outs = pl.pallas_call(
        functools.partial(_inproj_kernel, len(other_weights)),
        grid=(bsz, tiles),
        in_specs=[row(D_MODEL), pl.BlockSpec((tm, LANES), lambda b, i: (i, 0)),
                  pl.BlockSpec((tm, LANES), lambda b, i: (i, 0)),
                  _const_spec(gmix.shape), _const_spec(w_in.shape), _const_spec(bgate.shape)] + slabs,
        out_specs=(row(Q_W), row(KV_W), row(KV_W), row(POOL_WIDTH), row(GATE_W), *slabs),
        out_shape=(jax.ShapeDtypeStruct((bsz, seq, Q_W), BF16), jax.ShapeDtypeStruct((bsz, seq, KV_W), F32),
                   jax.ShapeDtypeStruct((bsz, seq, KV_W), F32), jax.ShapeDtypeStruct((bsz, seq, POOL_WIDTH), F32),
                   jax.ShapeDtypeStruct((bsz, seq, GATE_W), BF16),
                   *[jax.ShapeDtypeStruct(w.shape, BF16) for w in other_weights]),
        scratch_shapes=[pltpu.VMEM(w_in.shape, BF16)],
        compiler_params=pltpu.CompilerParams(dimension_semantics=("arbitrary", "arbitrary"),
                                             vmem_limit_bytes=VMEM_LIMIT),
        name="in_proj",
    )(x, cos, sin, gmix, w_in, bgate, *other_weights)
    return outs[:5], outs[5:]


TQ = 512
BAND = WINDOW + CHUNK
KEYS_PAD = BAND + N_META + 16
N_TABS = WINDOW // CHUNK + 1
SUM_ROWS = 16


def _attention_stages(i, q_ref, kc_ref, kp_ref, vc_ref, vp_ref, mk_ref, mv_ref, kaug_ref, qaug_ref, store):
    kvar = _kv_variants(jnp.concatenate([kp_ref[...], kc_ref[...]], axis=0), BF16)
    vvar = _kv_variants(jnp.concatenate([vp_ref[...], vc_ref[...]], axis=0), F32)
    mkvar, mvvar = _kv_variants(mk_ref[...], BF16), _kv_variants(mv_ref[...], F32)
    kpad = jnp.zeros((KEYS_PAD - BAND - N_META, LANES), BF16)
    vpad = jnp.zeros((KEYS_PAD - BAND - N_META, LANES), F32)
    row = lax.broadcasted_iota(jnp.int32, (SUM_ROWS, 2 * KEYS_PAD), 0)
    col = lax.broadcasted_iota(jnp.int32, (SUM_ROWS, 2 * KEYS_PAD), 1)
    ones = jnp.where((row == 0) & (col < KEYS_PAD) | (row == 1) & (col >= KEYS_PAD), 1.0, 0.0).astype(BF16)

    def scores(j, n):
        tsel = jnp.minimum(i * (TQ // CHUNK) + j, N_TABS - 1)
        rows = slice(j * CHUNK, (j + 1) * CHUNK)
        band = slice(j * CHUNK, j * CHUNK + BAND)
        kst = jnp.concatenate([kvar[n][0][band], mkvar[n][0], kpad, kvar[n][1][band], mkvar[n][1], kpad], axis=0)
        vst = jnp.concatenate([vvar[n][0][band], mvvar[n][0], vpad, vvar[n][1][band], mvvar[n][1], vpad], axis=0)
        qg = jnp.concatenate([q_ref[rows, (BLOCKS_PER_KV * n + bi) * LANES:(BLOCKS_PER_KV * n + bi + 1) * LANES]
                              for bi in range(BLOCKS_PER_KV)], axis=0)
        s = lax.dot_general(jnp.concatenate([kst, kaug_ref[tsel]], axis=1),
                            jnp.concatenate([qg, qaug_ref[n]], axis=1),
                            (((1,), (1,)), ((), ())), preferred_element_type=F32)
        vt = jnp.concatenate([vst.T.astype(BF16), ones], axis=0)
        return s, vt

    def finish(j, n, s, vt):
        se, so = s[:KEYS_PAD], s[KEYS_PAD:]
        pe = jnp.exp2(se - jnp.max(se, axis=0, keepdims=True))
        po = jnp.exp2(so - jnp.max(so, axis=0, keepdims=True))
        o = _dot(vt, jnp.concatenate([pe, po], axis=0).astype(BF16))
        o = jnp.concatenate([o[:HEAD_DIM] / o[LANES:LANES + 1], o[HEAD_DIM:LANES] / o[LANES + 1:LANES + 2]], axis=0).T
        for bi in range(BLOCKS_PER_KV):
            store(j, BLOCKS_PER_KV * n + bi, o[bi * CHUNK:(bi + 1) * CHUNK].astype(BF16))

    groups = [(j, n) for j in range(TQ // CHUNK) for n in range(N_KV_HEADS)]
    return groups, scores, finish


def _attn_aug_tables(sinks):
    n_keys = BAND + N_META
    r = np.arange(KEYS_PAD)
    lane = np.arange(LANES)[None, :]
    kaug = []
    for c in range(N_TABS):
        masked = (r < (WINDOW // CHUNK - c) * CHUNK) | (r > n_keys)
        mask_col = np.where(masked, NEG, 0.0)[:, None] * (lane == 0)
        sink_row = (r == n_keys)[:, None]
        even = mask_col + np.where(sink_row & ((lane == 1) | (lane == 2)), 1.0, 0.0)
        odd = mask_col + np.where(sink_row & ((lane == 3) | (lane == 4)), 1.0, 0.0)
        kaug.append(np.concatenate([even, odd], axis=0))
    s2 = sinks.astype(F32) * LOG2E
    hi = s2.astype(BF16).astype(F32)
    lo = s2 - hi
    head = 2 * jnp.arange(Q_BLOCKS)
    cols = [jnp.ones((Q_BLOCKS,), F32), hi[head], lo[head], hi[head + 1], lo[head + 1]]
    qaug = jnp.stack(cols, axis=1)
    qaug = jnp.pad(qaug, ((0, 0), (0, LANES - qaug.shape[1])))
    qaug = jnp.broadcast_to(qaug[:, None, :], (Q_BLOCKS, CHUNK, LANES)).reshape(N_KV_HEADS, BLOCKS_PER_KV * CHUNK, LANES)
    return jnp.asarray(np.stack(kaug), dtype=BF16), qaug.astype(BF16)


TM_POST = 512
POST_SUB = 256
FF_CHUNK = 256
FF_SKEW = 3
FF_LANE_BLOCKS = FF_CHUNK // LANES
DOWN_GROUP = 4


def _pool_weight_kernel(wgrp_ref, pscale_ref, w_po_ref, out_ref):
    for g in range(N_POOL_GROUPS):
        sl = slice(g * POOL_GROUP_W, (g + 1) * POOL_GROUP_W)
        out_ref[sl, :] = _dot((wgrp_ref[g] * pscale_ref[:, sl]).astype(BF16), w_po_ref[sl, :]).astype(BF16)


def _pool_weight_call(wgrp, pscale, w_po):
    return pl.pallas_call(
        _pool_weight_kernel,
        out_shape=jax.ShapeDtypeStruct((POOL_WIDTH, D_MODEL), BF16),
        name="pool_weights",
    )(wgrp, pscale, w_po)


def _stream_kernel(tiles_per_batch, n_tiles,
                   q_ref, kc_ref, kp_ref, vc_ref, vp_ref, mk_ref, mv_ref, kaug_ref, qaug_ref,
                   x_ref, p_ref, pprev_ref, pmeta_ref, g_ref, upmeta_ref, w_ao_ref, w_pool_ref,
                   w_out_ref, gffn_ref, w_up_ref, convw_ref, convb_ref, w_down_ref, gfin_ref,
                   o0_ref, y_ref, carry_ref, oattn_ref, pext_ref, x1_ref, hn_ref, act_ref, *uext_refs):
    t = pl.program_id(0)
    i_attn = lax.rem(jnp.minimum(t + 1, n_tiles - 1), tiles_per_batch)
    i = lax.rem(t, tiles_per_batch)
    slot_new, slot_old = lax.rem(t + 1, 2), lax.rem(t, 2)
    tm = x_ref.shape[0]
    n_chunks = D_FF // FF_CHUNK

    @pl.when(t == 0)
    def _():
        oattn_ref[0] = o0_ref[...]

    @pl.when(i == 0)
    def _():
        pext_ref[0:N_META] = pmeta_ref[...]
        for c in range(n_chunks):
            for half, off in enumerate((c * FF_CHUNK, D_FF + c * FF_CHUNK)):
                for lb in range(FF_LANE_BLOCKS):
                    uext_refs[c][half * FF_LANE_BLOCKS + lb, POST_SUB:POST_SUB + SUBLANES] = (
                        upmeta_ref[:, off + lb * LANES:off + (lb + 1) * LANES])

    @pl.when(i > 0)
    def _():
        pext_ref[0:N_META] = pprev_ref[...]

    def store_attn(j, blk, value):
        oattn_ref[slot_new, j * CHUNK:(j + 1) * CHUNK, blk * LANES:(blk + 1) * LANES] = value

    groups, scores, finish = _attention_stages(i_attn, q_ref, kc_ref, kp_ref, vc_ref, vp_ref, mk_ref, mv_ref,
                                               kaug_ref, qaug_ref, store_attn)
    pending = [None]

    def attention_group(g):
        current = pending[0]
        if g + 1 < len(groups):
            pending[0] = scores(*groups[g + 1])
        finish(*groups[g], *current)

    pext_ref[N_META:] = p_ref[...]
    subs = [slice(r0, r0 + POST_SUB) for r0 in range(0, tm, POST_SUB)]

    mixes = []
    for rows in subs:
        attn = _dot(oattn_ref[slot_old, rows], w_ao_ref[...])
        ext_rows = slice(rows.start, rows.stop + N_META)
        pm = jnp.concatenate([_pool_group(pext_ref[ext_rows, g * POOL_GROUP_W:(g + 1) * POOL_GROUP_W], g)
                              for g in range(N_POOL_GROUPS)], axis=1)
        pool = _dot(pm.astype(BF16), w_pool_ref[...])
        mix = g_ref[rows, :D_MODEL].astype(F32) * attn + g_ref[rows, D_MODEL:].astype(F32) * pool
        mixes.append(mix.astype(BF16))
    for rows, mix in zip(subs, mixes):
        x1 = x_ref[rows] + _dot(mix, w_out_ref[...])
        x1_ref[rows] = x1
        hn_ref[rows] = _rmsnorm(x1, gffn_ref[...]).astype(BF16)

    def up_project(rows, c):
        for half, off in enumerate((c * FF_CHUNK, D_FF + c * FF_CHUNK)):
            u = _dot(hn_ref[rows], w_up_ref[:, off:off + FF_CHUNK])
            for lb in range(FF_LANE_BLOCKS):
                blk = uext_refs[c].at[half * FF_LANE_BLOCKS + lb]
                blk[0:SUBLANES] = blk[POST_SUB:POST_SUB + SUBLANES]
                blk[SUBLANES:] = u[:, lb * LANES:(lb + 1) * LANES]
            if rows.stop == tm:
                carry_ref[:, off:off + FF_CHUNK] = u[POST_SUB - SUBLANES:]

    def conv(rows, c, half):
        outs = []
        for lb in range(FF_LANE_BLOCKS):
            off = half * D_FF + c * FF_CHUNK + lb * LANES
            w, b = convw_ref[:, off:off + LANES], convb_ref[:, off:off + LANES]
            ext = uext_refs[c].at[half * FF_LANE_BLOCKS + lb]
            cv = b + ext[SUBLANES - 2:SUBLANES - 2 + POST_SUB] * w[0:1]
            cv = cv + ext[SUBLANES - 1:SUBLANES - 1 + POST_SUB] * w[1:2]
            outs.append(cv + ext[SUBLANES:] * w[2:3])
        return jnp.concatenate(outs, axis=1)

    steps = [(rows, c) for rows in subs for c in range(n_chunks)]
    for step in steps[:FF_SKEW]:
        up_project(*step)
    pending[0] = scores(*groups[0])
    acc = None
    for j, (rows, c) in enumerate(steps):
        act_ref[rows, c * FF_CHUNK:(c + 1) * FF_CHUNK] = (_silu(conv(rows, c, 0)) * conv(rows, c, 1)).astype(BF16)
        if j + FF_SKEW < len(steps):
            up_project(*steps[j + FF_SKEW])
        if (c + 1) % DOWN_GROUP == 0 or c == n_chunks - 1:
            ks = slice((c // DOWN_GROUP) * DOWN_GROUP * FF_CHUNK, (c + 1) * FF_CHUNK)
            d = _dot(act_ref[rows, ks], w_down_ref[ks, :])
            acc = d if c < DOWN_GROUP else acc + d
        if c == n_chunks - 1:
            y_ref[rows] = _rmsnorm(x1_ref[rows] + acc, gfin_ref[...])
        if j < len(groups):
            attention_group(j)
    assert len(groups) <= len(steps)


def _stream_call(q, k, v, mk, mv, kaug, qaug, x, p, p_meta, gates, o0, up_meta8, w_ao, w_pool, w_out, gffn, w_up,
                 convw, convb, w_down, gfin):
    bsz, seq, _ = x.shape
    tm = TM_POST
    assert tm == TQ
    tiles_per_batch = seq // tm
    n_tiles = bsz * tiles_per_batch

    def new_tile(t):
        a = jnp.minimum(t + 1, n_tiles - 1)
        return a // tiles_per_batch, lax.rem(a, tiles_per_batch)

    def old_tile(t):
        return t // tiles_per_batch, lax.rem(t, tiles_per_batch)

    new_row = lambda w: pl.BlockSpec((None, tm, w), lambda t: (*new_tile(t), 0))
    old_row = lambda w: pl.BlockSpec((None, tm, w), lambda t: (*old_tile(t), 0))

    def kv_prev_map(t):
        b, i = new_tile(t)
        return b, jnp.maximum(i * (tm // WINDOW) - 1, 0), 0

    def p_prev_map(t):
        b, i = old_tile(t)
        return b, jnp.maximum(i * (tm // N_META) - 1, 0), 0

    kv_prev = pl.BlockSpec((None, WINDOW, KV_W), kv_prev_map)
    consts = (up_meta8, w_ao, w_pool, w_out, gffn, w_up, convw, convb, w_down, gfin, o0)
    return pl.pallas_call(
        functools.partial(_stream_kernel, tiles_per_batch, n_tiles),
        grid=(n_tiles,),
        in_specs=[new_row(Q_W), new_row(KV_W), kv_prev, new_row(KV_W), kv_prev, _const_spec(mk.shape),
                  _const_spec(mv.shape), _const_spec(kaug.shape), _const_spec(qaug.shape),
                  old_row(D_MODEL), old_row(POOL_WIDTH), pl.BlockSpec((None, N_META, POOL_WIDTH), p_prev_map),
                  _const_spec(p_meta.shape), old_row(GATE_W)] + [_const_spec(a.shape) for a in consts],
        out_specs=(old_row(D_MODEL), pl.BlockSpec((None, SUBLANES, 2 * D_FF), lambda t: (old_tile(t)[0], 0, 0))),
        out_shape=(jax.ShapeDtypeStruct((bsz, seq, D_MODEL), F32),
                   jax.ShapeDtypeStruct((bsz, SUBLANES, 2 * D_FF), F32)),
        scratch_shapes=[pltpu.VMEM((2, tm, Q_W), BF16), pltpu.VMEM((N_META + tm, POOL_WIDTH), F32),
                        pltpu.VMEM((tm, D_MODEL), F32), pltpu.VMEM((tm, D_MODEL), BF16),
                        pltpu.VMEM((tm, D_FF), BF16)]
                       + [pltpu.VMEM((2 * FF_LANE_BLOCKS, SUBLANES + POST_SUB, LANES), F32)
                          for _ in range(D_FF // FF_CHUNK)],
        compiler_params=pltpu.CompilerParams(dimension_semantics=("arbitrary",), vmem_limit_bytes=VMEM_LIMIT),
        name="attn_mixer_ffn",
    )(q, k, k, v, v, mk, mv, kaug, qaug, x, p, p, p_meta, gates, *consts)


def _rope_tables(pos):
    half = HEAD_DIM // 2
    inv = ROPE_THETA ** (-np.arange(half, dtype=np.float64) / half)
    ang = np.asarray(pos, dtype=np.float64)[:, None] * inv[None, :]
    cos, sin = np.cos(ang), np.sin(ang)
    cos = np.tile(cos, (1, LANES // half))
    sin = np.tile(np.concatenate([-sin, sin], axis=1), (1, LANES // HEAD_DIM))
    return jnp.asarray(cos, dtype=F32), jnp.asarray(sin, dtype=F32)


def _score_table(sinks, rows_per_block, n_keys, keys_pad, masked_prefix=0):
    col = jnp.arange(keys_pad)
    base = jnp.where((col >= masked_prefix) & (col < n_keys), 0.0, NEG).astype(F32)
    head = (2 * jnp.arange(Q_BLOCKS)[:, None] + jnp.arange(2)[None, :])
    tab = jnp.where(col[None, None, :] == n_keys, sinks.astype(F32)[head][:, :, None], base[None, None, :])
    tab = tab.reshape(N_KV_HEADS, BLOCKS_PER_KV, 1, 2 * keys_pad)
    tab = jnp.broadcast_to(tab, (N_KV_HEADS, BLOCKS_PER_KV, rows_per_block, 2 * keys_pad))
    return tab.reshape(N_KV_HEADS, BLOCKS_PER_KV * rows_per_block, 2 * keys_pad)


def kernel(x_prompt, x_sample, cache_swa_k, cache_swa_v, cache_meta_k, cache_meta_v, state_pool, state_conv,
           meta_tokens, g_norm_mix, w_in, b_gate, sinks, w_attn_o, w_pool_grp, pool_scale, w_pool_o, w_out,
           g_norm_ffn, w_up, conv_w, conv_b, w_down, g_norm_final):
    bsz, seq, _ = x_prompt.shape
    dbsz, t_dec, _ = x_sample.shape
    row2 = lambda a: a.reshape(1, -1)
    gmix, gffn, gfin = row2(g_norm_mix), row2(g_norm_ffn), row2(g_norm_final)
    bgate, pscale, convb = row2(b_gate), row2(pool_scale), row2(conv_b)

    cos_p, sin_p = _rope_tables(N_META + np.arange(seq))
    (q, k, v, p, gates), (w_ao_b, w_po_b, w_out_b, w_up_b, w_down_b) = _inproj_call(
        x_prompt, cos_p, sin_p, gmix, w_in, bgate, (w_attn_o, w_pool_o, w_out, w_up, w_down))

    pos_side = np.concatenate([np.arange(N_META), np.tile(N_META + PAST_LEN + np.arange(t_dec), dbsz)])
    cos_s, sin_s = _rope_tables(pos_side)
    xs = jnp.concatenate([meta_tokens, x_sample.reshape(dbsz * t_dec, D_MODEL)], axis=0)
    spool16 = jnp.pad(state_pool, ((0, 0), (N_META - POOL_HIST, 0), (0, 0)))
    sconv8 = jnp.pad(state_conv, ((0, 0), (SUBLANES - (CONV_W - 1), 0), (0, 0)))
    tab_meta = _score_table(sinks, N_META, N_META, LANES)
    tab_dec = _score_table(sinks, t_dec, N_META + WINDOW + t_dec, 2 * LANES)
    kaug, qaug = _attn_aug_tables(sinks)
    y_side, k_side, v_side, p_side, up_side, o_first, km, vm, p_meta, up_meta8 = _side_call(
        (xs, cache_swa_k.reshape(dbsz, WINDOW, KV_W), cache_swa_v.reshape(dbsz, WINDOW, KV_W),
         cache_meta_k.reshape(dbsz, N_META, KV_W), cache_meta_v.reshape(dbsz, N_META, KV_W), spool16, sconv8,
         cos_s, sin_s, tab_meta, tab_dec, gmix, w_in, bgate, w_ao_b, w_pool_grp, pscale, w_po_b, w_out_b,
         gffn, w_up_b, conv_w, convb, w_down_b, gfin), q, k, v, kaug, qaug)
    w_pool_b = _pool_weight_call(w_pool_grp, pscale, w_po_b)
    y_prompt, up_tail = _stream_call(q, k, v, km, vm, kaug, qaug, x_prompt, p, p_meta, gates, o_first, up_meta8,
                                     w_ao_b, w_pool_b, w_out_b, gffn, w_up_b, conv_w, convb, w_down_b, gfin)

    kv4 = lambda a, n: a.reshape(a.shape[0], n, N_KV_HEADS, HEAD_DIM)
    dec = lambda a: a[N_META:].reshape(dbsz, t_dec, -1)
    return (
        y_prompt,
        dec(y_side),
        kv4(k[:, seq - WINDOW:], WINDOW),
        kv4(v[:, seq - WINDOW:], WINDOW),
        jnp.broadcast_to(km.reshape(1, N_META, N_KV_HEADS, HEAD_DIM), (bsz, N_META, N_KV_HEADS, HEAD_DIM)),
        jnp.broadcast_to(vm.reshape(1, N_META, N_KV_HEADS, HEAD_DIM), (bsz, N_META, N_KV_HEADS, HEAD_DIM)),
        p[:, seq - POOL_HIST:],
        up_tail[:, SUBLANES - (CONV_W - 1):],
        kv4(dec(k_side), t_dec),
        kv4(dec(v_side), t_dec),
        dec(p_side)[:, t_dec - POOL_HIST:],
        dec(up_side)[:, t_dec - (CONV_W - 1):],
    )
```

```python
import functools

import numpy as np
import jax
import jax.numpy as jnp
from jax import lax
from jax.experimental import pallas as pl
from jax.experimental.pallas import tpu as pltpu

D_MODEL = 1024
N_META = 16
CHUNK = 64
HEAD_DIM = 64
N_Q_HEADS = 16
N_KV_HEADS = 2
WINDOW = 128
ROPE_THETA = 10000.0
POOL_WINDOWS = (2, 4, 8, 16)
N_POOL_GROUPS = 4
POOL_WIDTH = D_MODEL // 2
POOL_GROUP_W = POOL_WIDTH // N_POOL_GROUPS
POOL_HIST = max(POOL_WINDOWS) - 1
Q_W = N_Q_HEADS * HEAD_DIM
KV_W = N_KV_HEADS * HEAD_DIM
GATE_W = 2 * D_MODEL
IN_W = Q_W + 2 * KV_W + POOL_WIDTH + GATE_W
D_FF = ((8 * D_MODEL // 3) + 127) // 128 * 128
CONV_W = 3
RMS_EPS = 1e-6
PAST_LEN = 1024

LANES = 128
SUBLANES = 8
Q_BLOCKS = Q_W // LANES
BLOCKS_PER_KV = Q_BLOCKS // N_KV_HEADS
NEG = -1e30
LOG2E = 1.4426950408889634

K_OFF = Q_W
V_OFF = Q_W + KV_W
P_OFF = Q_W + 2 * KV_W
G_OFF = P_OFF + POOL_WIDTH

F32 = jnp.float32
BF16 = jnp.bfloat16

VMEM_LIMIT = 60 * 1024 * 1024


def _rmsnorm(x, g):
    ms = jnp.mean(x * x, axis=-1, keepdims=True)
    return (x * lax.rsqrt(ms + RMS_EPS)) * g


def _dot(a, b):
    return jnp.dot(a, b, preferred_element_type=F32)


def _rope(x, cos, sin_signed):
    lane = lax.broadcasted_iota(jnp.int32, (x.shape[0], LANES), 1)
    first_half = (lane & (HEAD_DIM // 2)) == 0
    outs = []
    for b in range(x.shape[1] // LANES):
        xb = x[:, b * LANES:(b + 1) * LANES]
        partner = jnp.where(first_half, pltpu.roll(xb, LANES - HEAD_DIM // 2, 1), pltpu.roll(xb, HEAD_DIM // 2, 1))
        outs.append(xb * cos + partner * sin_signed)
    return outs[0] if len(outs) == 1 else jnp.concatenate(outs, axis=1)


def _kv_variants(kv, dtype=BF16):
    lane = lax.broadcasted_iota(jnp.int32, kv.shape, 1)
    low = lane < HEAD_DIM
    swapped = pltpu.roll(kv, HEAD_DIM, 1)
    zero = jnp.zeros_like(kv)
    return (
        (jnp.where(low, kv, zero).astype(dtype), jnp.where(low, zero, swapped).astype(dtype)),
        (jnp.where(low, swapped, zero).astype(dtype), jnp.where(low, zero, kv).astype(dtype)),
    )


def _attend_group(qg, kst, vst, tab):
    s = lax.dot_general(qg, kst, (((1,), (1,)), ((), ())), preferred_element_type=F32) + tab
    npad = kst.shape[0] // 2
    se, so = s[:, :npad], s[:, npad:]
    pe = jnp.exp(se - jnp.max(se, axis=1, keepdims=True))
    po = jnp.exp(so - jnp.max(so, axis=1, keepdims=True))
    le = jnp.sum(pe, axis=1, keepdims=True)
    lo = jnp.sum(po, axis=1, keepdims=True)
    p = jnp.concatenate([pe, po], axis=1).astype(BF16)
    o = _dot(p, vst)
    lane = lax.broadcasted_iota(jnp.int32, o.shape, 1)
    return o * jnp.where(lane < HEAD_DIM, 1.0 / le, 1.0 / lo)


def _pool_group(ext, g, cnt=None):
    w = POOL_WINDOWS[g]
    s = ext
    k = 1
    while k < w:
        s = s + pltpu.roll(s, k, 0)
        k *= 2
    cur = ext[N_META:]
    return s[N_META:] / (float(w) if cnt is None else cnt) - cur


def _pool_project(pm, wgrp_ref, pscale):
    outs = []
    for g in range(N_POOL_GROUPS):
        sl = slice(g * POOL_GROUP_W, (g + 1) * POOL_GROUP_W)
        outs.append(_dot(pm[:, sl].astype(BF16), wgrp_ref[g].astype(BF16)))
    return (jnp.concatenate(outs, axis=1) * pscale).astype(BF16)


def _conv3(u, prev8, w, b):
    t = u.shape[0]
    ext = jnp.concatenate([prev8, u], axis=0)
    u1 = pltpu.roll(ext, 1, 0)[SUBLANES:SUBLANES + t]
    u2 = pltpu.roll(ext, 2, 0)[SUBLANES:SUBLANES + t]
    c = b + u2 * w[0:1]
    c = c + u1 * w[1:2]
    return c + u * w[2:3]


def _sigmoid(x):
    return 0.5 * jnp.tanh(0.5 * x) + 0.5


def _silu(x):
    half = 0.5 * x
    return half + half * jnp.tanh(half)


SIDE_FF_CHUNK = 256


def _side_kernel(xs_ref, ck_ref, cv_ref, cmk_ref, cmv_ref, spool_ref, sconv_ref, cos_ref, sin_ref, tabm_ref,
                 tabs_ref, gmix_ref, w_in_ref, bgate_ref, w_ao_ref, wgrp_ref, pscale_ref, w_po_ref, w_out_ref,
                 gffn_ref, w_up_ref, convw_ref, convb_ref, w_down_ref, gfin_ref,
                 q0_ref, k0_ref, v0_ref, kaug_ref, qaug_ref,
                 y_ref, k_ref, v_ref, p_ref, up_ref, o0_ref, km_ref, vm_ref, pm_ref, upm_ref):
    n_side = xs_ref.shape[0]
    dec_b = ck_ref.shape[0]
    t_dec = (n_side - N_META) // dec_b
    x = xs_ref[...]
    h = _rmsnorm(x, gmix_ref[...]).astype(BF16)
    cos, sin = cos_ref[...], sin_ref[...]
    w_in = lambda c0, c1: w_in_ref[:, c0:c1].astype(BF16)
    q = _rope(_dot(h, w_in(0, Q_W)) * (HEAD_DIM ** -0.5), cos, sin).astype(BF16)
    k = _rope(_dot(h, w_in(K_OFF, K_OFF + KV_W)), cos, sin)
    v = _dot(h, w_in(V_OFF, V_OFF + KV_W))
    p = _dot(h, w_in(P_OFF, P_OFF + POOL_WIDTH))
    gates = _sigmoid(jnp.concatenate([_dot(h, w_in(G_OFF + c0, G_OFF + c0 + GATE_CHUNK))
                                      for c0 in range(0, GATE_W, GATE_CHUNK)], axis=1) + bgate_ref[...])
    k_ref[...] = k
    v_ref[...] = v
    p_ref[...] = p
    km_ref[...] = k[:N_META]
    vm_ref[...] = v[:N_META]
    pm_ref[...] = p[:N_META]

    def attend_rows(r0, nrows, kx, vx, tab_ref):
        kvar, vvar = _kv_variants(kx), _kv_variants(vx)
        blocks = [None] * Q_BLOCKS
        for n in range(N_KV_HEADS):
            qg = jnp.concatenate([q[r0:r0 + nrows, (BLOCKS_PER_KV * n + bi) * LANES:(BLOCKS_PER_KV * n + bi + 1) * LANES]
                                  for bi in range(BLOCKS_PER_KV)], axis=0)
            kst = jnp.concatenate(kvar[n], axis=0)
            vst = jnp.concatenate(vvar[n], axis=0)
            o = _attend_group(qg, kst, vst, tab_ref[n])
            for bi in range(BLOCKS_PER_KV):
                blocks[BLOCKS_PER_KV * n + bi] = o[bi * nrows:(bi + 1) * nrows]
        return jnp.concatenate(blocks, axis=1)

    zpad = jnp.zeros((LANES - N_META, LANES), F32)
    o_rows = [attend_rows(0, N_META, jnp.concatenate([k[:N_META], zpad], axis=0),
                          jnp.concatenate([v[:N_META], zpad], axis=0), tabm_ref)]
    n_keys = N_META + WINDOW + t_dec
    zpad = jnp.zeros((2 * LANES - n_keys, LANES), F32)
    for b in range(dec_b):
        r0 = N_META + b * t_dec
        kx = jnp.concatenate([cmk_ref[b], ck_ref[b], k[r0:r0 + t_dec], zpad], axis=0)
        vx = jnp.concatenate([cmv_ref[b], cv_ref[b], v[r0:r0 + t_dec], zpad], axis=0)
        o_rows.append(attend_rows(r0, t_dec, kx, vx, tabs_ref))
    o_attn = jnp.concatenate(o_rows, axis=0).astype(BF16)

    row = lax.broadcasted_iota(jnp.int32, (N_META, LANES), 0)
    pm_rows = []
    for seg in range(1 + dec_b):
        if seg == 0:
            ext = jnp.concatenate([jnp.zeros((N_META, POOL_WIDTH), F32), p[:N_META]], axis=0)
        else:
            r0 = N_META + (seg - 1) * t_dec
            ext = jnp.concatenate([spool_ref[seg - 1], p[r0:r0 + t_dec]], axis=0)
        groups = []
        for g in range(N_POOL_GROUPS):
            cnt = jnp.minimum(POOL_WINDOWS[g], row + 1).astype(F32) if seg == 0 else None
            groups.append(_pool_group(ext[:, g * POOL_GROUP_W:(g + 1) * POOL_GROUP_W], g, cnt))
        pm_rows.append(jnp.concatenate(groups, axis=1))
    pm = jnp.concatenate(pm_rows, axis=0)

    pool = _pool_project(pm, wgrp_ref, pscale_ref[...])
    mix = gates[:, :D_MODEL] * _dot(o_attn, w_ao_ref[...]) + gates[:, D_MODEL:] * _dot(pool, w_po_ref[...])
    x1 = x + _dot(mix.astype(BF16), w_out_ref[...])

    hn = _rmsnorm(x1, gffn_ref[...]).astype(BF16)
    acc = jnp.zeros((n_side, D_MODEL), F32)
    zprev = jnp.zeros((SUBLANES, SIDE_FF_CHUNK), F32)
    for c0 in range(0, D_FF, SIDE_FF_CHUNK):
        halves = []
        for off in (c0, D_FF + c0):
            cs = slice(off, off + SIDE_FF_CHUNK)
            u = _dot(hn, w_up_ref[:, cs])
            up_ref[:, cs] = u
            w, bias = convw_ref[:, cs], convb_ref[:, cs]
            segs = [_conv3(u[:N_META], zprev, w, bias)]
            for b in range(dec_b):
                r0 = N_META + b * t_dec
                segs.append(_conv3(u[r0:r0 + t_dec], sconv_ref[b, :, cs], w, bias))
            halves.append(jnp.concatenate(segs, axis=0))
        act = (_silu(halves[0]) * halves[1]).astype(BF16)
        acc = acc + _dot(act, w_down_ref[c0:c0 + SIDE_FF_CHUNK, :])
    y_ref[...] = _rmsnorm(x1 + acc, gfin_ref[...])
    upm_ref[...] = up_ref[N_META - SUBLANES:N_META]

    def store_o0(j, blk, value):
        o0_ref[j * CHUNK:(j + 1) * CHUNK, blk * LANES:(blk + 1) * LANES] = value

    groups, scores, finish = _attention_stages(0, q0_ref, k0_ref, k0_ref.at[0:WINDOW], v0_ref, v0_ref.at[0:WINDOW],
                                               k_ref.at[0:N_META], v_ref.at[0:N_META], kaug_ref, qaug_ref, store_o0)
    pending = scores(*groups[0])
    for g, group in enumerate(groups):
        current = pending
        if g + 1 < len(groups):
            pending = scores(*groups[g + 1])
        finish(*group, *current)


def _side_call(args, q, k, v, kaug, qaug):
    n_side = args[0].shape[0]
    out_shape = (
        jax.ShapeDtypeStruct((n_side, D_MODEL), F32),
        jax.ShapeDtypeStruct((n_side, KV_W), F32),
        jax.ShapeDtypeStruct((n_side, KV_W), F32),
        jax.ShapeDtypeStruct((n_side, POOL_WIDTH), F32),
        jax.ShapeDtypeStruct((n_side, 2 * D_FF), F32),
        jax.ShapeDtypeStruct((TQ, Q_W), BF16),
        jax.ShapeDtypeStruct((N_META, KV_W), F32),
        jax.ShapeDtypeStruct((N_META, KV_W), F32),
        jax.ShapeDtypeStruct((N_META, POOL_WIDTH), F32),
        jax.ShapeDtypeStruct((SUBLANES, 2 * D_FF), F32),
    )
    first_tile = lambda w: pl.BlockSpec((None, TQ, w), lambda i: (0, 0, 0))
    return pl.pallas_call(
        _side_kernel,
        grid=(1,),
        in_specs=[_const_spec(a.shape) for a in args] + [first_tile(Q_W), first_tile(KV_W), first_tile(KV_W),
                                                         _const_spec(kaug.shape), _const_spec(qaug.shape)],
        out_specs=tuple(_const_spec(s.shape) for s in out_shape),
        out_shape=out_shape,
        compiler_params=pltpu.CompilerParams(dimension_semantics=("arbitrary",), vmem_limit_bytes=VMEM_LIMIT),
        name="side_rows",
    )(*args, q, k, v, kaug, qaug)


TM_IN = 1024
IN_SUB = 256
GATE_CHUNK = 512


W_IN_CAST_CHUNK = 768


def _inproj_kernel(n_cast, x_ref, cos_ref, sin_ref, gmix_ref, w_in_f32_ref, bgate_ref, *refs):
    cast_in, (q_ref, k_ref, v_ref, p_ref, g_ref) = refs[:n_cast], refs[n_cast:n_cast + 5]
    cast_out, w_in_ref = refs[n_cast + 5:2 * n_cast + 5], refs[2 * n_cast + 5]

    @pl.when((pl.program_id(0) == 0) & (pl.program_id(1) == 0))
    def _():
        for c0 in range(0, IN_W, W_IN_CAST_CHUNK):
            w_in_ref[:, c0:c0 + W_IN_CAST_CHUNK] = w_in_f32_ref[:, c0:c0 + W_IN_CAST_CHUNK].astype(BF16)

    for src, dst in zip(cast_in, cast_out):
        dst[...] = src[...].astype(BF16)

    subs = [slice(r0, r0 + IN_SUB) for r0 in range(0, x_ref.shape[0], IN_SUB)]
    hs = [_rmsnorm(x_ref[rows], gmix_ref[...]).astype(BF16) for rows in subs]
    for rows, h in zip(subs, hs):
        cos, sin = cos_ref[rows], sin_ref[rows]
        for c0 in range(0, GATE_W, GATE_CHUNK):
            z = _dot(h, w_in_ref[:, G_OFF + c0:G_OFF + c0 + GATE_CHUNK]) + bgate_ref[:, c0:c0 + GATE_CHUNK]
            g_ref[rows, c0:c0 + GATE_CHUNK] = _sigmoid(z).astype(BF16)
        q_ref[rows] = _rope(_dot(h, w_in_ref[:, 0:Q_W]) * (HEAD_DIM ** -0.5 * LOG2E), cos, sin).astype(BF16)
        p_ref[rows] = _dot(h, w_in_ref[:, P_OFF:P_OFF + POOL_WIDTH])
        kv = _dot(h, w_in_ref[:, K_OFF:K_OFF + 2 * KV_W])
        k_ref[rows] = _rope(kv[:, :KV_W], cos, sin)
        v_ref[rows] = kv[:, KV_W:]


def _const_spec(shape):
    nd = len(shape)
    return pl.BlockSpec(shape, lambda *_: (0,) * nd)


BF16_SUBLANES = 16


def _inproj_call(x, cos, sin, gmix, w_in, bgate, other_weights):
    bsz, seq, _ = x.shape
    tm = TM_IN
    tiles = seq // tm
    n_steps = bsz * tiles
    row = lambda w: pl.BlockSpec((None, tm, w), lambda b, i: (b, i, 0))

    def slab_spec(w):
        n_slabs = n_steps
        while w.shape[0] % (n_slabs * BF16_SUBLANES):
            n_slabs //= 2
        return pl.BlockSpec((w.shape[0] // n_slabs, w.shape[1]),
                            lambda b, i: (jnp.minimum(b * tiles + i, n_slabs - 1), 0))

    slabs = [slab_spec(w) for w in other_weights]
    outs = pl.pallas_call(
        functools.partial(_inproj_kernel, len(other_weights)),
        grid=(bsz, tiles),
        in_specs=[row(D_MODEL), pl.BlockSpec((tm, LANES), lambda b, i: (i, 0)),
                  pl.BlockSpec((tm, LANES), lambda b, i: (i, 0)),
                  _const_spec(gmix.shape), _const_spec(w_in.shape), _const_spec(bgate.shape)] + slabs,
        out_specs=(row(Q_W), row(KV_W), row(KV_W), row(POOL_WIDTH), row(GATE_W), *slabs),
        out_shape=(jax.ShapeDtypeStruct((bsz, seq, Q_W), BF16), jax.ShapeDtypeStruct((bsz, seq, KV_W), F32),
                   jax.ShapeDtypeStruct((bsz, seq, KV_W), F32), jax.ShapeDtypeStruct((bsz, seq, POOL_WIDTH), F32),
                   jax.ShapeDtypeStruct((bsz, seq, GATE_W), BF16),
                   *[jax.ShapeDtypeStruct(w.shape, BF16) for w in other_weights]),
        scratch_shapes=[pltpu.VMEM(w_in.shape, BF16)],
        compiler_params=pltpu.CompilerParams(dimension_semantics=("arbitrary", "arbitrary"),
                                             vmem_limit_bytes=VMEM_LIMIT),
        name="in_proj",
    )(x, cos, sin, gmix, w_in, bgate, *other_weights)
    return outs[:5], outs[5:]


TQ = 512
BAND = WINDOW + CHUNK
KEYS_PAD = BAND + N_META + 16
N_TABS = WINDOW // CHUNK + 1
SUM_ROWS = 16


def _attention_stages(i, q_ref, kc_ref, kp_ref, vc_ref, vp_ref, mk_ref, mv_ref, kaug_ref, qaug_ref, store):
    kvar = _kv_variants(jnp.concatenate([kp_ref[...], kc_ref[...]], axis=0), BF16)
    vvar = _kv_variants(jnp.concatenate([vp_ref[...], vc_ref[...]], axis=0), F32)
    mkvar, mvvar = _kv_variants(mk_ref[...], BF16), _kv_variants(mv_ref[...], F32)
    kpad = jnp.zeros((KEYS_PAD - BAND - N_META, LANES), BF16)
    vpad = jnp.zeros((KEYS_PAD - BAND - N_META, LANES), F32)
    row = lax.broadcasted_iota(jnp.int32, (SUM_ROWS, 2 * KEYS_PAD), 0)
    col = lax.broadcasted_iota(jnp.int32, (SUM_ROWS, 2 * KEYS_PAD), 1)
    ones = jnp.where((row == 0) & (col < KEYS_PAD) | (row == 1) & (col >= KEYS_PAD), 1.0, 0.0).astype(BF16)

    def scores(j, n):
        tsel = jnp.minimum(i * (TQ // CHUNK) + j, N_TABS - 1)
        rows = slice(j * CHUNK, (j + 1) * CHUNK)
        band = slice(j * CHUNK, j * CHUNK + BAND)
        kst = jnp.concatenate([kvar[n][0][band], mkvar[n][0], kpad, kvar[n][1][band], mkvar[n][1], kpad], axis=0)
        vst = jnp.concatenate([vvar[n][0][band], mvvar[n][0], vpad, vvar[n][1][band], mvvar[n][1], vpad], axis=0)
        qg = jnp.concatenate([q_ref[rows, (BLOCKS_PER_KV * n + bi) * LANES:(BLOCKS_PER_KV * n + bi + 1) * LANES]
                              for bi in range(BLOCKS_PER_KV)], axis=0)
        s = lax.dot_general(jnp.concatenate([kst, kaug_ref[tsel]], axis=1),
                            jnp.concatenate([qg, qaug_ref[n]], axis=1),
                            (((1,), (1,)), ((), ())), preferred_element_type=F32)
        vt = jnp.concatenate([vst.T.astype(BF16), ones], axis=0)
        return s, vt

    def finish(j, n, s, vt):
        se, so = s[:KEYS_PAD], s[KEYS_PAD:]
        pe = jnp.exp2(se - jnp.max(se, axis=0, keepdims=True))
        po = jnp.exp2(so - jnp.max(so, axis=0, keepdims=True))
        o = _dot(vt, jnp.concatenate([pe, po], axis=0).astype(BF16))
        o = jnp.concatenate([o[:HEAD_DIM] / o[LANES:LANES + 1], o[HEAD_DIM:LANES] / o[LANES + 1:LANES + 2]], axis=0).T
        for bi in range(BLOCKS_PER_KV):
            store(j, BLOCKS_PER_KV * n + bi, o[bi * CHUNK:(bi + 1) * CHUNK].astype(BF16))

    groups = [(j, n) for j in range(TQ // CHUNK) for n in range(N_KV_HEADS)]
    return groups, scores, finish


def _attn_aug_tables(sinks):
    n_keys = BAND + N_META
    r = np.arange(KEYS_PAD)
    lane = np.arange(LANES)[None, :]
    kaug = []
    for c in range(N_TABS):
        masked = (r < (WINDOW // CHUNK - c) * CHUNK) | (r > n_keys)
        mask_col = np.where(masked, NEG, 0.0)[:, None] * (lane == 0)
        sink_row = (r == n_keys)[:, None]
        even = mask_col + np.where(sink_row & ((lane == 1) | (lane == 2)), 1.0, 0.0)
        odd = mask_col + np.where(sink_row & ((lane == 3) | (lane == 4)), 1.0, 0.0)
        kaug.append(np.concatenate([even, odd], axis=0))
    s2 = sinks.astype(F32) * LOG2E
    hi = s2.astype(BF16).astype(F32)
    lo = s2 - hi
    head = 2 * jnp.arange(Q_BLOCKS)
    cols = [jnp.ones((Q_BLOCKS,), F32), hi[head], lo[head], hi[head + 1], lo[head + 1]]
    qaug = jnp.stack(cols, axis=1)
    qaug = jnp.pad(qaug, ((0, 0), (0, LANES - qaug.shape[1])))
    qaug = jnp.broadcast_to(qaug[:, None, :], (Q_BLOCKS, CHUNK, LANES)).reshape(N_KV_HEADS, BLOCKS_PER_KV * CHUNK, LANES)
    return jnp.asarray(np.stack(kaug), dtype=BF16), qaug.astype(BF16)


TM_POST = 512
POST_SUB = 256
FF_CHUNK = 256
FF_SKEW = 3
FF_LANE_BLOCKS = FF_CHUNK // LANES
DOWN_GROUP_ENDS = (4, 8, 10, 11)
assert DOWN_GROUP_ENDS[-1] == D_FF // FF_CHUNK


def _pool_weight_kernel(wgrp_ref, pscale_ref, w_po_ref, out_ref):
    for g in range(N_POOL_GROUPS):
        sl = slice(g * POOL_GROUP_W, (g + 1) * POOL_GROUP_W)
        out_ref[sl, :] = _dot((wgrp_ref[g] * pscale_ref[:, sl]).astype(BF16), w_po_ref[sl, :]).astype(BF16)


def _pool_weight_call(wgrp, pscale, w_po):
    return pl.pallas_call(
        _pool_weight_kernel,
        out_shape=jax.ShapeDtypeStruct((POOL_WIDTH, D_MODEL), BF16),
        name="pool_weights",
    )(wgrp, pscale, w_po)


def _stream_kernel(tiles_per_batch, n_tiles,
                   q_ref, kc_ref, kp_ref, vc_ref, vp_ref, mk_ref, mv_ref, kaug_ref, qaug_ref,
                   x_ref, p_ref, pprev_ref, pmeta_ref, g_ref, upmeta_ref, w_ao_ref, w_pool_ref,
                   w_out_ref, gffn_ref, w_up_ref, convw_ref, convb_ref, w_down_ref, gfin_ref,
                   o0_ref, y_ref, carry_ref, oattn_ref, pext_ref, x1_ref, hn_ref, act_ref, *uext_refs):
    t = pl.program_id(0)
    i_attn = lax.rem(jnp.minimum(t + 1, n_tiles - 1), tiles_per_batch)
    i = lax.rem(t, tiles_per_batch)
    slot_new, slot_old = lax.rem(t + 1, 2), lax.rem(t, 2)
    tm = x_ref.shape[0]
    n_chunks = D_FF // FF_CHUNK

    @pl.when(t == 0)
    def _():
        oattn_ref[0] = o0_ref[...]

    @pl.when(i == 0)
    def _():
        pext_ref[0:N_META] = pmeta_ref[...]
        for c in range(n_chunks):
            for half, off in enumerate((c * FF_CHUNK, D_FF + c * FF_CHUNK)):
                for lb in range(FF_LANE_BLOCKS):
                    uext_refs[c][half * FF_LANE_BLOCKS + lb, POST_SUB:POST_SUB + SUBLANES] = (
                        upmeta_ref[:, off + lb * LANES:off + (lb + 1) * LANES])

    @pl.when(i > 0)
    def _():
        pext_ref[0:N_META] = pprev_ref[...]

    def store_attn(j, blk, value):
        oattn_ref[slot_new, j * CHUNK:(j + 1) * CHUNK, blk * LANES:(blk + 1) * LANES] = value

    groups, scores, finish = _attention_stages(i_attn, q_ref, kc_ref, kp_ref, vc_ref, vp_ref, mk_ref, mv_ref,
                                               kaug_ref, qaug_ref, store_attn)
    pending = [None]

    def attention_group(g):
        current = pending[0]
        if g + 1 < len(groups):
            pending[0] = scores(*groups[g + 1])
        finish(*groups[g], *current)

    pext_ref[N_META:] = p_ref[...]
    subs = [slice(r0, r0 + POST_SUB) for r0 in range(0, tm, POST_SUB)]

    mixes = []
    for rows in subs:
        attn = _dot(oattn_ref[slot_old, rows], w_ao_ref[...])
        ext_rows = slice(rows.start, rows.stop + N_META)
        pm = jnp.concatenate([_pool_group(pext_ref[ext_rows, g * POOL_GROUP_W:(g + 1) * POOL_GROUP_W], g)
                              for g in range(N_POOL_GROUPS)], axis=1)
        pool = _dot(pm.astype(BF16), w_pool_ref[...])
        mix = g_ref[rows, :D_MODEL].astype(F32) * attn + g_ref[rows, D_MODEL:].astype(F32) * pool
        mixes.append(mix.astype(BF16))
    for rows, mix in zip(subs, mixes):
        x1 = x_ref[rows] + _dot(mix, w_out_ref[...])
        x1_ref[rows] = x1
        hn_ref[rows] = _rmsnorm(x1, gffn_ref[...]).astype(BF16)

    def up_project(rows, c):
        for half, off in enumerate((c * FF_CHUNK, D_FF + c * FF_CHUNK)):
            u = _dot(hn_ref[rows], w_up_ref[:, off:off + FF_CHUNK])
            for lb in range(FF_LANE_BLOCKS):
                blk = uext_refs[c].at[half * FF_LANE_BLOCKS + lb]
                blk[0:SUBLANES] = blk[POST_SUB:POST_SUB + SUBLANES]
                blk[SUBLANES:] = u[:, lb * LANES:(lb + 1) * LANES]
            if rows.stop == tm:
                carry_ref[:, off:off + FF_CHUNK] = u[POST_SUB - SUBLANES:]

    def conv(rows, c, half):
        outs = []
        for lb in range(FF_LANE_BLOCKS):
            off = half * D_FF + c * FF_CHUNK + lb * LANES
            w, b = convw_ref[:, off:off + LANES], convb_ref[:, off:off + LANES]
            ext = uext_refs[c].at[half * FF_LANE_BLOCKS + lb]
            cv = b + ext[SUBLANES - 2:SUBLANES - 2 + POST_SUB] * w[0:1]
            cv = cv + ext[SUBLANES - 1:SUBLANES - 1 + POST_SUB] * w[1:2]
            outs.append(cv + ext[SUBLANES:] * w[2:3])
        return jnp.concatenate(outs, axis=1)

    steps = [(rows, c) for rows in subs for c in range(n_chunks)]
    for step in steps[:FF_SKEW]:
        up_project(*step)
    pending[0] = scores(*groups[0])
    acc = None
    for j, (rows, c) in enumerate(steps):
        act_ref[rows, c * FF_CHUNK:(c + 1) * FF_CHUNK] = (_silu(conv(rows, c, 0)) * conv(rows, c, 1)).astype(BF16)
        if j + FF_SKEW < len(steps):
            up_project(*steps[j + FF_SKEW])
        if c + 1 in DOWN_GROUP_ENDS:
            first = ([0] + list(DOWN_GROUP_ENDS))[DOWN_GROUP_ENDS.index(c + 1)]
            ks = slice(first * FF_CHUNK, (c + 1) * FF_CHUNK)
            d = _dot(act_ref[rows, ks], w_down_ref[ks, :])
            acc = d if first == 0 else acc + d
        if c == n_chunks - 1:
            y_ref[rows] = _rmsnorm(x1_ref[rows] + acc, gfin_ref[...])
        if j < len(groups):
            attention_group(j)
    assert len(groups) <= len(steps)


def _stream_call(q, k, v, mk, mv, kaug, qaug, x, p, p_meta, gates, o0, up_meta8, w_ao, w_pool, w_out, gffn, w_up,
                 convw, convb, w_down, gfin):
    bsz, seq, _ = x.shape
    tm = TM_POST
    assert tm == TQ
    tiles_per_batch = seq // tm
    n_tiles = bsz * tiles_per_batch

    def new_tile(t):
        a = jnp.minimum(t + 1, n_tiles - 1)
        return a // tiles_per_batch, lax.rem(a, tiles_per_batch)

    def old_tile(t):
        return t // tiles_per_batch, lax.rem(t, tiles_per_batch)

    new_row = lambda w: pl.BlockSpec((None, tm, w), lambda t: (*new_tile(t), 0))
    old_row = lambda w: pl.BlockSpec((None, tm, w), lambda t: (*old_tile(t), 0))

    def kv_prev_map(t):
        b, i = new_tile(t)
        return b, jnp.maximum(i * (tm // WINDOW) - 1, 0), 0

    def p_prev_map(t):
        b, i = old_tile(t)
        return b, jnp.maximum(i * (tm // N_META) - 1, 0), 0

    kv_prev = pl.BlockSpec((None, WINDOW, KV_W), kv_prev_map)
    consts = (up_meta8, w_ao, w_pool, w_out, gffn, w_up, convw, convb, w_down, gfin, o0)
    return pl.pallas_call(
        functools.partial(_stream_kernel, tiles_per_batch, n_tiles),
        grid=(n_tiles,),
        in_specs=[new_row(Q_W), new_row(KV_W), kv_prev, new_row(KV_W), kv_prev, _const_spec(mk.shape),
                  _const_spec(mv.shape), _const_spec(kaug.shape), _const_spec(qaug.shape),
                  old_row(D_MODEL), old_row(POOL_WIDTH), pl.BlockSpec((None, N_META, POOL_WIDTH), p_prev_map),
                  _const_spec(p_meta.shape), old_row(GATE_W)] + [_const_spec(a.shape) for a in consts],
        out_specs=(old_row(D_MODEL), pl.BlockSpec((None, SUBLANES, 2 * D_FF), lambda t: (old_tile(t)[0], 0, 0))),
        out_shape=(jax.ShapeDtypeStruct((bsz, seq, D_MODEL), F32),
                   jax.ShapeDtypeStruct((bsz, SUBLANES, 2 * D_FF), F32)),
        scratch_shapes=[pltpu.VMEM((2, tm, Q_W), BF16), pltpu.VMEM((N_META + tm, POOL_WIDTH), F32),
                        pltpu.VMEM((tm, D_MODEL), F32), pltpu.VMEM((tm, D_MODEL), BF16),
                        pltpu.VMEM((tm, D_FF), BF16)]
                       + [pltpu.VMEM((2 * FF_LANE_BLOCKS, SUBLANES + POST_SUB, LANES), F32)
                          for _ in range(D_FF // FF_CHUNK)],
        compiler_params=pltpu.CompilerParams(dimension_semantics=("arbitrary",), vmem_limit_bytes=VMEM_LIMIT),
        name="attn_mixer_ffn",
    )(q, k, k, v, v, mk, mv, kaug, qaug, x, p, p, p_meta, gates, *consts)


def _rope_tables(pos):
    half = HEAD_DIM // 2
    inv = ROPE_THETA ** (-np.arange(half, dtype=np.float64) / half)
    ang = np.asarray(pos, dtype=np.float64)[:, None] * inv[None, :]
    cos, sin = np.cos(ang), np.sin(ang)
    cos = np.tile(cos, (1, LANES // half))
    sin = np.tile(np.concatenate([-sin, sin], axis=1), (1, LANES // HEAD_DIM))
    return jnp.asarray(cos, dtype=F32), jnp.asarray(sin, dtype=F32)


def _score_table(sinks, rows_per_block, n_keys, keys_pad, masked_prefix=0):
    col = jnp.arange(keys_pad)
    base = jnp.where((col >= masked_prefix) & (col < n_keys), 0.0, NEG).astype(F32)
    head = (2 * jnp.arange(Q_BLOCKS)[:, None] + jnp.arange(2)[None, :])
    tab = jnp.where(col[None, None, :] == n_keys, sinks.astype(F32)[head][:, :, None], base[None, None, :])
    tab = tab.reshape(N_KV_HEADS, BLOCKS_PER_KV, 1, 2 * keys_pad)
    tab = jnp.broadcast_to(tab, (N_KV_HEADS, BLOCKS_PER_KV, rows_per_block, 2 * keys_pad))
    return tab.reshape(N_KV_HEADS, BLOCKS_PER_KV * rows_per_block, 2 * keys_pad)


def kernel(x_prompt, x_sample, cache_swa_k, cache_swa_v, cache_meta_k, cache_meta_v, state_pool, state_conv,
           meta_tokens, g_norm_mix, w_in, b_gate, sinks, w_attn_o, w_pool_grp, pool_scale, w_pool_o, w_out,
           g_norm_ffn, w_up, conv_w, conv_b, w_down, g_norm_final):
    bsz, seq, _ = x_prompt.shape
    dbsz, t_dec, _ = x_sample.shape
    row2 = lambda a: a.reshape(1, -1)
    gmix, gffn, gfin = row2(g_norm_mix), row2(g_norm_ffn), row2(g_norm_final)
    bgate, pscale, convb = row2(b_gate), row2(pool_scale), row2(conv_b)

    cos_p, sin_p = _rope_tables(N_META + np.arange(seq))
    (q, k, v, p, gates), (w_ao_b, w_po_b, w_out_b, w_up_b, w_down_b) = _inproj_call(
        x_prompt, cos_p, sin_p, gmix, w_in, bgate, (w_attn_o, w_pool_o, w_out, w_up, w_down))

    pos_side = np.concatenate([np.arange(N_META), np.tile(N_META + PAST_LEN + np.arange(t_dec), dbsz)])
    cos_s, sin_s = _rope_tables(pos_side)
    xs = jnp.concatenate([meta_tokens, x_sample.reshape(dbsz * t_dec, D_MODEL)], axis=0)
    spool16 = jnp.pad(state_pool, ((0, 0), (N_META - POOL_HIST, 0), (0, 0)))
    sconv8 = jnp.pad(state_conv, ((0, 0), (SUBLANES - (CONV_W - 1), 0), (0, 0)))
    tab_meta = _score_table(sinks, N_META, N_META, LANES)
    tab_dec = _score_table(sinks, t_dec, N_META + WINDOW + t_dec, 2 * LANES)
    kaug, qaug = _attn_aug_tables(sinks)
    y_side, k_side, v_side, p_side, up_side, o_first, km, vm, p_meta, up_meta8 = _side_call(
        (xs, cache_swa_k.reshape(dbsz, WINDOW, KV_W), cache_swa_v.reshape(dbsz, WINDOW, KV_W),
         cache_meta_k.reshape(dbsz, N_META, KV_W), cache_meta_v.reshape(dbsz, N_META, KV_W), spool16, sconv8,
         cos_s, sin_s, tab_meta, tab_dec, gmix, w_in, bgate, w_ao_b, w_pool_grp, pscale, w_po_b, w_out_b,
         gffn, w_up_b, conv_w, convb, w_down_b, gfin), q, k, v, kaug, qaug)
    w_pool_b = _pool_weight_call(w_pool_grp, pscale, w_po_b)
    y_prompt, up_tail = _stream_call(q, k, v, km, vm, kaug, qaug, x_prompt, p, p_meta, gates, o_first, up_meta8,
                                     w_ao_b, w_pool_b, w_out_b, gffn, w_up_b, conv_w, convb, w_down_b, gfin)

    kv4 = lambda a, n: a.reshape(a.shape[0], n, N_KV_HEADS, HEAD_DIM)
    dec = lambda a: a[N_META:].reshape(dbsz, t_dec, -1)
    return (
        y_prompt,
        dec(y_side),
        kv4(k[:, seq - WINDOW:], WINDOW),
        kv4(v[:, seq - WINDOW:], WINDOW),
        jnp.broadcast_to(km.reshape(1, N_META, N_KV_HEADS, HEAD_DIM), (bsz, N_META, N_KV_HEADS, HEAD_DIM)),
        jnp.broadcast_to(vm.reshape(1, N_META, N_KV_HEADS, HEAD_DIM), (bsz, N_META, N_KV_HEADS, HEAD_DIM)),
        p[:, seq - POOL_HIST:],
        up_tail[:, SUBLANES - (CONV_W - 1):],
        kv4(dec(k_side), t_dec),
        kv4(dec(v_side), t_dec),
        dec(p_side)[:, t_dec - POOL_HIST:],
        dec(up_side)[:, t_dec - (CONV_W - 1):],
    )
```

```python
import functools

import numpy as np
import jax
import jax.numpy as jnp
from jax import lax
from jax.experimental import pallas as pl
from jax.experimental.pallas import tpu as pltpu

D_MODEL = 1024
N_META = 16
CHUNK = 64
HEAD_DIM = 64
N_Q_HEADS = 16
N_KV_HEADS = 2
WINDOW = 128
ROPE_THETA = 10000.0
POOL_WINDOWS = (2, 4, 8, 16)
N_POOL_GROUPS = 4
POOL_WIDTH = D_MODEL // 2
POOL_GROUP_W = POOL_WIDTH // N_POOL_GROUPS
POOL_HIST = max(POOL_WINDOWS) - 1
Q_W = N_Q_HEADS * HEAD_DIM
KV_W = N_KV_HEADS * HEAD_DIM
GATE_W = 2 * D_MODEL
IN_W = Q_W + 2 * KV_W + POOL_WIDTH + GATE_W
D_FF = ((8 * D_MODEL // 3) + 127) // 128 * 128
CONV_W = 3
RMS_EPS = 1e-6
PAST_LEN = 1024

LANES = 128
SUBLANES = 8
Q_BLOCKS = Q_W // LANES
BLOCKS_PER_KV = Q_BLOCKS // N_KV_HEADS
NEG = -1e30
LOG2E = 1.4426950408889634

K_OFF = Q_W
V_OFF = Q_W + KV_W
P_OFF = Q_W + 2 * KV_W
G_OFF = P_OFF + POOL_WIDTH

F32 = jnp.float32
BF16 = jnp.bfloat16

VMEM_LIMIT = 60 * 1024 * 1024


def _rmsnorm(x, g):
    ms = jnp.mean(x * x, axis=-1, keepdims=True)
    return (x * lax.rsqrt(ms + RMS_EPS)) * g


def _dot(a, b):
    return jnp.dot(a, b, preferred_element_type=F32)


def _rope(x, cos, sin_signed):
    lane = lax.broadcasted_iota(jnp.int32, (x.shape[0], LANES), 1)
    first_half = (lane & (HEAD_DIM // 2)) == 0
    outs = []
    for b in range(x.shape[1] // LANES):
        xb = x[:, b * LANES:(b + 1) * LANES]
        partner = jnp.where(first_half, pltpu.roll(xb, LANES - HEAD_DIM // 2, 1), pltpu.roll(xb, HEAD_DIM // 2, 1))
        outs.append(xb * cos + partner * sin_signed)
    return outs[0] if len(outs) == 1 else jnp.concatenate(outs, axis=1)


def _kv_variants(kv, dtype=BF16):
    lane = lax.broadcasted_iota(jnp.int32, kv.shape, 1)
    low = lane < HEAD_DIM
    swapped = pltpu.roll(kv, HEAD_DIM, 1)
    zero = jnp.zeros_like(kv)
    return (
        (jnp.where(low, kv, zero).astype(dtype), jnp.where(low, zero, swapped).astype(dtype)),
        (jnp.where(low, swapped, zero).astype(dtype), jnp.where(low, zero, kv).astype(dtype)),
    )


def _attend_group(qg, kst, vst, tab):
    s = lax.dot_general(qg, kst, (((1,), (1,)), ((), ())), preferred_element_type=F32) + tab
    npad = kst.shape[0] // 2
    se, so = s[:, :npad], s[:, npad:]
    pe = jnp.exp(se - jnp.max(se, axis=1, keepdims=True))
    po = jnp.exp(so - jnp.max(so, axis=1, keepdims=True))
    le = jnp.sum(pe, axis=1, keepdims=True)
    lo = jnp.sum(po, axis=1, keepdims=True)
    p = jnp.concatenate([pe, po], axis=1).astype(BF16)
    o = _dot(p, vst)
    lane = lax.broadcasted_iota(jnp.int32, o.shape, 1)
    return o * jnp.where(lane < HEAD_DIM, 1.0 / le, 1.0 / lo)


def _pool_group(ext, g, cnt=None):
    w = POOL_WINDOWS[g]
    s = ext
    k = 1
    while k < w:
        s = s + pltpu.roll(s, k, 0)
        k *= 2
    cur = ext[N_META:]
    return s[N_META:] / (float(w) if cnt is None else cnt) - cur


def _pool_project(pm, wgrp_ref, pscale):
    outs = []
    for g in range(N_POOL_GROUPS):
        sl = slice(g * POOL_GROUP_W, (g + 1) * POOL_GROUP_W)
        outs.append(_dot(pm[:, sl].astype(BF16), wgrp_ref[g].astype(BF16)))
    return (jnp.concatenate(outs, axis=1) * pscale).astype(BF16)


def _conv3(u, prev8, w, b):
    t = u.shape[0]
    ext = jnp.concatenate([prev8, u], axis=0)
    u1 = pltpu.roll(ext, 1, 0)[SUBLANES:SUBLANES + t]
    u2 = pltpu.roll(ext, 2, 0)[SUBLANES:SUBLANES + t]
    c = b + u2 * w[0:1]
    c = c + u1 * w[1:2]
    return c + u * w[2:3]


def _sigmoid(x):
    return 0.5 * jnp.tanh(0.5 * x) + 0.5


def _silu(x):
    half = 0.5 * x
    return half + half * jnp.tanh(half)


SIDE_FF_CHUNK = 256


SIDE_HBM_WEIGHTS = ("w_in", "w_ao", "w_po", "w_out", "w_up", "w_down")
SIDE_HBM_ARGS = (12, 14, 17, 18, 20, 23)


def _side_kernel(xs_ref, ck_ref, cv_ref, cmk_ref, cmv_ref, spool_ref, sconv_ref, cos_ref, sin_ref, tabm_ref,
                 tabs_ref, gmix_ref, w_in_hbm, bgate_ref, w_ao_hbm, wgrp_ref, pscale_ref, w_po_hbm, w_out_hbm,
                 gffn_ref, w_up_hbm, convw_ref, convb_ref, w_down_hbm, gfin_ref,
                 q0_ref, k0_ref, v0_ref, kaug_ref, qaug_ref,
                 y_ref, k_ref, v_ref, p_ref, up_ref, o0_ref, km_ref, vm_ref, pm_ref, upm_ref,
                 w_in_ref, w_ao_ref, w_po_ref, w_out_ref, w_up_ref, w_down_ref, dma_sems):
    hbm = (w_in_hbm, w_ao_hbm, w_po_hbm, w_out_hbm, w_up_hbm, w_down_hbm)
    vmem = (w_in_ref, w_ao_ref, w_po_ref, w_out_ref, w_up_ref, w_down_ref)
    copies = {name: pltpu.make_async_copy(src, dst, dma_sems.at[n])
              for n, (name, src, dst) in enumerate(zip(SIDE_HBM_WEIGHTS, hbm, vmem))}
    for name in SIDE_HBM_WEIGHTS:
        copies[name].start()

    n_side = xs_ref.shape[0]
    dec_b = ck_ref.shape[0]
    t_dec = (n_side - N_META) // dec_b
    x = xs_ref[...]
    h = _rmsnorm(x, gmix_ref[...]).astype(BF16)
    cos, sin = cos_ref[...], sin_ref[...]
    copies["w_in"].wait()
    w_in = lambda c0, c1: w_in_ref[:, c0:c1].astype(BF16)
    q = _rope(_dot(h, w_in(0, Q_W)) * (HEAD_DIM ** -0.5), cos, sin).astype(BF16)
    k = _rope(_dot(h, w_in(K_OFF, K_OFF + KV_W)), cos, sin)
    v = _dot(h, w_in(V_OFF, V_OFF + KV_W))
    p = _dot(h, w_in(P_OFF, P_OFF + POOL_WIDTH))
    gates = _sigmoid(jnp.concatenate([_dot(h, w_in(G_OFF + c0, G_OFF + c0 + GATE_CHUNK))
                                      for c0 in range(0, GATE_W, GATE_CHUNK)], axis=1) + bgate_ref[...])
    k_ref[...] = k
    v_ref[...] = v
    p_ref[...] = p
    km_ref[...] = k[:N_META]
    vm_ref[...] = v[:N_META]
    pm_ref[...] = p[:N_META]

    def attend_rows(r0, nrows, kx, vx, tab_ref):
        kvar, vvar = _kv_variants(kx), _kv_variants(vx)
        blocks = [None] * Q_BLOCKS
        for n in range(N_KV_HEADS):
            qg = jnp.concatenate([q[r0:r0 + nrows, (BLOCKS_PER_KV * n + bi) * LANES:(BLOCKS_PER_KV * n + bi + 1) * LANES]
                                  for bi in range(BLOCKS_PER_KV)], axis=0)
            kst = jnp.concatenate(kvar[n], axis=0)
            vst = jnp.concatenate(vvar[n], axis=0)
            o = _attend_group(qg, kst, vst, tab_ref[n])
            for bi in range(BLOCKS_PER_KV):
                blocks[BLOCKS_PER_KV * n + bi] = o[bi * nrows:(bi + 1) * nrows]
        return jnp.concatenate(blocks, axis=1)

    zpad = jnp.zeros((LANES - N_META, LANES), F32)
    o_rows = [attend_rows(0, N_META, jnp.concatenate([k[:N_META], zpad], axis=0),
                          jnp.concatenate([v[:N_META], zpad], axis=0), tabm_ref)]
    n_keys = N_META + WINDOW + t_dec
    zpad = jnp.zeros((2 * LANES - n_keys, LANES), F32)
    for b in range(dec_b):
        r0 = N_META + b * t_dec
        kx = jnp.concatenate([cmk_ref[b], ck_ref[b], k[r0:r0 + t_dec], zpad], axis=0)
        vx = jnp.concatenate([cmv_ref[b], cv_ref[b], v[r0:r0 + t_dec], zpad], axis=0)
        o_rows.append(attend_rows(r0, t_dec, kx, vx, tabs_ref))
    o_attn = jnp.concatenate(o_rows, axis=0).astype(BF16)

    row = lax.broadcasted_iota(jnp.int32, (N_META, LANES), 0)
    pm_rows = []
    for seg in range(1 + dec_b):
        if seg == 0:
            ext = jnp.concatenate([jnp.zeros((N_META, POOL_WIDTH), F32), p[:N_META]], axis=0)
        else:
            r0 = N_META + (seg - 1) * t_dec
            ext = jnp.concatenate([spool_ref[seg - 1], p[r0:r0 + t_dec]], axis=0)
        groups = []
        for g in range(N_POOL_GROUPS):
            cnt = jnp.minimum(POOL_WINDOWS[g], row + 1).astype(F32) if seg == 0 else None
            groups.append(_pool_group(ext[:, g * POOL_GROUP_W:(g + 1) * POOL_GROUP_W], g, cnt))
        pm_rows.append(jnp.concatenate(groups, axis=1))
    pm = jnp.concatenate(pm_rows, axis=0)

    pool = _pool_project(pm, wgrp_ref, pscale_ref[...])
    for name in ("w_ao", "w_po", "w_out"):
        copies[name].wait()
    mix = gates[:, :D_MODEL] * _dot(o_attn, w_ao_ref[...]) + gates[:, D_MODEL:] * _dot(pool, w_po_ref[...])
    x1 = x + _dot(mix.astype(BF16), w_out_ref[...])

    hn = _rmsnorm(x1, gffn_ref[...]).astype(BF16)
    acc = jnp.zeros((n_side, D_MODEL), F32)
    zprev = jnp.zeros((SUBLANES, SIDE_FF_CHUNK), F32)
    copies["w_up"].wait()
    copies["w_down"].wait()
    for c0 in range(0, D_FF, SIDE_FF_CHUNK):
        halves = []
        for off in (c0, D_FF + c0):
            cs = slice(off, off + SIDE_FF_CHUNK)
            u = _dot(hn, w_up_ref[:, cs])
            up_ref[:, cs] = u
            w, bias = convw_ref[:, cs], convb_ref[:, cs]
            segs = [_conv3(u[:N_META], zprev, w, bias)]
            for b in range(dec_b):
                r0 = N_META + b * t_dec
                segs.append(_conv3(u[r0:r0 + t_dec], sconv_ref[b, :, cs], w, bias))
            halves.append(jnp.concatenate(segs, axis=0))
        act = (_silu(halves[0]) * halves[1]).astype(BF16)
        acc = acc + _dot(act, w_down_ref[c0:c0 + SIDE_FF_CHUNK, :])
    y_ref[...] = _rmsnorm(x1 + acc, gfin_ref[...])
    upm_ref[...] = up_ref[N_META - SUBLANES:N_META]

    def store_o0(j, blk, value):
        o0_ref[j * CHUNK:(j + 1) * CHUNK, blk * LANES:(blk + 1) * LANES] = value

    groups, scores, finish = _attention_stages(0, q0_ref, k0_ref, k0_ref.at[0:WINDOW], v0_ref, v0_ref.at[0:WINDOW],
                                               k_ref.at[0:N_META], v_ref.at[0:N_META], kaug_ref, qaug_ref, store_o0)
    pending = scores(*groups[0])
    for g, group in enumerate(groups):
        current = pending
        if g + 1 < len(groups):
            pending = scores(*groups[g + 1])
        finish(*group, *current)


def _side_call(args, q, k, v, kaug, qaug):
    n_side = args[0].shape[0]
    out_shape = (
        jax.ShapeDtypeStruct((n_side, D_MODEL), F32),
        jax.ShapeDtypeStruct((n_side, KV_W), F32),
        jax.ShapeDtypeStruct((n_side, KV_W), F32),
        jax.ShapeDtypeStruct((n_side, POOL_WIDTH), F32),
        jax.ShapeDtypeStruct((n_side, 2 * D_FF), F32),
        jax.ShapeDtypeStruct((TQ, Q_W), BF16),
        jax.ShapeDtypeStruct((N_META, KV_W), F32),
        jax.ShapeDtypeStruct((N_META, KV_W), F32),
        jax.ShapeDtypeStruct((N_META, POOL_WIDTH), F32),
        jax.ShapeDtypeStruct((SUBLANES, 2 * D_FF), F32),
    )
    first_tile = lambda w: pl.BlockSpec((None, TQ, w), lambda i: (0, 0, 0))
    hbm_weights = [args[n] for n in SIDE_HBM_ARGS]
    assert [w.shape for w in hbm_weights] == [(D_MODEL, IN_W), (Q_W, D_MODEL), (POOL_WIDTH, D_MODEL),
                                              (D_MODEL, D_MODEL), (D_MODEL, 2 * D_FF), (D_FF, D_MODEL)]
    arg_specs = [pl.BlockSpec(memory_space=pl.ANY) if n in SIDE_HBM_ARGS else _const_spec(a.shape)
                 for n, a in enumerate(args)]
    return pl.pallas_call(
        _side_kernel,
        grid=(1,),
        in_specs=arg_specs + [first_tile(Q_W), first_tile(KV_W), first_tile(KV_W),
                              _const_spec(kaug.shape), _const_spec(qaug.shape)],
        out_specs=tuple(_const_spec(s.shape) for s in out_shape),
        out_shape=out_shape,
        scratch_shapes=[pltpu.VMEM(w.shape, w.dtype) for w in hbm_weights]
                       + [pltpu.SemaphoreType.DMA((len(hbm_weights),))],
        compiler_params=pltpu.CompilerParams(dimension_semantics=("arbitrary",), vmem_limit_bytes=VMEM_LIMIT),
        name="side_rows",
    )(*args, q, k, v, kaug, qaug)


TM_IN = 1024
IN_SUB = 256
GATE_CHUNK = 512


W_IN_CAST_CHUNK = 768


def _inproj_kernel(n_cast, x_ref, cos_ref, sin_ref, gmix_ref, w_in_f32_ref, bgate_ref, *refs):
    cast_in, (q_ref, k_ref, v_ref, p_ref, g_ref) = refs[:n_cast], refs[n_cast:n_cast + 5]
    cast_out, w_in_ref = refs[n_cast + 5:2 * n_cast + 5], refs[2 * n_cast + 5]

    @pl.when((pl.program_id(0) == 0) & (pl.program_id(1) == 0))
    def _():
        for c0 in range(0, IN_W, W_IN_CAST_CHUNK):
            w_in_ref[:, c0:c0 + W_IN_CAST_CHUNK] = w_in_f32_ref[:, c0:c0 + W_IN_CAST_CHUNK].astype(BF16)

    for src, dst in zip(cast_in, cast_out):
        dst[...] = src[...].astype(BF16)

    subs = [slice(r0, r0 + IN_SUB) for r0 in range(0, x_ref.shape[0], IN_SUB)]
    hs = [_rmsnorm(x_ref[rows], gmix_ref[...]).astype(BF16) for rows in subs]
    for rows, h in zip(subs, hs):
        cos, sin = cos_ref[rows], sin_ref[rows]
        for c0 in range(0, GATE_W, GATE_CHUNK):
            z = _dot(h, w_in_ref[:, G_OFF + c0:G_OFF + c0 + GATE_CHUNK]) + bgate_ref[:, c0:c0 + GATE_CHUNK]
            g_ref[rows, c0:c0 + GATE_CHUNK] = _sigmoid(z).astype(BF16)
        q_ref[rows] = _rope(_dot(h, w_in_ref[:, 0:Q_W]) * (HEAD_DIM ** -0.5 * LOG2E), cos, sin).astype(BF16)
        p_ref[rows] = _dot(h, w_in_ref[:, P_OFF:P_OFF + POOL_WIDTH])
        kv = _dot(h, w_in_ref[:, K_OFF:K_OFF + 2 * KV_W])
        k_ref[rows] = _rope(kv[:, :KV_W], cos, sin)
        v_ref[rows] = kv[:, KV_W:]


def _const_spec(shape):
    nd = len(shape)
    return pl.BlockSpec(shape, lambda *_: (0,) * nd)


BF16_SUBLANES = 16


def _inproj_call(x, cos, sin, gmix, w_in, bgate, other_weights):
    bsz, seq, _ = x.shape
    tm = TM_IN
    tiles = seq // tm
    n_steps = bsz * tiles
    row = lambda w: pl.BlockSpec((None, tm, w), lambda b, i: (b, i, 0))

    def slab_spec(w):
        n_slabs = n_steps
        while w.shape[0] % (n_slabs * BF16_SUBLANES):
            n_slabs //= 2
        return pl.BlockSpec((w.shape[0] // n_slabs, w.shape[1]),
                            lambda b, i: (jnp.minimum(b * tiles + i, n_slabs - 1), 0))

    slabs = [slab_spec(w) for w in other_weights]
    outs = pl.pallas_call(
        functools.partial(_inproj_kernel, len(other_weights)),
        grid=(bsz, tiles),
        in_specs=[row(D_MODEL), pl.BlockSpec((tm, LANES), lambda b, i: (i, 0)),
                  pl.BlockSpec((tm, LANES), lambda b, i: (i, 0)),
                  _const_spec(gmix.shape), _const_spec(w_in.shape), _const_spec(bgate.shape)] + slabs,
        out_specs=(row(Q_W), row(KV_W), row(KV_W), row(POOL_WIDTH), row(GATE_W), *slabs),
        out_shape=(jax.ShapeDtypeStruct((bsz, seq, Q_W), BF16), jax.ShapeDtypeStruct((bsz, seq, KV_W), F32),
                   jax.ShapeDtypeStruct((bsz, seq, KV_W), F32), jax.ShapeDtypeStruct((bsz, seq, POOL_WIDTH), F32),
                   jax.ShapeDtypeStruct((bsz, seq, GATE_W), BF16),
                   *[jax.ShapeDtypeStruct(w.shape, BF16) for w in other_weights]),
        scratch_shapes=[pltpu.VMEM(w_in.shape, BF16)],
        compiler_params=pltpu.CompilerParams(dimension_semantics=("arbitrary", "arbitrary"),
                                             vmem_limit_bytes=VMEM_LIMIT),
        name="in_proj",
    )(x, cos, sin, gmix, w_in, bgate, *other_weights)
    return outs[:5], outs[5:]


TQ = 512
BAND = WINDOW + CHUNK
KEYS_PAD = BAND + N_META + 16
N_TABS = WINDOW // CHUNK + 1
SUM_ROWS = 16


def _attention_stages(i, q_ref, kc_ref, kp_ref, vc_ref, vp_ref, mk_ref, mv_ref, kaug_ref, qaug_ref, store):
    kvar = _kv_variants(jnp.concatenate([kp_ref[...], kc_ref[...]], axis=0), BF16)
    vvar = _kv_variants(jnp.concatenate([vp_ref[...], vc_ref[...]], axis=0), F32)
    mkvar, mvvar = _kv_variants(mk_ref[...], BF16), _kv_variants(mv_ref[...], F32)
    kpad = jnp.zeros((KEYS_PAD - BAND - N_META, LANES), BF16)
    vpad = jnp.zeros((KEYS_PAD - BAND - N_META, LANES), F32)
    row = lax.broadcasted_iota(jnp.int32, (SUM_ROWS, 2 * KEYS_PAD), 0)
    col = lax.broadcasted_iota(jnp.int32, (SUM_ROWS, 2 * KEYS_PAD), 1)
    ones = jnp.where((row == 0) & (col < KEYS_PAD) | (row == 1) & (col >= KEYS_PAD), 1.0, 0.0).astype(BF16)

    def scores(j, n):
        tsel = jnp.minimum(i * (TQ // CHUNK) + j, N_TABS - 1)
        rows = slice(j * CHUNK, (j + 1) * CHUNK)
        band = slice(j * CHUNK, j * CHUNK + BAND)
        kst = jnp.concatenate([kvar[n][0][band], mkvar[n][0], kpad, kvar[n][1][band], mkvar[n][1], kpad], axis=0)
        vst = jnp.concatenate([vvar[n][0][band], mvvar[n][0], vpad, vvar[n][1][band], mvvar[n][1], vpad], axis=0)
        qg = jnp.concatenate([q_ref[rows, (BLOCKS_PER_KV * n + bi) * LANES:(BLOCKS_PER_KV * n + bi + 1) * LANES]
                              for bi in range(BLOCKS_PER_KV)], axis=0)
        s = lax.dot_general(jnp.concatenate([kst, kaug_ref[tsel]], axis=1),
                            jnp.concatenate([qg, qaug_ref[n]], axis=1),
                            (((1,), (1,)), ((), ())), preferred_element_type=F32)
        vt = jnp.concatenate([vst.T.astype(BF16), ones], axis=0)
        return s, vt

    def finish(j, n, s, vt):
        se, so = s[:KEYS_PAD], s[KEYS_PAD:]
        pe = jnp.exp2(se - jnp.max(se, axis=0, keepdims=True))
        po = jnp.exp2(so - jnp.max(so, axis=0, keepdims=True))
        o = _dot(vt, jnp.concatenate([pe, po], axis=0).astype(BF16))
        o = jnp.concatenate([o[:HEAD_DIM] / o[LANES:LANES + 1], o[HEAD_DIM:LANES] / o[LANES + 1:LANES + 2]], axis=0).T
        for bi in range(BLOCKS_PER_KV):
            store(j, BLOCKS_PER_KV * n + bi, o[bi * CHUNK:(bi + 1) * CHUNK].astype(BF16))

    groups = [(j, n) for j in range(TQ // CHUNK) for n in range(N_KV_HEADS)]
    return groups, scores, finish


def _attn_aug_tables(sinks):
    n_keys = BAND + N_META
    r = np.arange(KEYS_PAD)
    lane = np.arange(LANES)[None, :]
    kaug = []
    for c in range(N_TABS):
        masked = (r < (WINDOW // CHUNK - c) * CHUNK) | (r > n_keys)
        mask_col = np.where(masked, NEG, 0.0)[:, None] * (lane == 0)
        sink_row = (r == n_keys)[:, None]
        even = mask_col + np.where(sink_row & ((lane == 1) | (lane == 2)), 1.0, 0.0)
        odd = mask_col + np.where(sink_row & ((lane == 3) | (lane == 4)), 1.0, 0.0)
        kaug.append(np.concatenate([even, odd], axis=0))
    s2 = sinks.astype(F32) * LOG2E
    hi = s2.astype(BF16).astype(F32)
    lo = s2 - hi
    head = 2 * jnp.arange(Q_BLOCKS)
    cols = [jnp.ones((Q_BLOCKS,), F32), hi[head], lo[head], hi[head + 1], lo[head + 1]]
    qaug = jnp.stack(cols, axis=1)
    qaug = jnp.pad(qaug, ((0, 0), (0, LANES - qaug.shape[1])))
    qaug = jnp.broadcast_to(qaug[:, None, :], (Q_BLOCKS, CHUNK, LANES)).reshape(N_KV_HEADS, BLOCKS_PER_KV * CHUNK, LANES)
    return jnp.asarray(np.stack(kaug), dtype=BF16), qaug.astype(BF16)


TM_POST = 512
POST_SUB = 256
FF_CHUNK = 256
FF_SKEW = 3
FF_LANE_BLOCKS = FF_CHUNK // LANES
DOWN_GROUP_ENDS = (4, 8, 10, 11)
assert DOWN_GROUP_ENDS[-1] == D_FF // FF_CHUNK


def _pool_weight_kernel(wgrp_ref, pscale_ref, w_po_ref, out_ref):
    for g in range(N_POOL_GROUPS):
        sl = slice(g * POOL_GROUP_W, (g + 1) * POOL_GROUP_W)
        out_ref[sl, :] = _dot((wgrp_ref[g] * pscale_ref[:, sl]).astype(BF16), w_po_ref[sl, :]).astype(BF16)


def _pool_weight_call(wgrp, pscale, w_po):
    return pl.pallas_call(
        _pool_weight_kernel,
        out_shape=jax.ShapeDtypeStruct((POOL_WIDTH, D_MODEL), BF16),
        name="pool_weights",
    )(wgrp, pscale, w_po)


def _stream_kernel(tiles_per_batch, n_tiles,
                   q_ref, kc_ref, kp_ref, vc_ref, vp_ref, mk_ref, mv_ref, kaug_ref, qaug_ref,
                   x_ref, p_ref, pprev_ref, pmeta_ref, g_ref, upmeta_ref, w_ao_ref, w_pool_ref,
                   w_out_ref, gffn_ref, w_up_ref, convw_ref, convb_ref, w_down_ref, gfin_ref,
                   o0_ref, y_ref, carry_ref, oattn_ref, pext_ref, x1_ref, hn_ref, act_ref, *uext_refs):
    t = pl.program_id(0)
    i_attn = lax.rem(jnp.minimum(t + 1, n_tiles - 1), tiles_per_batch)
    i = lax.rem(t, tiles_per_batch)
    slot_new, slot_old = lax.rem(t + 1, 2), lax.rem(t, 2)
    tm = x_ref.shape[0]
    n_chunks = D_FF // FF_CHUNK

    @pl.when(t == 0)
    def _():
        oattn_ref[0] = o0_ref[...]

    @pl.when(i == 0)
    def _():
        pext_ref[0:N_META] = pmeta_ref[...]
        for c in range(n_chunks):
            for half, off in enumerate((c * FF_CHUNK, D_FF + c * FF_CHUNK)):
                for lb in range(FF_LANE_BLOCKS):
                    uext_refs[c][half * FF_LANE_BLOCKS + lb, POST_SUB:POST_SUB + SUBLANES] = (
                        upmeta_ref[:, off + lb * LANES:off + (lb + 1) * LANES])

    @pl.when(i > 0)
    def _():
        pext_ref[0:N_META] = pprev_ref[...]

    def store_attn(j, blk, value):
        oattn_ref[slot_new, j * CHUNK:(j + 1) * CHUNK, blk * LANES:(blk + 1) * LANES] = value

    groups, scores, finish = _attention_stages(i_attn, q_ref, kc_ref, kp_ref, vc_ref, vp_ref, mk_ref, mv_ref,
                                               kaug_ref, qaug_ref, store_attn)
    pending = [None]

    def attention_group(g):
        current = pending[0]
        if g + 1 < len(groups):
            pending[0] = scores(*groups[g + 1])
        finish(*groups[g], *current)

    pext_ref[N_META:] = p_ref[...]
    subs = [slice(r0, r0 + POST_SUB) for r0 in range(0, tm, POST_SUB)]

    mixes = []
    for rows in subs:
        attn = _dot(oattn_ref[slot_old, rows], w_ao_ref[...])
        ext_rows = slice(rows.start, rows.stop + N_META)
        pm = jnp.concatenate([_pool_group(pext_ref[ext_rows, g * POOL_GROUP_W:(g + 1) * POOL_GROUP_W], g)
                              for g in range(N_POOL_GROUPS)], axis=1)
        pool = _dot(pm.astype(BF16), w_pool_ref[...])
        mix = g_ref[rows, :D_MODEL].astype(F32) * attn + g_ref[rows, D_MODEL:].astype(F32) * pool
        mixes.append(mix.astype(BF16))
    for rows, mix in zip(subs, mixes):
        x1 = x_ref[rows] + _dot(mix, w_out_ref[...])
        x1_ref[rows] = x1
        hn_ref[rows] = _rmsnorm(x1, gffn_ref[...]).astype(BF16)

    def up_project(rows, c):
        for half, off in enumerate((c * FF_CHUNK, D_FF + c * FF_CHUNK)):
            u = _dot(hn_ref[rows], w_up_ref[:, off:off + FF_CHUNK])
            for lb in range(FF_LANE_BLOCKS):
                blk = uext_refs[c].at[half * FF_LANE_BLOCKS + lb]
                blk[0:SUBLANES] = blk[POST_SUB:POST_SUB + SUBLANES]
                blk[SUBLANES:] = u[:, lb * LANES:(lb + 1) * LANES]
            if rows.stop == tm:
                carry_ref[:, off:off + FF_CHUNK] = u[POST_SUB - SUBLANES:]

    def conv(rows, c, half):
        outs = []
        for lb in range(FF_LANE_BLOCKS):
            off = half * D_FF + c * FF_CHUNK + lb * LANES
            w, b = convw_ref[:, off:off + LANES], convb_ref[:, off:off + LANES]
            ext = uext_refs[c].at[half * FF_LANE_BLOCKS + lb]
            cv = b + ext[SUBLANES - 2:SUBLANES - 2 + POST_SUB] * w[0:1]
            cv = cv + ext[SUBLANES - 1:SUBLANES - 1 + POST_SUB] * w[1:2]
            outs.append(cv + ext[SUBLANES:] * w[2:3])
        return jnp.concatenate(outs, axis=1)

    steps = [(rows, c) for rows in subs for c in range(n_chunks)]
    for step in steps[:FF_SKEW]:
        up_project(*step)
    pending[0] = scores(*groups[0])
    acc = None
    for j, (rows, c) in enumerate(steps):
        act_ref[rows, c * FF_CHUNK:(c + 1) * FF_CHUNK] = (_silu(conv(rows, c, 0)) * conv(rows, c, 1)).astype(BF16)
        if j + FF_SKEW < len(steps):
            up_project(*steps[j + FF_SKEW])
        if c + 1 in DOWN_GROUP_ENDS:
            first = ([0] + list(DOWN_GROUP_ENDS))[DOWN_GROUP_ENDS.index(c + 1)]
            ks = slice(first * FF_CHUNK, (c + 1) * FF_CHUNK)
            d = _dot(act_ref[rows, ks], w_down_ref[ks, :])
            acc = d if first == 0 else acc + d
        if c == n_chunks - 1:
            y_ref[rows] = _rmsnorm(x1_ref[rows] + acc, gfin_ref[...])
        if j < len(groups):
            attention_group(j)
    assert len(groups) <= len(steps)


def _stream_call(q, k, v, mk, mv, kaug, qaug, x, p, p_meta, gates, o0, up_meta8, w_ao, w_pool, w_out, gffn, w_up,
                 convw, convb, w_down, gfin):
    bsz, seq, _ = x.shape
    tm = TM_POST
    assert tm == TQ
    tiles_per_batch = seq // tm
    n_tiles = bsz * tiles_per_batch

    def new_tile(t):
        a = jnp.minimum(t + 1, n_tiles - 1)
        return a // tiles_per_batch, lax.rem(a, tiles_per_batch)

    def old_tile(t):
        return t // tiles_per_batch, lax.rem(t, tiles_per_batch)

    new_row = lambda w: pl.BlockSpec((None, tm, w), lambda t: (*new_tile(t), 0))
    old_row = lambda w: pl.BlockSpec((None, tm, w), lambda t: (*old_tile(t), 0))

    def kv_prev_map(t):
        b, i = new_tile(t)
        return b, jnp.maximum(i * (tm // WINDOW) - 1, 0), 0

    def p_prev_map(t):
        b, i = old_tile(t)
        return b, jnp.maximum(i * (tm // N_META) - 1, 0), 0

    kv_prev = pl.BlockSpec((None, WINDOW, KV_W), kv_prev_map)
    consts = (up_meta8, w_ao, w_pool, w_out, gffn, w_up, convw, convb, w_down, gfin, o0)
    return pl.pallas_call(
        functools.partial(_stream_kernel, tiles_per_batch, n_tiles),
        grid=(n_tiles,),
        in_specs=[new_row(Q_W), new_row(KV_W), kv_prev, new_row(KV_W), kv_prev, _const_spec(mk.shape),
                  _const_spec(mv.shape), _const_spec(kaug.shape), _const_spec(qaug.shape),
                  old_row(D_MODEL), old_row(POOL_WIDTH), pl.BlockSpec((None, N_META, POOL_WIDTH), p_prev_map),
                  _const_spec(p_meta.shape), old_row(GATE_W)] + [_const_spec(a.shape) for a in consts],
        out_specs=(old_row(D_MODEL), pl.BlockSpec((None, SUBLANES, 2 * D_FF), lambda t: (old_tile(t)[0], 0, 0))),
        out_shape=(jax.ShapeDtypeStruct((bsz, seq, D_MODEL), F32),
                   jax.ShapeDtypeStruct((bsz, SUBLANES, 2 * D_FF), F32)),
        scratch_shapes=[pltpu.VMEM((2, tm, Q_W), BF16), pltpu.VMEM((N_META + tm, POOL_WIDTH), F32),
                        pltpu.VMEM((tm, D_MODEL), F32), pltpu.VMEM((tm, D_MODEL), BF16),
                        pltpu.VMEM((tm, D_FF), BF16)]
                       + [pltpu.VMEM((2 * FF_LANE_BLOCKS, SUBLANES + POST_SUB, LANES), F32)
                          for _ in range(D_FF // FF_CHUNK)],
        compiler_params=pltpu.CompilerParams(dimension_semantics=("arbitrary",), vmem_limit_bytes=VMEM_LIMIT),
        name="attn_mixer_ffn",
    )(q, k, k, v, v, mk, mv, kaug, qaug, x, p, p, p_meta, gates, *consts)


def _rope_tables(pos):
    half = HEAD_DIM // 2
    inv = ROPE_THETA ** (-np.arange(half, dtype=np.float64) / half)
    ang = np.asarray(pos, dtype=np.float64)[:, None] * inv[None, :]
    cos, sin = np.cos(ang), np.sin(ang)
    cos = np.tile(cos, (1, LANES // half))
    sin = np.tile(np.concatenate([-sin, sin], axis=1), (1, LANES // HEAD_DIM))
    return jnp.asarray(cos, dtype=F32), jnp.asarray(sin, dtype=F32)


def _score_table(sinks, rows_per_block, n_keys, keys_pad, masked_prefix=0):
    col = jnp.arange(keys_pad)
    base = jnp.where((col >= masked_prefix) & (col < n_keys), 0.0, NEG).astype(F32)
    head = (2 * jnp.arange(Q_BLOCKS)[:, None] + jnp.arange(2)[None, :])
    tab = jnp.where(col[None, None, :] == n_keys, sinks.astype(F32)[head][:, :, None], base[None, None, :])
    tab = tab.reshape(N_KV_HEADS, BLOCKS_PER_KV, 1, 2 * keys_pad)
    tab = jnp.broadcast_to(tab, (N_KV_HEADS, BLOCKS_PER_KV, rows_per_block, 2 * keys_pad))
    return tab.reshape(N_KV_HEADS, BLOCKS_PER_KV * rows_per_block, 2 * keys_pad)


def kernel(x_prompt, x_sample, cache_swa_k, cache_swa_v, cache_meta_k, cache_meta_v, state_pool, state_conv,
           meta_tokens, g_norm_mix, w_in, b_gate, sinks, w_attn_o, w_pool_grp, pool_scale, w_pool_o, w_out,
           g_norm_ffn, w_up, conv_w, conv_b, w_down, g_norm_final):
    bsz, seq, _ = x_prompt.shape
    dbsz, t_dec, _ = x_sample.shape
    row2 = lambda a: a.reshape(1, -1)
    gmix, gffn, gfin = row2(g_norm_mix), row2(g_norm_ffn), row2(g_norm_final)
    bgate, pscale, convb = row2(b_gate), row2(pool_scale), row2(conv_b)

    cos_p, sin_p = _rope_tables(N_META + np.arange(seq))
    (q, k, v, p, gates), (w_ao_b, w_po_b, w_out_b, w_up_b, w_down_b) = _inproj_call(
        x_prompt, cos_p, sin_p, gmix, w_in, bgate, (w_attn_o, w_pool_o, w_out, w_up, w_down))

    pos_side = np.concatenate([np.arange(N_META), np.tile(N_META + PAST_LEN + np.arange(t_dec), dbsz)])
    cos_s, sin_s = _rope_tables(pos_side)
    xs = jnp.concatenate([meta_tokens, x_sample.reshape(dbsz * t_dec, D_MODEL)], axis=0)
    spool16 = jnp.pad(state_pool, ((0, 0), (N_META - POOL_HIST, 0), (0, 0)))
    sconv8 = jnp.pad(state_conv, ((0, 0), (SUBLANES - (CONV_W - 1), 0), (0, 0)))
    tab_meta = _score_table(sinks, N_META, N_META, LANES)
    tab_dec = _score_table(sinks, t_dec, N_META + WINDOW + t_dec, 2 * LANES)
    kaug, qaug = _attn_aug_tables(sinks)
    y_side, k_side, v_side, p_side, up_side, o_first, km, vm, p_meta, up_meta8 = _side_call(
        (xs, cache_swa_k.reshape(dbsz, WINDOW, KV_W), cache_swa_v.reshape(dbsz, WINDOW, KV_W),
         cache_meta_k.reshape(dbsz, N_META, KV_W), cache_meta_v.reshape(dbsz, N_META, KV_W), spool16, sconv8,
         cos_s, sin_s, tab_meta, tab_dec, gmix, w_in, bgate, w_ao_b, w_pool_grp, pscale, w_po_b, w_out_b,
         gffn, w_up_b, conv_w, convb, w_down_b, gfin), q, k, v, kaug, qaug)
    w_pool_b = _pool_weight_call(w_pool_grp, pscale, w_po_b)
    y_prompt, up_tail = _stream_call(q, k, v, km, vm, kaug, qaug, x_prompt, p, p_meta, gates, o_first, up_meta8,
                                     w_ao_b, w_pool_b, w_out_b, gffn, w_up_b, conv_w, convb, w_down_b, gfin)

    kv4 = lambda a, n: a.reshape(a.shape[0], n, N_KV_HEADS, HEAD_DIM)
    dec = lambda a: a[N_META:].reshape(dbsz, t_dec, -1)
    return (
        y_prompt,
        dec(y_side),
        kv4(k[:, seq - WINDOW:], WINDOW),
        kv4(v[:, seq - WINDOW:], WINDOW),
        jnp.broadcast_to(km.reshape(1, N_META, N_KV_HEADS, HEAD_DIM), (bsz, N_META, N_KV_HEADS, HEAD_DIM)),
        jnp.broadcast_to(vm.reshape(1, N_META, N_KV_HEADS, HEAD_DIM), (bsz, N_META, N_KV_HEADS, HEAD_DIM)),
        p[:, seq - POOL_HIST:],
        up_tail[:, SUBLANES - (CONV_W - 1):],
        kv4(dec(k_side), t_dec),
        kv4(dec(v_side), t_dec),
        dec(p_side)[:, t_dec - POOL_HIST:],
        dec(up_side)[:, t_dec - (CONV_W - 1):],
    )
```

```python
import functools

import numpy as np
import jax
import jax.numpy as jnp
from jax import lax
from jax.experimental import pallas as pl
from jax.experimental.pallas import tpu as pltpu

D_MODEL = 1024
N_META = 16
CHUNK = 64
HEAD_DIM = 64
N_Q_HEADS = 16
N_KV_HEADS = 2
WINDOW = 128
ROPE_THETA = 10000.0
POOL_WINDOWS = (2, 4, 8, 16)
N_POOL_GROUPS = 4
POOL_WIDTH = D_MODEL // 2
POOL_GROUP_W = POOL_WIDTH // N_POOL_GROUPS
POOL_HIST = max(POOL_WINDOWS) - 1
Q_W = N_Q_HEADS * HEAD_DIM
KV_W = N_KV_HEADS * HEAD_DIM
GATE_W = 2 * D_MODEL
IN_W = Q_W + 2 * KV_W + POOL_WIDTH + GATE_W
D_FF = ((8 * D_MODEL // 3) + 127) // 128 * 128
CONV_W = 3
RMS_EPS = 1e-6
PAST_LEN = 1024

LANES = 128
SUBLANES = 8
BF16_SUBLANES = 16
Q_BLOCKS = Q_W // LANES
BLOCKS_PER_KV = Q_BLOCKS // N_KV_HEADS
NEG = -1e30
LOG2E = 1.4426950408889634

K_OFF = Q_W
V_OFF = Q_W + KV_W
P_OFF = Q_W + 2 * KV_W
G_OFF = P_OFF + POOL_WIDTH

F32 = jnp.float32
BF16 = jnp.bfloat16

VMEM_LIMIT = 60 * 1024 * 1024


def _rmsnorm(x, g):
    ms = jnp.mean(x * x, axis=-1, keepdims=True)
    return (x * lax.rsqrt(ms + RMS_EPS)) * g


def _dot(a, b):
    return jnp.dot(a, b, preferred_element_type=F32)


def _rope(x, cos, sin_signed):
    lane = lax.broadcasted_iota(jnp.int32, (x.shape[0], LANES), 1)
    first_half = (lane & (HEAD_DIM // 2)) == 0
    outs = []
    for b in range(x.shape[1] // LANES):
        xb = x[:, b * LANES:(b + 1) * LANES]
        partner = jnp.where(first_half, pltpu.roll(xb, LANES - HEAD_DIM // 2, 1), pltpu.roll(xb, HEAD_DIM // 2, 1))
        outs.append(xb * cos + partner * sin_signed)
    return outs[0] if len(outs) == 1 else jnp.concatenate(outs, axis=1)


def _kv_variants(kv, dtype=BF16):
    lane = lax.broadcasted_iota(jnp.int32, kv.shape, 1)
    low = lane < HEAD_DIM
    swapped = pltpu.roll(kv, HEAD_DIM, 1)
    zero = jnp.zeros_like(kv)
    return (
        (jnp.where(low, kv, zero).astype(dtype), jnp.where(low, zero, swapped).astype(dtype)),
        (jnp.where(low, swapped, zero).astype(dtype), jnp.where(low, zero, kv).astype(dtype)),
    )


def _attend_group(qg, kst, vst, tab):
    s = lax.dot_general(qg, kst, (((1,), (1,)), ((), ())), preferred_element_type=F32) + tab
    npad = kst.shape[0] // 2
    se, so = s[:, :npad], s[:, npad:]
    pe = jnp.exp(se - jnp.max(se, axis=1, keepdims=True))
    po = jnp.exp(so - jnp.max(so, axis=1, keepdims=True))
    le = jnp.sum(pe, axis=1, keepdims=True)
    lo = jnp.sum(po, axis=1, keepdims=True)
    p = jnp.concatenate([pe, po], axis=1).astype(BF16)
    o = _dot(p, vst)
    lane = lax.broadcasted_iota(jnp.int32, o.shape, 1)
    return o * jnp.where(lane < HEAD_DIM, 1.0 / le, 1.0 / lo)


def _pool_group(ext, g, cnt=None):
    w = POOL_WINDOWS[g]
    s = ext
    k = 1
    while k < w:
        s = s + pltpu.roll(s, k, 0)
        k *= 2
    cur = ext[N_META:]
    return s[N_META:] / (float(w) if cnt is None else cnt) - cur


def _pool_project(pm, wgrp_ref, pscale):
    outs = []
    for g in range(N_POOL_GROUPS):
        sl = slice(g * POOL_GROUP_W, (g + 1) * POOL_GROUP_W)
        outs.append(_dot(pm[:, sl].astype(BF16), wgrp_ref[g].astype(BF16)))
    return (jnp.concatenate(outs, axis=1) * pscale).astype(BF16)


def _conv3(u, prev8, w, b):
    t = u.shape[0]
    ext = jnp.concatenate([prev8, u], axis=0)
    u1 = pltpu.roll(ext, 1, 0)[SUBLANES:SUBLANES + t]
    u2 = pltpu.roll(ext, 2, 0)[SUBLANES:SUBLANES + t]
    c = b + u2 * w[0:1]
    c = c + u1 * w[1:2]
    return c + u * w[2:3]


def _sigmoid(x):
    return 0.5 * jnp.tanh(0.5 * x) + 0.5


def _silu(x):
    half = 0.5 * x
    return half + half * jnp.tanh(half)


SIDE_FF_CHUNK = 256


SIDE_HBM_WEIGHTS = ("w_in", "w_ao", "w_po", "w_out", "w_up", "w_down")
SIDE_HBM_ARGS = (12, 14, 17, 18, 20, 23)


def _side_kernel(xs_ref, ck_ref, cv_ref, cmk_ref, cmv_ref, spool_ref, sconv_ref, cos_ref, sin_ref, tabm_ref,
                 tabs_ref, gmix_ref, w_in_hbm, bgate_ref, w_ao_hbm, wgrp_ref, pscale_ref, w_po_hbm, w_out_hbm,
                 gffn_ref, w_up_hbm, convw_ref, convb_ref, w_down_hbm, gfin_ref,
                 q0_ref, k0_ref, v0_ref, kaug_ref, qaug_ref,
                 y_ref, k_ref, v_ref, p_ref, up_ref, o0_ref, km_ref, vm_ref, pm_ref, upm_ref,
                 w_in_ref, w_ao_ref, w_po_ref, w_out_ref, w_up_ref, w_down_ref, dma_sems):
    hbm = (w_in_hbm, w_ao_hbm, w_po_hbm, w_out_hbm, w_up_hbm, w_down_hbm)
    vmem = (w_in_ref, w_ao_ref, w_po_ref, w_out_ref, w_up_ref, w_down_ref)
    copies = {name: pltpu.make_async_copy(src, dst, dma_sems.at[n])
              for n, (name, src, dst) in enumerate(zip(SIDE_HBM_WEIGHTS, hbm, vmem))}
    for name in SIDE_HBM_WEIGHTS:
        copies[name].start()

    n_side = xs_ref.shape[0]
    dec_b = ck_ref.shape[0]
    t_dec = (n_side - N_META) // dec_b
    x = xs_ref[...]
    h = _rmsnorm(x, gmix_ref[...]).astype(BF16)
    cos, sin = cos_ref[...], sin_ref[...]
    copies["w_in"].wait()
    w_in = lambda c0, c1: w_in_ref[:, c0:c1]
    q = _rope(_dot(h, w_in(0, Q_W)) * (HEAD_DIM ** -0.5), cos, sin).astype(BF16)
    k = _rope(_dot(h, w_in(K_OFF, K_OFF + KV_W)), cos, sin)
    v = _dot(h, w_in(V_OFF, V_OFF + KV_W))
    p = _dot(h, w_in(P_OFF, P_OFF + POOL_WIDTH))
    gates = _sigmoid(jnp.concatenate([_dot(h, w_in(G_OFF + c0, G_OFF + c0 + GATE_CHUNK))
                                      for c0 in range(0, GATE_W, GATE_CHUNK)], axis=1) + bgate_ref[...])
    k_ref[...] = k
    v_ref[...] = v
    p_ref[...] = p
    km_ref[...] = k[:N_META]
    vm_ref[...] = v[:N_META]
    pm_ref[...] = p[:N_META]

    def attend_rows(r0, nrows, kx, vx, tab_ref):
        kvar, vvar = _kv_variants(kx), _kv_variants(vx)
        blocks = [None] * Q_BLOCKS
        for n in range(N_KV_HEADS):
            qg = jnp.concatenate([q[r0:r0 + nrows, (BLOCKS_PER_KV * n + bi) * LANES:(BLOCKS_PER_KV * n + bi + 1) * LANES]
                                  for bi in range(BLOCKS_PER_KV)], axis=0)
            kst = jnp.concatenate(kvar[n], axis=0)
            vst = jnp.concatenate(vvar[n], axis=0)
            o = _attend_group(qg, kst, vst, tab_ref[n])
            for bi in range(BLOCKS_PER_KV):
                blocks[BLOCKS_PER_KV * n + bi] = o[bi * nrows:(bi + 1) * nrows]
        return jnp.concatenate(blocks, axis=1)

    zpad = jnp.zeros((LANES - N_META, LANES), F32)
    o_rows = [attend_rows(0, N_META, jnp.concatenate([k[:N_META], zpad], axis=0),
                          jnp.concatenate([v[:N_META], zpad], axis=0), tabm_ref)]
    n_keys = N_META + WINDOW + t_dec
    zpad = jnp.zeros((2 * LANES - n_keys, LANES), F32)
    for b in range(dec_b):
        r0 = N_META + b * t_dec
        kx = jnp.concatenate([cmk_ref[b], ck_ref[b], k[r0:r0 + t_dec], zpad], axis=0)
        vx = jnp.concatenate([cmv_ref[b], cv_ref[b], v[r0:r0 + t_dec], zpad], axis=0)
        o_rows.append(attend_rows(r0, t_dec, kx, vx, tabs_ref))
    o_attn = jnp.concatenate(o_rows, axis=0).astype(BF16)

    row = lax.broadcasted_iota(jnp.int32, (N_META, LANES), 0)
    pm_rows = []
    for seg in range(1 + dec_b):
        if seg == 0:
            ext = jnp.concatenate([jnp.zeros((N_META, POOL_WIDTH), F32), p[:N_META]], axis=0)
        else:
            r0 = N_META + (seg - 1) * t_dec
            ext = jnp.concatenate([spool_ref[seg - 1], p[r0:r0 + t_dec]], axis=0)
        groups = []
        for g in range(N_POOL_GROUPS):
            cnt = jnp.minimum(POOL_WINDOWS[g], row + 1).astype(F32) if seg == 0 else None
            groups.append(_pool_group(ext[:, g * POOL_GROUP_W:(g + 1) * POOL_GROUP_W], g, cnt))
        pm_rows.append(jnp.concatenate(groups, axis=1))
    pm = jnp.concatenate(pm_rows, axis=0)

    pool = _pool_project(pm, wgrp_ref, pscale_ref[...])
    for name in ("w_ao", "w_po", "w_out"):
        copies[name].wait()
    mix = gates[:, :D_MODEL] * _dot(o_attn, w_ao_ref[...]) + gates[:, D_MODEL:] * _dot(pool, w_po_ref[...])
    x1 = x + _dot(mix.astype(BF16), w_out_ref[...])

    hn = _rmsnorm(x1, gffn_ref[...]).astype(BF16)
    acc = jnp.zeros((n_side, D_MODEL), F32)
    zprev = jnp.zeros((SUBLANES, SIDE_FF_CHUNK), F32)
    copies["w_up"].wait()
    copies["w_down"].wait()
    for c0 in range(0, D_FF, SIDE_FF_CHUNK):
        halves = []
        for off in (c0, D_FF + c0):
            cs = slice(off, off + SIDE_FF_CHUNK)
            u = _dot(hn, w_up_ref[:, cs])
            up_ref[:, cs] = u
            w, bias = convw_ref[:, cs], convb_ref[:, cs]
            segs = [_conv3(u[:N_META], zprev, w, bias)]
            for b in range(dec_b):
                r0 = N_META + b * t_dec
                segs.append(_conv3(u[r0:r0 + t_dec], sconv_ref[b, :, cs], w, bias))
            halves.append(jnp.concatenate(segs, axis=0))
        act = (_silu(halves[0]) * halves[1]).astype(BF16)
        acc = acc + _dot(act, w_down_ref[c0:c0 + SIDE_FF_CHUNK, :])
    y_ref[...] = _rmsnorm(x1 + acc, gfin_ref[...])
    upm_ref[...] = up_ref[N_META - SUBLANES:N_META]

    def store_o0(j, blk, value):
        o0_ref[j * CHUNK:(j + 1) * CHUNK, blk * LANES:(blk + 1) * LANES] = value

    groups, scores, finish = _attention_stages(0, q0_ref, k0_ref, k0_ref.at[0:WINDOW], v0_ref, v0_ref.at[0:WINDOW],
                                               k_ref.at[0:N_META], v_ref.at[0:N_META], kaug_ref, qaug_ref, store_o0)
    pending = scores(*groups[0])
    for g, group in enumerate(groups):
        current = pending
        if g + 1 < len(groups):
            pending = scores(*groups[g + 1])
        finish(*group, *current)


def _side_call(args, q, k, v, kaug, qaug):
    n_side = args[0].shape[0]
    out_shape = (
        jax.ShapeDtypeStruct((n_side, D_MODEL), F32),
        jax.ShapeDtypeStruct((n_side, KV_W), F32),
        jax.ShapeDtypeStruct((n_side, KV_W), F32),
        jax.ShapeDtypeStruct((n_side, POOL_WIDTH), F32),
        jax.ShapeDtypeStruct((n_side, 2 * D_FF), F32),
        jax.ShapeDtypeStruct((TQ, Q_W), BF16),
        jax.ShapeDtypeStruct((N_META, KV_W), F32),
        jax.ShapeDtypeStruct((N_META, KV_W), F32),
        jax.ShapeDtypeStruct((N_META, POOL_WIDTH), F32),
        jax.ShapeDtypeStruct((SUBLANES, 2 * D_FF), F32),
    )
    first_tile = lambda w: pl.BlockSpec((None, TQ, w), lambda i: (0, 0, 0))
    hbm_weights = [args[n] for n in SIDE_HBM_ARGS]
    assert [w.shape for w in hbm_weights] == [(D_MODEL, IN_W), (Q_W, D_MODEL), (POOL_WIDTH, D_MODEL),
                                              (D_MODEL, D_MODEL), (D_MODEL, 2 * D_FF), (D_FF, D_MODEL)]
    arg_specs = [pl.BlockSpec(memory_space=pl.ANY) if n in SIDE_HBM_ARGS else _const_spec(a.shape)
                 for n, a in enumerate(args)]
    return pl.pallas_call(
        _side_kernel,
        grid=(1,),
        in_specs=arg_specs + [first_tile(Q_W), first_tile(KV_W), first_tile(KV_W),
                              _const_spec(kaug.shape), _const_spec(qaug.shape)],
        out_specs=tuple(_const_spec(s.shape) for s in out_shape),
        out_shape=out_shape,
        scratch_shapes=[pltpu.VMEM(w.shape, w.dtype) for w in hbm_weights]
                       + [pltpu.SemaphoreType.DMA((len(hbm_weights),))],
        compiler_params=pltpu.CompilerParams(dimension_semantics=("arbitrary",), vmem_limit_bytes=VMEM_LIMIT),
        name="side_rows",
    )(*args, q, k, v, kaug, qaug)


TM_IN = 1024
IN_SUB = 256
GATE_CHUNK = 512


W_IN_CAST_CHUNK = 768


def _inproj_kernel(n_cast, x_ref, cos_ref, sin_ref, gmix_ref, w_in_f32_ref, bgate_ref, *refs):
    cast_in, (q_ref, k_ref, v_ref, p_ref, g_ref) = refs[:n_cast], refs[n_cast:n_cast + 5]
    cast_out, w_in_ref = refs[n_cast + 5:2 * n_cast + 5], refs[2 * n_cast + 5]

    @pl.when((pl.program_id(0) == 0) & (pl.program_id(1) == 0))
    def _():
        for c0 in range(0, IN_W, W_IN_CAST_CHUNK):
            w_in_ref[:, c0:c0 + W_IN_CAST_CHUNK] = w_in_f32_ref[:, c0:c0 + W_IN_CAST_CHUNK].astype(BF16)

    for src, dst in zip(cast_in, cast_out):
        dst[...] = src[...].astype(BF16)

    subs = [slice(r0, r0 + IN_SUB) for r0 in range(0, x_ref.shape[0], IN_SUB)]
    hs = [_rmsnorm(x_ref[rows], gmix_ref[...]).astype(BF16) for rows in subs]
    for rows, h in zip(subs, hs):
        cos, sin = cos_ref[rows], sin_ref[rows]
        for c0 in range(0, GATE_W, GATE_CHUNK):
            z = _dot(h, w_in_ref[:, G_OFF + c0:G_OFF + c0 + GATE_CHUNK]) + bgate_ref[:, c0:c0 + GATE_CHUNK]
            g_ref[rows, c0:c0 + GATE_CHUNK] = _sigmoid(z).astype(BF16)
        q_ref[rows] = _rope(_dot(h, w_in_ref[:, 0:Q_W]) * (HEAD_DIM ** -0.5 * LOG2E), cos, sin).astype(BF16)
        p_ref[rows] = _dot(h, w_in_ref[:, P_OFF:P_OFF + POOL_WIDTH])
        kv = _dot(h, w_in_ref[:, K_OFF:K_OFF + 2 * KV_W])
        k_ref[rows] = _rope(kv[:, :KV_W], cos, sin)
        v_ref[rows] = kv[:, KV_W:]


def _const_spec(shape):
    nd = len(shape)
    return pl.BlockSpec(shape, lambda *_: (0,) * nd)


def _inproj_call(x, cos, sin, gmix, w_in, bgate, other_weights):
    bsz, seq, _ = x.shape
    tm = TM_IN
    tiles = seq // tm
    n_steps = bsz * tiles
    row = lambda w: pl.BlockSpec((None, tm, w), lambda b, i: (b, i, 0))

    def slab_spec(w):
        n_slabs = n_steps
        while w.shape[0] % (n_slabs * BF16_SUBLANES):
            n_slabs //= 2
        return pl.BlockSpec((w.shape[0] // n_slabs, w.shape[1]),
                            lambda b, i: (jnp.minimum(b * tiles + i, n_slabs - 1), 0))

    slabs = [slab_spec(w) for w in other_weights]
    outs = pl.pallas_call(
        functools.partial(_inproj_kernel, len(other_weights)),
        grid=(bsz, tiles),
        in_specs=[row(D_MODEL), pl.BlockSpec((tm, LANES), lambda b, i: (i, 0)),
                  pl.BlockSpec((tm, LANES), lambda b, i: (i, 0)),
                  _const_spec(gmix.shape), _const_spec(w_in.shape), _const_spec(bgate.shape)] + slabs,
        out_specs=(row(Q_W), row(KV_W), row(KV_W), row(POOL_WIDTH), row(GATE_W), *slabs),
        out_shape=(jax.ShapeDtypeStruct((bsz, seq, Q_W), BF16), jax.ShapeDtypeStruct((bsz, seq, KV_W), F32),
                   jax.ShapeDtypeStruct((bsz, seq, KV_W), F32), jax.ShapeDtypeStruct((bsz, seq, POOL_WIDTH), F32),
                   jax.ShapeDtypeStruct((bsz, seq, GATE_W), BF16),
                   *[jax.ShapeDtypeStruct(w.shape, BF16) for w in other_weights]),
        scratch_shapes=[pltpu.VMEM(w_in.shape, BF16)],
        compiler_params=pltpu.CompilerParams(dimension_semantics=("arbitrary", "arbitrary"),
                                             vmem_limit_bytes=VMEM_LIMIT),
        name="in_proj",
    )(x, cos, sin, gmix, w_in, bgate, *other_weights)
    return outs[:5], outs[5:]


TQ = 512
BAND = WINDOW + CHUNK
KEYS_PAD = BAND + N_META + BF16_SUBLANES
N_TABS = WINDOW // CHUNK + 1
SUM_ROWS = BF16_SUBLANES


def _attention_stages(i, q_ref, kc_ref, kp_ref, vc_ref, vp_ref, mk_ref, mv_ref, kaug_ref, qaug_ref, store):
    kvar = _kv_variants(jnp.concatenate([kp_ref[...], kc_ref[...]], axis=0), BF16)
    vvar = _kv_variants(jnp.concatenate([vp_ref[...], vc_ref[...]], axis=0), F32)
    mkvar, mvvar = _kv_variants(mk_ref[...], BF16), _kv_variants(mv_ref[...], F32)
    kpad = jnp.zeros((KEYS_PAD - BAND - N_META, LANES), BF16)
    vpad = jnp.zeros((KEYS_PAD - BAND - N_META, LANES), F32)
    row = lax.broadcasted_iota(jnp.int32, (SUM_ROWS, 2 * KEYS_PAD), 0)
    col = lax.broadcasted_iota(jnp.int32, (SUM_ROWS, 2 * KEYS_PAD), 1)
    ones = jnp.where((row == 0) & (col < KEYS_PAD) | (row == 1) & (col >= KEYS_PAD), 1.0, 0.0).astype(BF16)

    def scores(j, n):
        tsel = jnp.minimum(i * (TQ // CHUNK) + j, N_TABS - 1)
        rows = slice(j * CHUNK, (j + 1) * CHUNK)
        band = slice(j * CHUNK, j * CHUNK + BAND)
        kst = jnp.concatenate([kvar[n][0][band], mkvar[n][0], kpad, kvar[n][1][band], mkvar[n][1], kpad], axis=0)
        vst = jnp.concatenate([vvar[n][0][band], mvvar[n][0], vpad, vvar[n][1][band], mvvar[n][1], vpad], axis=0)
        qg = jnp.concatenate([q_ref[rows, (BLOCKS_PER_KV * n + bi) * LANES:(BLOCKS_PER_KV * n + bi + 1) * LANES]
                              for bi in range(BLOCKS_PER_KV)], axis=0)
        s = lax.dot_general(jnp.concatenate([kst, kaug_ref[tsel]], axis=1),
                            jnp.concatenate([qg, qaug_ref[n]], axis=1),
                            (((1,), (1,)), ((), ())), preferred_element_type=F32)
        vt = jnp.concatenate([vst.T.astype(BF16), ones], axis=0)
        return s, vt

    def finish(j, n, s, vt):
        se, so = s[:KEYS_PAD], s[KEYS_PAD:]
        pe = jnp.exp2(se - jnp.max(se, axis=0, keepdims=True))
        po = jnp.exp2(so - jnp.max(so, axis=0, keepdims=True))
        o = _dot(vt, jnp.concatenate([pe, po], axis=0).astype(BF16))
        o = jnp.concatenate([o[:HEAD_DIM] / o[LANES:LANES + 1], o[HEAD_DIM:LANES] / o[LANES + 1:LANES + 2]], axis=0).T
        for bi in range(BLOCKS_PER_KV):
            store(j, BLOCKS_PER_KV * n + bi, o[bi * CHUNK:(bi + 1) * CHUNK].astype(BF16))

    groups = [(j, n) for j in range(TQ // CHUNK) for n in range(N_KV_HEADS)]
    return groups, scores, finish


def _attn_aug_tables(sinks):
    n_keys = BAND + N_META
    r = np.arange(KEYS_PAD)
    lane = np.arange(LANES)[None, :]
    kaug = []
    for c in range(N_TABS):
        masked = (r < (WINDOW // CHUNK - c) * CHUNK) | (r > n_keys)
        mask_col = np.where(masked, NEG, 0.0)[:, None] * (lane == 0)
        sink_row = (r == n_keys)[:, None]
        even = mask_col + np.where(sink_row & ((lane == 1) | (lane == 2)), 1.0, 0.0)
        odd = mask_col + np.where(sink_row & ((lane == 3) | (lane == 4)), 1.0, 0.0)
        kaug.append(np.concatenate([even, odd], axis=0))
    s2 = sinks.astype(F32) * LOG2E
    hi = s2.astype(BF16).astype(F32)
    lo = s2 - hi
    head = 2 * jnp.arange(Q_BLOCKS)
    cols = [jnp.ones((Q_BLOCKS,), F32), hi[head], lo[head], hi[head + 1], lo[head + 1]]
    qaug = jnp.stack(cols, axis=1)
    qaug = jnp.pad(qaug, ((0, 0), (0, LANES - qaug.shape[1])))
    qaug = jnp.broadcast_to(qaug[:, None, :], (Q_BLOCKS, CHUNK, LANES)).reshape(N_KV_HEADS, BLOCKS_PER_KV * CHUNK, LANES)
    return jnp.asarray(np.stack(kaug), dtype=BF16), qaug.astype(BF16)


TM_POST = 512
POST_SUB = 256
FF_CHUNK = 256
FF_SKEW = 3
FF_LANE_BLOCKS = FF_CHUNK // LANES
DOWN_GROUP_ENDS = (4, 8, 10, 11)
assert DOWN_GROUP_ENDS[-1] == D_FF // FF_CHUNK


def _pool_weight_kernel(wgrp_ref, pscale_ref, w_po_ref, out_ref):
    for g in range(N_POOL_GROUPS):
        sl = slice(g * POOL_GROUP_W, (g + 1) * POOL_GROUP_W)
        out_ref[sl, :] = _dot((wgrp_ref[g] * pscale_ref[:, sl]).astype(BF16), w_po_ref[sl, :]).astype(BF16)


def _pool_weight_call(wgrp, pscale, w_po):
    return pl.pallas_call(
        _pool_weight_kernel,
        out_shape=jax.ShapeDtypeStruct((POOL_WIDTH, D_MODEL), BF16),
        name="pool_weights",
    )(wgrp, pscale, w_po)


def _stream_kernel(tiles_per_batch, n_tiles,
                   q_ref, kc_ref, kp_ref, vc_ref, vp_ref, mk_ref, mv_ref, kaug_ref, qaug_ref,
                   x_ref, p_ref, pprev_ref, pmeta_ref, g_ref, upmeta_ref, w_ao_ref, w_pool_ref,
                   w_out_ref, gffn_ref, w_up_ref, convw_ref, convb_ref, w_down_ref, gfin_ref,
                   o0_ref, y_ref, carry_ref, oattn_ref, pext_ref, x1_ref, hn_ref, act_ref, *uext_refs):
    t = pl.program_id(0)
    i_attn = lax.rem(jnp.minimum(t + 1, n_tiles - 1), tiles_per_batch)
    i = lax.rem(t, tiles_per_batch)
    slot_new, slot_old = lax.rem(t + 1, 2), lax.rem(t, 2)
    tm = x_ref.shape[0]
    n_chunks = D_FF // FF_CHUNK

    @pl.when(t == 0)
    def _():
        oattn_ref[0] = o0_ref[...]

    @pl.when(i == 0)
    def _():
        pext_ref[0:N_META] = pmeta_ref[...]
        for c in range(n_chunks):
            for half, off in enumerate((c * FF_CHUNK, D_FF + c * FF_CHUNK)):
                for lb in range(FF_LANE_BLOCKS):
                    uext_refs[c][half * FF_LANE_BLOCKS + lb, POST_SUB:POST_SUB + SUBLANES] = (
                        upmeta_ref[:, off + lb * LANES:off + (lb + 1) * LANES])

    @pl.when(i > 0)
    def _():
        pext_ref[0:N_META] = pprev_ref[...]

    def store_attn(j, blk, value):
        oattn_ref[slot_new, j * CHUNK:(j + 1) * CHUNK, blk * LANES:(blk + 1) * LANES] = value

    groups, scores, finish = _attention_stages(i_attn, q_ref, kc_ref, kp_ref, vc_ref, vp_ref, mk_ref, mv_ref,
                                               kaug_ref, qaug_ref, store_attn)
    pending = [None]

    def attention_group(g):
        current = pending[0]
        if g + 1 < len(groups):
            pending[0] = scores(*groups[g + 1])
        finish(*groups[g], *current)

    pext_ref[N_META:] = p_ref[...]
    subs = [slice(r0, r0 + POST_SUB) for r0 in range(0, tm, POST_SUB)]

    mixes = []
    for rows in subs:
        attn = _dot(oattn_ref[slot_old, rows], w_ao_ref[...])
        ext_rows = slice(rows.start, rows.stop + N_META)
        pm = jnp.concatenate([_pool_group(pext_ref[ext_rows, g * POOL_GROUP_W:(g + 1) * POOL_GROUP_W], g)
                              for g in range(N_POOL_GROUPS)], axis=1)
        pool = _dot(pm.astype(BF16), w_pool_ref[...])
        mix = g_ref[rows, :D_MODEL].astype(F32) * attn + g_ref[rows, D_MODEL:].astype(F32) * pool
        mixes.append(mix.astype(BF16))
    for rows, mix in zip(subs, mixes):
        x1 = x_ref[rows] + _dot(mix, w_out_ref[...])
        x1_ref[rows] = x1
        hn_ref[rows] = _rmsnorm(x1, gffn_ref[...]).astype(BF16)

    def up_project(rows, c):
        for half, off in enumerate((c * FF_CHUNK, D_FF + c * FF_CHUNK)):
            u = _dot(hn_ref[rows], w_up_ref[:, off:off + FF_CHUNK])
            for lb in range(FF_LANE_BLOCKS):
                blk = uext_refs[c].at[half * FF_LANE_BLOCKS + lb]
                blk[0:SUBLANES] = blk[POST_SUB:POST_SUB + SUBLANES]
                blk[SUBLANES:] = u[:, lb * LANES:(lb + 1) * LANES]
            if rows.stop == tm:
                carry_ref[:, off:off + FF_CHUNK] = u[POST_SUB - SUBLANES:]

    def conv(rows, c, half):
        outs = []
        for lb in range(FF_LANE_BLOCKS):
            off = half * D_FF + c * FF_CHUNK + lb * LANES
            w, b = convw_ref[:, off:off + LANES], convb_ref[:, off:off + LANES]
            ext = uext_refs[c].at[half * FF_LANE_BLOCKS + lb]
            cv = b + ext[SUBLANES - 2:SUBLANES - 2 + POST_SUB] * w[0:1]
            cv = cv + ext[SUBLANES - 1:SUBLANES - 1 + POST_SUB] * w[1:2]
            outs.append(cv + ext[SUBLANES:] * w[2:3])
        return jnp.concatenate(outs, axis=1)

    steps = [(rows, c) for rows in subs for c in range(n_chunks)]
    for step in steps[:FF_SKEW]:
        up_project(*step)
    pending[0] = scores(*groups[0])
    acc = None
    for j, (rows, c) in enumerate(steps):
        act_ref[rows, c * FF_CHUNK:(c + 1) * FF_CHUNK] = (_silu(conv(rows, c, 0)) * conv(rows, c, 1)).astype(BF16)
        if j + FF_SKEW < len(steps):
            up_project(*steps[j + FF_SKEW])
        if c + 1 in DOWN_GROUP_ENDS:
            first = ([0] + list(DOWN_GROUP_ENDS))[DOWN_GROUP_ENDS.index(c + 1)]
            ks = slice(first * FF_CHUNK, (c + 1) * FF_CHUNK)
            d = _dot(act_ref[rows, ks], w_down_ref[ks, :])
            acc = d if first == 0 else acc + d
        if c == n_chunks - 1:
            y_ref[rows] = _rmsnorm(x1_ref[rows] + acc, gfin_ref[...])
        if j < len(groups):
            attention_group(j)
    assert len(groups) <= len(steps)


def _stream_call(q, k, v, mk, mv, kaug, qaug, x, p, p_meta, gates, o0, up_meta8, w_ao, w_pool, w_out, gffn, w_up,
                 convw, convb, w_down, gfin):
    bsz, seq, _ = x.shape
    tm = TM_POST
    assert tm == TQ
    tiles_per_batch = seq // tm
    n_tiles = bsz * tiles_per_batch

    def new_tile(t):
        a = jnp.minimum(t + 1, n_tiles - 1)
        return a // tiles_per_batch, lax.rem(a, tiles_per_batch)

    def old_tile(t):
        return t // tiles_per_batch, lax.rem(t, tiles_per_batch)

    new_row = lambda w: pl.BlockSpec((None, tm, w), lambda t: (*new_tile(t), 0))
    old_row = lambda w: pl.BlockSpec((None, tm, w), lambda t: (*old_tile(t), 0))

    def kv_prev_map(t):
        b, i = new_tile(t)
        return b, jnp.maximum(i * (tm // WINDOW) - 1, 0), 0

    def p_prev_map(t):
        b, i = old_tile(t)
        return b, jnp.maximum(i * (tm // N_META) - 1, 0), 0

    kv_prev = pl.BlockSpec((None, WINDOW, KV_W), kv_prev_map)
    consts = (up_meta8, w_ao, w_pool, w_out, gffn, w_up, convw, convb, w_down, gfin, o0)
    return pl.pallas_call(
        functools.partial(_stream_kernel, tiles_per_batch, n_tiles),
        grid=(n_tiles,),
        in_specs=[new_row(Q_W), new_row(KV_W), kv_prev, new_row(KV_W), kv_prev, _const_spec(mk.shape),
                  _const_spec(mv.shape), _const_spec(kaug.shape), _const_spec(qaug.shape),
                  old_row(D_MODEL), old_row(POOL_WIDTH), pl.BlockSpec((None, N_META, POOL_WIDTH), p_prev_map),
                  _const_spec(p_meta.shape), old_row(GATE_W)] + [_const_spec(a.shape) for a in consts],
        out_specs=(old_row(D_MODEL), pl.BlockSpec((None, SUBLANES, 2 * D_FF), lambda t: (old_tile(t)[0], 0, 0))),
        out_shape=(jax.ShapeDtypeStruct((bsz, seq, D_MODEL), F32),
                   jax.ShapeDtypeStruct((bsz, SUBLANES, 2 * D_FF), F32)),
        scratch_shapes=[pltpu.VMEM((2, tm, Q_W), BF16), pltpu.VMEM((N_META + tm, POOL_WIDTH), F32),
                        pltpu.VMEM((tm, D_MODEL), F32), pltpu.VMEM((tm, D_MODEL), BF16),
                        pltpu.VMEM((tm, D_FF), BF16)]
                       + [pltpu.VMEM((2 * FF_LANE_BLOCKS, SUBLANES + POST_SUB, LANES), F32)
                          for _ in range(D_FF // FF_CHUNK)],
        compiler_params=pltpu.CompilerParams(dimension_semantics=("arbitrary",), vmem_limit_bytes=VMEM_LIMIT),
        name="attn_mixer_ffn",
    )(q, k, k, v, v, mk, mv, kaug, qaug, x, p, p, p_meta, gates, *consts)


def _rope_tables(pos):
    half = HEAD_DIM // 2
    inv = ROPE_THETA ** (-np.arange(half, dtype=np.float64) / half)
    ang = np.asarray(pos, dtype=np.float64)[:, None] * inv[None, :]
    cos, sin = np.cos(ang), np.sin(ang)
    cos = np.tile(cos, (1, LANES // half))
    sin = np.tile(np.concatenate([-sin, sin], axis=1), (1, LANES // HEAD_DIM))
    return jnp.asarray(cos, dtype=F32), jnp.asarray(sin, dtype=F32)


def _score_table(sinks, rows_per_block, n_keys, keys_pad, masked_prefix=0):
    col = jnp.arange(keys_pad)
    base = jnp.where((col >= masked_prefix) & (col < n_keys), 0.0, NEG).astype(F32)
    head = (2 * jnp.arange(Q_BLOCKS)[:, None] + jnp.arange(2)[None, :])
    tab = jnp.where(col[None, None, :] == n_keys, sinks.astype(F32)[head][:, :, None], base[None, None, :])
    tab = tab.reshape(N_KV_HEADS, BLOCKS_PER_KV, 1, 2 * keys_pad)
    tab = jnp.broadcast_to(tab, (N_KV_HEADS, BLOCKS_PER_KV, rows_per_block, 2 * keys_pad))
    return tab.reshape(N_KV_HEADS, BLOCKS_PER_KV * rows_per_block, 2 * keys_pad)


def kernel(x_prompt, x_sample, cache_swa_k, cache_swa_v, cache_meta_k, cache_meta_v, state_pool, state_conv,
           meta_tokens, g_norm_mix, w_in, b_gate, sinks, w_attn_o, w_pool_grp, pool_scale, w_pool_o, w_out,
           g_norm_ffn, w_up, conv_w, conv_b, w_down, g_norm_final):
    bsz, seq, _ = x_prompt.shape
    dbsz, t_dec, _ = x_sample.shape
    row2 = lambda a: a.reshape(1, -1)
    gmix, gffn, gfin = row2(g_norm_mix), row2(g_norm_ffn), row2(g_norm_final)
    bgate, pscale, convb = row2(b_gate), row2(pool_scale), row2(conv_b)

    cos_p, sin_p = _rope_tables(N_META + np.arange(seq))
    (q, k, v, p, gates), (w_in_b, w_ao_b, w_po_b, w_out_b, w_up_b, w_down_b) = _inproj_call(
        x_prompt, cos_p, sin_p, gmix, w_in, bgate, (w_in, w_attn_o, w_pool_o, w_out, w_up, w_down))

    pos_side = np.concatenate([np.arange(N_META), np.tile(N_META + PAST_LEN + np.arange(t_dec), dbsz)])
    cos_s, sin_s = _rope_tables(pos_side)
    xs = jnp.concatenate([meta_tokens, x_sample.reshape(dbsz * t_dec, D_MODEL)], axis=0)
    spool16 = jnp.pad(state_pool, ((0, 0), (N_META - POOL_HIST, 0), (0, 0)))
    sconv8 = jnp.pad(state_conv, ((0, 0), (SUBLANES - (CONV_W - 1), 0), (0, 0)))
    tab_meta = _score_table(sinks, N_META, N_META, LANES)
    tab_dec = _score_table(sinks, t_dec, N_META + WINDOW + t_dec, 2 * LANES)
    kaug, qaug = _attn_aug_tables(sinks)
    y_side, k_side, v_side, p_side, up_side, o_first, km, vm, p_meta, up_meta8 = _side_call(
        (xs, cache_swa_k.reshape(dbsz, WINDOW, KV_W), cache_swa_v.reshape(dbsz, WINDOW, KV_W),
         cache_meta_k.reshape(dbsz, N_META, KV_W), cache_meta_v.reshape(dbsz, N_META, KV_W), spool16, sconv8,
         cos_s, sin_s, tab_meta, tab_dec, gmix, w_in_b, bgate, w_ao_b, w_pool_grp, pscale, w_po_b, w_out_b,
         gffn, w_up_b, conv_w, convb, w_down_b, gfin), q, k, v, kaug, qaug)
    w_pool_b = _pool_weight_call(w_pool_grp, pscale, w_po_b)
    y_prompt, up_tail = _stream_call(q, k, v, km, vm, kaug, qaug, x_prompt, p, p_meta, gates, o_first, up_meta8,
                                     w_ao_b, w_pool_b, w_out_b, gffn, w_up_b, conv_w, convb, w_down_b, gfin)

    kv4 = lambda a, n: a.reshape(a.shape[0], n, N_KV_HEADS, HEAD_DIM)
    dec = lambda a: a[N_META:].reshape(dbsz, t_dec, -1)
    return (
        y_prompt,
        dec(y_side),
        kv4(k[:, seq - WINDOW:], WINDOW),
        kv4(v[:, seq - WINDOW:], WINDOW),
        jnp.broadcast_to(km.reshape(1, N_META, N_KV_HEADS, HEAD_DIM), (bsz, N_META, N_KV_HEADS, HEAD_DIM)),
        jnp.broadcast_to(vm.reshape(1, N_META, N_KV_HEADS, HEAD_DIM), (bsz, N_META, N_KV_HEADS, HEAD_DIM)),
        p[:, seq - POOL_HIST:],
        up_tail[:, SUBLANES - (CONV_W - 1):],
        kv4(dec(k_side), t_dec),
        kv4(dec(v_side), t_dec),
        dec(p_side)[:, t_dec - POOL_HIST:],
        dec(up_side)[:, t_dec - (CONV_W - 1):],
    )
```

```python
import functools

import numpy as np
import jax
import jax.numpy as jnp
from jax import lax
from jax.experimental import pallas as pl
from jax.experimental.pallas import tpu as pltpu

D_MODEL = 1024
N_META = 16
CHUNK = 64
HEAD_DIM = 64
N_Q_HEADS = 16
N_KV_HEADS = 2
WINDOW = 128
ROPE_THETA = 10000.0
POOL_WINDOWS = (2, 4, 8, 16)
N_POOL_GROUPS = 4
POOL_WIDTH = D_MODEL // 2
POOL_GROUP_W = POOL_WIDTH // N_POOL_GROUPS
POOL_HIST = max(POOL_WINDOWS) - 1
Q_W = N_Q_HEADS * HEAD_DIM
KV_W = N_KV_HEADS * HEAD_DIM
GATE_W = 2 * D_MODEL
IN_W = Q_W + 2 * KV_W + POOL_WIDTH + GATE_W
D_FF = ((8 * D_MODEL // 3) + 127) // 128 * 128
CONV_W = 3
RMS_EPS = 1e-6
PAST_LEN = 1024

LANES = 128
SUBLANES = 8
Q_BLOCKS = Q_W // LANES
BLOCKS_PER_KV = Q_BLOCKS // N_KV_HEADS
NEG = -1e30
LOG2E = 1.4426950408889634

K_OFF = Q_W
V_OFF = Q_W + KV_W
P_OFF = Q_W + 2 * KV_W
G_OFF = P_OFF + POOL_WIDTH

F32 = jnp.float32
BF16 = jnp.bfloat16

VMEM_LIMIT = 60 * 1024 * 1024


def _rmsnorm(x, g):
    ms = jnp.mean(x * x, axis=-1, keepdims=True)
    return (x * lax.rsqrt(ms + RMS_EPS)) * g


def _dot(a, b):
    return jnp.dot(a, b, preferred_element_type=F32)


def _rope(x, cos, sin_signed):
    lane = lax.broadcasted_iota(jnp.int32, (x.shape[0], LANES), 1)
    first_half = (lane & (HEAD_DIM // 2)) == 0
    outs = []
    for b in range(x.shape[1] // LANES):
        xb = x[:, b * LANES:(b + 1) * LANES]
        partner = jnp.where(first_half, pltpu.roll(xb, LANES - HEAD_DIM // 2, 1), pltpu.roll(xb, HEAD_DIM // 2, 1))
        outs.append(xb * cos + partner * sin_signed)
    return outs[0] if len(outs) == 1 else jnp.concatenate(outs, axis=1)


def _kv_variants(kv, dtype=BF16):
    lane = lax.broadcasted_iota(jnp.int32, kv.shape, 1)
    low = lane < HEAD_DIM
    swapped = pltpu.roll(kv, HEAD_DIM, 1)
    zero = jnp.zeros_like(kv)
    return (
        (jnp.where(low, kv, zero).astype(dtype), jnp.where(low, zero, swapped).astype(dtype)),
        (jnp.where(low, swapped, zero).astype(dtype), jnp.where(low, zero, kv).astype(dtype)),
    )


def _attend_group(qg, kst, vst, tab):
    s = lax.dot_general(qg, kst, (((1,), (1,)), ((), ())), preferred_element_type=F32) + tab
    npad = kst.shape[0] // 2
    se, so = s[:, :npad], s[:, npad:]
    pe = jnp.exp(se - jnp.max(se, axis=1, keepdims=True))
    po = jnp.exp(so - jnp.max(so, axis=1, keepdims=True))
    le = jnp.sum(pe, axis=1, keepdims=True)
    lo = jnp.sum(po, axis=1, keepdims=True)
    p = jnp.concatenate([pe, po], axis=1).astype(BF16)
    o = _dot(p, vst)
    lane = lax.broadcasted_iota(jnp.int32, o.shape, 1)
    return o * jnp.where(lane < HEAD_DIM, 1.0 / le, 1.0 / lo)


def _pool_group(ext, g, cnt=None):
    w = POOL_WINDOWS[g]
    s = ext
    k = 1
    while k < w:
        s = s + pltpu.roll(s, k, 0)
        k *= 2
    cur = ext[N_META:]
    return s[N_META:] / (float(w) if cnt is None else cnt) - cur


def _pool_project(pm, wgrp_ref, pscale):
    outs = []
    for g in range(N_POOL_GROUPS):
        sl = slice(g * POOL_GROUP_W, (g + 1) * POOL_GROUP_W)
        outs.append(_dot(pm[:, sl].astype(BF16), wgrp_ref[g].astype(BF16)))
    return (jnp.concatenate(outs, axis=1) * pscale).astype(BF16)


def _conv3(u, prev8, w, b):
    t = u.shape[0]
    ext = jnp.concatenate([prev8, u], axis=0)
    u1 = pltpu.roll(ext, 1, 0)[SUBLANES:SUBLANES + t]
    u2 = pltpu.roll(ext, 2, 0)[SUBLANES:SUBLANES + t]
    c = b + u2 * w[0:1]
    c = c + u1 * w[1:2]
    return c + u * w[2:3]


def _sigmoid(x):
    return 0.5 * jnp.tanh(0.5 * x) + 0.5


def _silu(x):
    half = 0.5 * x
    return half + half * jnp.tanh(half)


SIDE_FF_CHUNK = 256


SIDE_HBM_WEIGHTS = ("w_in", "w_ao", "w_po", "w_out", "w_up", "w_down")
SIDE_HBM_ARGS = (12, 14, 17, 18, 20, 23)


def _side_kernel(xs_ref, ck_ref, cv_ref, cmk_ref, cmv_ref, spool_ref, sconv_ref, cos_ref, sin_ref, tabm_ref,
                 tabs_ref, gmix_ref, w_in_hbm, bgate_ref, w_ao_hbm, wgrp_ref, pscale_ref, w_po_hbm, w_out_hbm,
                 gffn_ref, w_up_hbm, convw_ref, convb_ref, w_down_hbm, gfin_ref,
                 q0_ref, k0_ref, v0_ref, kaug_ref, qaug_ref, p0_ref, g0_ref, w_pool_ref,
                 y_ref, k_ref, v_ref, p_ref, up_ref, o0_ref, km_ref, vm_ref, pm_ref, upm_ref, mix0_ref,
                 w_in_ref, w_ao_ref, w_po_ref, w_out_ref, w_up_ref, w_down_ref, dma_sems):
    hbm = (w_in_hbm, w_ao_hbm, w_po_hbm, w_out_hbm, w_up_hbm, w_down_hbm)
    vmem = (w_in_ref, w_ao_ref, w_po_ref, w_out_ref, w_up_ref, w_down_ref)
    copies = {name: pltpu.make_async_copy(src, dst, dma_sems.at[n])
              for n, (name, src, dst) in enumerate(zip(SIDE_HBM_WEIGHTS, hbm, vmem))}
    for name in SIDE_HBM_WEIGHTS:
        copies[name].start()

    n_side = xs_ref.shape[0]
    dec_b = ck_ref.shape[0]
    t_dec = (n_side - N_META) // dec_b
    x = xs_ref[...]
    h = _rmsnorm(x, gmix_ref[...]).astype(BF16)
    cos, sin = cos_ref[...], sin_ref[...]
    copies["w_in"].wait()
    w_in = lambda c0, c1: w_in_ref[:, c0:c1].astype(BF16)
    q = _rope(_dot(h, w_in(0, Q_W)) * (HEAD_DIM ** -0.5), cos, sin).astype(BF16)
    k = _rope(_dot(h, w_in(K_OFF, K_OFF + KV_W)), cos, sin)
    v = _dot(h, w_in(V_OFF, V_OFF + KV_W))
    p = _dot(h, w_in(P_OFF, P_OFF + POOL_WIDTH))
    gates = _sigmoid(jnp.concatenate([_dot(h, w_in(G_OFF + c0, G_OFF + c0 + GATE_CHUNK))
                                      for c0 in range(0, GATE_W, GATE_CHUNK)], axis=1) + bgate_ref[...])
    k_ref[...] = k
    v_ref[...] = v
    p_ref[...] = p
    km_ref[...] = k[:N_META]
    vm_ref[...] = v[:N_META]
    pm_ref[...] = p[:N_META]

    def attend_rows(r0, nrows, kx, vx, tab_ref):
        kvar, vvar = _kv_variants(kx), _kv_variants(vx)
        blocks = [None] * Q_BLOCKS
        for n in range(N_KV_HEADS):
            qg = jnp.concatenate([q[r0:r0 + nrows, (BLOCKS_PER_KV * n + bi) * LANES:(BLOCKS_PER_KV * n + bi + 1) * LANES]
                                  for bi in range(BLOCKS_PER_KV)], axis=0)
            kst = jnp.concatenate(kvar[n], axis=0)
            vst = jnp.concatenate(vvar[n], axis=0)
            o = _attend_group(qg, kst, vst, tab_ref[n])
            for bi in range(BLOCKS_PER_KV):
                blocks[BLOCKS_PER_KV * n + bi] = o[bi * nrows:(bi + 1) * nrows]
        return jnp.concatenate(blocks, axis=1)

    zpad = jnp.zeros((LANES - N_META, LANES), F32)
    o_rows = [attend_rows(0, N_META, jnp.concatenate([k[:N_META], zpad], axis=0),
                          jnp.concatenate([v[:N_META], zpad], axis=0), tabm_ref)]
    n_keys = N_META + WINDOW + t_dec
    zpad = jnp.zeros((2 * LANES - n_keys, LANES), F32)
    for b in range(dec_b):
        r0 = N_META + b * t_dec
        kx = jnp.concatenate([cmk_ref[b], ck_ref[b], k[r0:r0 + t_dec], zpad], axis=0)
        vx = jnp.concatenate([cmv_ref[b], cv_ref[b], v[r0:r0 + t_dec], zpad], axis=0)
        o_rows.append(attend_rows(r0, t_dec, kx, vx, tabs_ref))
    o_attn = jnp.concatenate(o_rows, axis=0).astype(BF16)

    row = lax.broadcasted_iota(jnp.int32, (N_META, LANES), 0)
    pm_rows = []
    for seg in range(1 + dec_b):
        if seg == 0:
            ext = jnp.concatenate([jnp.zeros((N_META, POOL_WIDTH), F32), p[:N_META]], axis=0)
        else:
            r0 = N_META + (seg - 1) * t_dec
            ext = jnp.concatenate([spool_ref[seg - 1], p[r0:r0 + t_dec]], axis=0)
        groups = []
        for g in range(N_POOL_GROUPS):
            cnt = jnp.minimum(POOL_WINDOWS[g], row + 1).astype(F32) if seg == 0 else None
            groups.append(_pool_group(ext[:, g * POOL_GROUP_W:(g + 1) * POOL_GROUP_W], g, cnt))
        pm_rows.append(jnp.concatenate(groups, axis=1))
    pm = jnp.concatenate(pm_rows, axis=0)

    pool = _pool_project(pm, wgrp_ref, pscale_ref[...])
    for name in ("w_ao", "w_po", "w_out"):
        copies[name].wait()
    mix = gates[:, :D_MODEL] * _dot(o_attn, w_ao_ref[...]) + gates[:, D_MODEL:] * _dot(pool, w_po_ref[...])
    x1 = x + _dot(mix.astype(BF16), w_out_ref[...])

    hn = _rmsnorm(x1, gffn_ref[...]).astype(BF16)
    acc = jnp.zeros((n_side, D_MODEL), F32)
    zprev = jnp.zeros((SUBLANES, SIDE_FF_CHUNK), F32)
    copies["w_up"].wait()
    copies["w_down"].wait()
    for c0 in range(0, D_FF, SIDE_FF_CHUNK):
        halves = []
        for off in (c0, D_FF + c0):
            cs = slice(off, off + SIDE_FF_CHUNK)
            u = _dot(hn, w_up_ref[:, cs])
            up_ref[:, cs] = u
            w, bias = convw_ref[:, cs], convb_ref[:, cs]
            segs = [_conv3(u[:N_META], zprev, w, bias)]
            for b in range(dec_b):
                r0 = N_META + b * t_dec
                segs.append(_conv3(u[r0:r0 + t_dec], sconv_ref[b, :, cs], w, bias))
            halves.append(jnp.concatenate(segs, axis=0))
        act = (_silu(halves[0]) * halves[1]).astype(BF16)
        acc = acc + _dot(act, w_down_ref[c0:c0 + SIDE_FF_CHUNK, :])
    y_ref[...] = _rmsnorm(x1 + acc, gfin_ref[...])
    upm_ref[...] = up_ref[N_META - SUBLANES:N_META]

    def store_o0(j, blk, value):
        o0_ref[j * CHUNK:(j + 1) * CHUNK, blk * LANES:(blk + 1) * LANES] = value

    groups, scores, finish = _attention_stages(0, q0_ref, k0_ref, k0_ref.at[0:WINDOW], v0_ref, v0_ref.at[0:WINDOW],
                                               k_ref.at[0:N_META], v_ref.at[0:N_META], kaug_ref, qaug_ref, store_o0)
    pending = scores(*groups[0])
    for g, group in enumerate(groups):
        current = pending
        if g + 1 < len(groups):
            pending = scores(*groups[g + 1])
        finish(*group, *current)

    ext = jnp.concatenate([p[:N_META], p0_ref[...]], axis=0)
    for r0 in range(0, TQ, POST_SUB):
        rows = slice(r0, r0 + POST_SUB)
        pm0 = jnp.concatenate([_pool_group(ext[r0:r0 + POST_SUB + N_META, g * POOL_GROUP_W:(g + 1) * POOL_GROUP_W], g)
                               for g in range(N_POOL_GROUPS)], axis=1)
        mix0 = (g0_ref[rows, :D_MODEL].astype(F32) * _dot(o0_ref[rows], w_ao_ref[...])
                + g0_ref[rows, D_MODEL:].astype(F32) * _dot(pm0.astype(BF16), w_pool_ref[...]))
        mix0_ref[rows] = mix0.astype(BF16)


def _side_call(args, q, k, v, kaug, qaug, p, gates, w_pool):
    n_side = args[0].shape[0]
    out_shape = (
        jax.ShapeDtypeStruct((n_side, D_MODEL), F32),
        jax.ShapeDtypeStruct((n_side, KV_W), F32),
        jax.ShapeDtypeStruct((n_side, KV_W), F32),
        jax.ShapeDtypeStruct((n_side, POOL_WIDTH), F32),
        jax.ShapeDtypeStruct((n_side, 2 * D_FF), F32),
        jax.ShapeDtypeStruct((TQ, Q_W), BF16),
        jax.ShapeDtypeStruct((N_META, KV_W), F32),
        jax.ShapeDtypeStruct((N_META, KV_W), F32),
        jax.ShapeDtypeStruct((N_META, POOL_WIDTH), F32),
        jax.ShapeDtypeStruct((SUBLANES, 2 * D_FF), F32),
        jax.ShapeDtypeStruct((TQ, D_MODEL), BF16),
    )
    first_tile = lambda w: pl.BlockSpec((None, TQ, w), lambda i: (0, 0, 0))
    hbm_weights = [args[n] for n in SIDE_HBM_ARGS]
    assert [w.shape for w in hbm_weights] == [(D_MODEL, IN_W), (Q_W, D_MODEL), (POOL_WIDTH, D_MODEL),
                                              (D_MODEL, D_MODEL), (D_MODEL, 2 * D_FF), (D_FF, D_MODEL)]
    arg_specs = [pl.BlockSpec(memory_space=pl.ANY) if n in SIDE_HBM_ARGS else _const_spec(a.shape)
                 for n, a in enumerate(args)]
    return pl.pallas_call(
        _side_kernel,
        grid=(1,),
        in_specs=arg_specs + [first_tile(Q_W), first_tile(KV_W), first_tile(KV_W),
                              _const_spec(kaug.shape), _const_spec(qaug.shape),
                              first_tile(POOL_WIDTH), first_tile(GATE_W), _const_spec(w_pool.shape)],
        out_specs=tuple(_const_spec(s.shape) for s in out_shape),
        out_shape=out_shape,
        scratch_shapes=[pltpu.VMEM(w.shape, w.dtype) for w in hbm_weights]
                       + [pltpu.SemaphoreType.DMA((len(hbm_weights),))],
        compiler_params=pltpu.CompilerParams(dimension_semantics=("arbitrary",), vmem_limit_bytes=VMEM_LIMIT),
        name="side_rows",
    )(*args, q, k, v, kaug, qaug, p, gates, w_pool)


TM_IN = 1024
IN_SUB = 256
GATE_CHUNK = 512


W_IN_CAST_CHUNK = 768


def _inproj_kernel(n_cast, x_ref, cos_ref, sin_ref, gmix_ref, w_in_f32_ref, bgate_ref, *refs):
    cast_in, (q_ref, k_ref, v_ref, p_ref, g_ref) = refs[:n_cast], refs[n_cast:n_cast + 5]
    cast_out, w_in_ref = refs[n_cast + 5:2 * n_cast + 5], refs[2 * n_cast + 5]

    @pl.when((pl.program_id(0) == 0) & (pl.program_id(1) == 0))
    def _():
        for c0 in range(0, IN_W, W_IN_CAST_CHUNK):
            w_in_ref[:, c0:c0 + W_IN_CAST_CHUNK] = w_in_f32_ref[:, c0:c0 + W_IN_CAST_CHUNK].astype(BF16)

    for src, dst in zip(cast_in, cast_out):
        dst[...] = src[...].astype(BF16)

    subs = [slice(r0, r0 + IN_SUB) for r0 in range(0, x_ref.shape[0], IN_SUB)]
    hs = [_rmsnorm(x_ref[rows], gmix_ref[...]).astype(BF16) for rows in subs]
    for rows, h in zip(subs, hs):
        cos, sin = cos_ref[rows], sin_ref[rows]
        for c0 in range(0, GATE_W, GATE_CHUNK):
            z = _dot(h, w_in_ref[:, G_OFF + c0:G_OFF + c0 + GATE_CHUNK]) + bgate_ref[:, c0:c0 + GATE_CHUNK]
            g_ref[rows, c0:c0 + GATE_CHUNK] = _sigmoid(z).astype(BF16)
        q_ref[rows] = _rope(_dot(h, w_in_ref[:, 0:Q_W]) * (HEAD_DIM ** -0.5 * LOG2E), cos, sin).astype(BF16)
        p_ref[rows] = _dot(h, w_in_ref[:, P_OFF:P_OFF + POOL_WIDTH])
        kv = _dot(h, w_in_ref[:, K_OFF:K_OFF + 2 * KV_W])
        k_ref[rows] = _rope(kv[:, :KV_W], cos, sin)
        v_ref[rows] = kv[:, KV_W:]


def _const_spec(shape):
    nd = len(shape)
    return pl.BlockSpec(shape, lambda *_: (0,) * nd)


BF16_SUBLANES = 16


def _inproj_call(x, cos, sin, gmix, w_in, bgate, other_weights):
    bsz, seq, _ = x.shape
    tm = TM_IN
    tiles = seq // tm
    n_steps = bsz * tiles
    row = lambda w: pl.BlockSpec((None, tm, w), lambda b, i: (b, i, 0))

    def slab_spec(w):
        n_slabs = n_steps
        while w.shape[0] % (n_slabs * BF16_SUBLANES):
            n_slabs //= 2
        return pl.BlockSpec((w.shape[0] // n_slabs, w.shape[1]),
                            lambda b, i: (jnp.minimum(b * tiles + i, n_slabs - 1), 0))

    slabs = [slab_spec(w) for w in other_weights]
    outs = pl.pallas_call(
        functools.partial(_inproj_kernel, len(other_weights)),
        grid=(bsz, tiles),
        in_specs=[row(D_MODEL), pl.BlockSpec((tm, LANES), lambda b, i: (i, 0)),
                  pl.BlockSpec((tm, LANES), lambda b, i: (i, 0)),
                  _const_spec(gmix.shape), _const_spec(w_in.shape), _const_spec(bgate.shape)] + slabs,
        out_specs=(row(Q_W), row(KV_W), row(KV_W), row(POOL_WIDTH), row(GATE_W), *slabs),
        out_shape=(jax.ShapeDtypeStruct((bsz, seq, Q_W), BF16), jax.ShapeDtypeStruct((bsz, seq, KV_W), F32),
                   jax.ShapeDtypeStruct((bsz, seq, KV_W), F32), jax.ShapeDtypeStruct((bsz, seq, POOL_WIDTH), F32),
                   jax.ShapeDtypeStruct((bsz, seq, GATE_W), BF16),
                   *[jax.ShapeDtypeStruct(w.shape, BF16) for w in other_weights]),
        scratch_shapes=[pltpu.VMEM(w_in.shape, BF16)],
        compiler_params=pltpu.CompilerParams(dimension_semantics=("arbitrary", "arbitrary"),
                                             vmem_limit_bytes=VMEM_LIMIT),
        name="in_proj",
    )(x, cos, sin, gmix, w_in, bgate, *other_weights)
    return outs[:5], outs[5:]


TQ = 512
BAND = WINDOW + CHUNK
KEYS_PAD = BAND + N_META + 16
N_TABS = WINDOW // CHUNK + 1
SUM_ROWS = 16


def _attention_stages(i, q_ref, kc_ref, kp_ref, vc_ref, vp_ref, mk_ref, mv_ref, kaug_ref, qaug_ref, store):
    kvar = _kv_variants(jnp.concatenate([kp_ref[...], kc_ref[...]], axis=0), BF16)
    vvar = _kv_variants(jnp.concatenate([vp_ref[...], vc_ref[...]], axis=0), F32)
    mkvar, mvvar = _kv_variants(mk_ref[...], BF16), _kv_variants(mv_ref[...], F32)
    kpad = jnp.zeros((KEYS_PAD - BAND - N_META, LANES), BF16)
    vpad = jnp.zeros((KEYS_PAD - BAND - N_META, LANES), F32)
    row = lax.broadcasted_iota(jnp.int32, (SUM_ROWS, 2 * KEYS_PAD), 0)
    col = lax.broadcasted_iota(jnp.int32, (SUM_ROWS, 2 * KEYS_PAD), 1)
    ones = jnp.where((row == 0) & (col < KEYS_PAD) | (row == 1) & (col >= KEYS_PAD), 1.0, 0.0).astype(BF16)

    def scores(j, n):
        tsel = jnp.minimum(i * (TQ // CHUNK) + j, N_TABS - 1)
        rows = slice(j * CHUNK, (j + 1) * CHUNK)
        band = slice(j * CHUNK, j * CHUNK + BAND)
        kst = jnp.concatenate([kvar[n][0][band], mkvar[n][0], kpad, kvar[n][1][band], mkvar[n][1], kpad], axis=0)
        vst = jnp.concatenate([vvar[n][0][band], mvvar[n][0], vpad, vvar[n][1][band], mvvar[n][1], vpad], axis=0)
        qg = jnp.concatenate([q_ref[rows, (BLOCKS_PER_KV * n + bi) * LANES:(BLOCKS_PER_KV * n + bi + 1) * LANES]
                              for bi in range(BLOCKS_PER_KV)], axis=0)
        s = lax.dot_general(jnp.concatenate([kst, kaug_ref[tsel]], axis=1),
                            jnp.concatenate([qg, qaug_ref[n]], axis=1),
                            (((1,), (1,)), ((), ())), preferred_element_type=F32)
        vt = jnp.concatenate([vst.T.astype(BF16), ones], axis=0)
        return s, vt

    def finish(j, n, s, vt):
        se, so = s[:KEYS_PAD], s[KEYS_PAD:]
        pe = jnp.exp2(se - jnp.max(se, axis=0, keepdims=True))
        po = jnp.exp2(so - jnp.max(so, axis=0, keepdims=True))
        o = _dot(vt, jnp.concatenate([pe, po], axis=0).astype(BF16))
        o = jnp.concatenate([o[:HEAD_DIM] / o[LANES:LANES + 1], o[HEAD_DIM:LANES] / o[LANES + 1:LANES + 2]], axis=0).T
        for bi in range(BLOCKS_PER_KV):
            store(j, BLOCKS_PER_KV * n + bi, o[bi * CHUNK:(bi + 1) * CHUNK].astype(BF16))

    groups = [(j, n) for j in range(TQ // CHUNK) for n in range(N_KV_HEADS)]
    return groups, scores, finish


def _attn_aug_tables(sinks):
    n_keys = BAND + N_META
    r = np.arange(KEYS_PAD)
    lane = np.arange(LANES)[None, :]
    kaug = []
    for c in range(N_TABS):
        masked = (r < (WINDOW // CHUNK - c) * CHUNK) | (r > n_keys)
        mask_col = np.where(masked, NEG, 0.0)[:, None] * (lane == 0)
        sink_row = (r == n_keys)[:, None]
        even = mask_col + np.where(sink_row & ((lane == 1) | (lane == 2)), 1.0, 0.0)
        odd = mask_col + np.where(sink_row & ((lane == 3) | (lane == 4)), 1.0, 0.0)
        kaug.append(np.concatenate([even, odd], axis=0))
    s2 = sinks.astype(F32) * LOG2E
    hi = s2.astype(BF16).astype(F32)
    lo = s2 - hi
    head = 2 * jnp.arange(Q_BLOCKS)
    cols = [jnp.ones((Q_BLOCKS,), F32), hi[head], lo[head], hi[head + 1], lo[head + 1]]
    qaug = jnp.stack(cols, axis=1)
    qaug = jnp.pad(qaug, ((0, 0), (0, LANES - qaug.shape[1])))
    qaug = jnp.broadcast_to(qaug[:, None, :], (Q_BLOCKS, CHUNK, LANES)).reshape(N_KV_HEADS, BLOCKS_PER_KV * CHUNK, LANES)
    return jnp.asarray(np.stack(kaug), dtype=BF16), qaug.astype(BF16)


TM_POST = 512
POST_SUB = 256
FF_CHUNK = 256
FF_SKEW = 3
FF_LANE_BLOCKS = FF_CHUNK // LANES
DOWN_GROUP_ENDS = (4, 8, 10, 11)
assert DOWN_GROUP_ENDS[-1] == D_FF // FF_CHUNK


def _pool_weight_kernel(wgrp_ref, pscale_ref, w_po_ref, out_ref):
    for g in range(N_POOL_GROUPS):
        sl = slice(g * POOL_GROUP_W, (g + 1) * POOL_GROUP_W)
        out_ref[sl, :] = _dot((wgrp_ref[g] * pscale_ref[:, sl]).astype(BF16), w_po_ref[sl, :]).astype(BF16)


def _pool_weight_call(wgrp, pscale, w_po):
    return pl.pallas_call(
        _pool_weight_kernel,
        out_shape=jax.ShapeDtypeStruct((POOL_WIDTH, D_MODEL), BF16),
        name="pool_weights",
    )(wgrp, pscale, w_po)


def _stream_kernel(tiles_per_batch, n_tiles,
                   q_ref, kc_ref, kp_ref, vc_ref, vp_ref, mk_ref, mv_ref, kaug_ref, qaug_ref,
                   x_ref, p_ref, pprev_ref, pmeta_ref, g_ref, upmeta_ref, w_ao_ref, w_pool_ref,
                   w_out_ref, gffn_ref, w_up_ref, convw_ref, convb_ref, w_down_ref, gfin_ref,
                   mix0_ref, y_ref, carry_ref, oattn_ref, mix_ref, pext_ref, x1_ref, hn_ref, act_ref, *uext_refs):
    t = pl.program_id(0)
    i_next = lax.rem(jnp.minimum(t + 1, n_tiles - 1), tiles_per_batch)
    i = lax.rem(t, tiles_per_batch)
    tm = x_ref.shape[0]
    n_chunks = D_FF // FF_CHUNK

    @pl.when(t == 0)
    def _():
        mix_ref[...] = mix0_ref[...]

    @pl.when(i_next == 0)
    def _():
        pext_ref[0:N_META] = pmeta_ref[...]

    @pl.when(i_next > 0)
    def _():
        pext_ref[0:N_META] = pprev_ref[...]

    @pl.when(i == 0)
    def _():
        for c in range(n_chunks):
            for half, off in enumerate((c * FF_CHUNK, D_FF + c * FF_CHUNK)):
                for lb in range(FF_LANE_BLOCKS):
                    uext_refs[c][half * FF_LANE_BLOCKS + lb, POST_SUB:POST_SUB + SUBLANES] = (
                        upmeta_ref[:, off + lb * LANES:off + (lb + 1) * LANES])

    def store_attn(j, blk, value):
        oattn_ref[j * CHUNK:(j + 1) * CHUNK, blk * LANES:(blk + 1) * LANES] = value

    groups, scores, finish = _attention_stages(i_next, q_ref, kc_ref, kp_ref, vc_ref, vp_ref, mk_ref, mv_ref,
                                               kaug_ref, qaug_ref, store_attn)
    pending = [None]

    def attention_group(g):
        current = pending[0]
        if g + 1 < len(groups):
            pending[0] = scores(*groups[g + 1])
        finish(*groups[g], *current)

    pext_ref[N_META:] = p_ref[...]
    subs = [slice(r0, r0 + POST_SUB) for r0 in range(0, tm, POST_SUB)]

    for rows in subs:
        x1 = x_ref[rows] + _dot(mix_ref[rows], w_out_ref[...])
        x1_ref[rows] = x1
        hn_ref[rows] = _rmsnorm(x1, gffn_ref[...]).astype(BF16)

    def up_project(rows, c):
        for half, off in enumerate((c * FF_CHUNK, D_FF + c * FF_CHUNK)):
            u = _dot(hn_ref[rows], w_up_ref[:, off:off + FF_CHUNK])
            for lb in range(FF_LANE_BLOCKS):
                blk = uext_refs[c].at[half * FF_LANE_BLOCKS + lb]
                blk[0:SUBLANES] = blk[POST_SUB:POST_SUB + SUBLANES]
                blk[SUBLANES:] = u[:, lb * LANES:(lb + 1) * LANES]
            if rows.stop == tm:
                carry_ref[:, off:off + FF_CHUNK] = u[POST_SUB - SUBLANES:]

    def conv(rows, c, half):
        outs = []
        for lb in range(FF_LANE_BLOCKS):
            off = half * D_FF + c * FF_CHUNK + lb * LANES
            w, b = convw_ref[:, off:off + LANES], convb_ref[:, off:off + LANES]
            ext = uext_refs[c].at[half * FF_LANE_BLOCKS + lb]
            cv = b + ext[SUBLANES - 2:SUBLANES - 2 + POST_SUB] * w[0:1]
            cv = cv + ext[SUBLANES - 1:SUBLANES - 1 + POST_SUB] * w[1:2]
            outs.append(cv + ext[SUBLANES:] * w[2:3])
        return jnp.concatenate(outs, axis=1)

    steps = [(rows, c) for rows in subs for c in range(n_chunks)]
    for step in steps[:FF_SKEW]:
        up_project(*step)
    pending[0] = scores(*groups[0])
    acc = None
    for j, (rows, c) in enumerate(steps):
        act_ref[rows, c * FF_CHUNK:(c + 1) * FF_CHUNK] = (_silu(conv(rows, c, 0)) * conv(rows, c, 1)).astype(BF16)
        if j + FF_SKEW < len(steps):
            up_project(*steps[j + FF_SKEW])
        if c + 1 in DOWN_GROUP_ENDS:
            first = ([0] + list(DOWN_GROUP_ENDS))[DOWN_GROUP_ENDS.index(c + 1)]
            ks = slice(first * FF_CHUNK, (c + 1) * FF_CHUNK)
            d = _dot(act_ref[rows, ks], w_down_ref[ks, :])
            acc = d if first == 0 else acc + d
        if c == n_chunks - 1:
            y_ref[rows] = _rmsnorm(x1_ref[rows] + acc, gfin_ref[...])
        if j < len(groups):
            attention_group(j)
    assert len(groups) <= len(steps)

    for rows in subs:
        attn = _dot(oattn_ref[rows], w_ao_ref[...])
        ext_rows = slice(rows.start, rows.stop + N_META)
        pm = jnp.concatenate([_pool_group(pext_ref[ext_rows, g * POOL_GROUP_W:(g + 1) * POOL_GROUP_W], g)
                              for g in range(N_POOL_GROUPS)], axis=1)
        pool = _dot(pm.astype(BF16), w_pool_ref[...])
        mix = g_ref[rows, :D_MODEL].astype(F32) * attn + g_ref[rows, D_MODEL:].astype(F32) * pool
        mix_ref[rows] = mix.astype(BF16)


def _stream_call(q, k, v, mk, mv, kaug, qaug, x, p, p_meta, gates, o0, up_meta8, w_ao, w_pool, w_out, gffn, w_up,
                 convw, convb, w_down, gfin):
    bsz, seq, _ = x.shape
    tm = TM_POST
    assert tm == TQ
    tiles_per_batch = seq // tm
    n_tiles = bsz * tiles_per_batch

    def new_tile(t):
        a = jnp.minimum(t + 1, n_tiles - 1)
        return a // tiles_per_batch, lax.rem(a, tiles_per_batch)

    def old_tile(t):
        return t // tiles_per_batch, lax.rem(t, tiles_per_batch)

    new_row = lambda w: pl.BlockSpec((None, tm, w), lambda t: (*new_tile(t), 0))
    old_row = lambda w: pl.BlockSpec((None, tm, w), lambda t: (*old_tile(t), 0))

    def kv_prev_map(t):
        b, i = new_tile(t)
        return b, jnp.maximum(i * (tm // WINDOW) - 1, 0), 0

    def p_prev_map(t):
        b, i = new_tile(t)
        return b, jnp.maximum(i * (tm // N_META) - 1, 0), 0

    kv_prev = pl.BlockSpec((None, WINDOW, KV_W), kv_prev_map)
    consts = (up_meta8, w_ao, w_pool, w_out, gffn, w_up, convw, convb, w_down, gfin, o0)
    return pl.pallas_call(
        functools.partial(_stream_kernel, tiles_per_batch, n_tiles),
        grid=(n_tiles,),
        in_specs=[new_row(Q_W), new_row(KV_W), kv_prev, new_row(KV_W), kv_prev, _const_spec(mk.shape),
                  _const_spec(mv.shape), _const_spec(kaug.shape), _const_spec(qaug.shape),
                  old_row(D_MODEL), new_row(POOL_WIDTH), pl.BlockSpec((None, N_META, POOL_WIDTH), p_prev_map),
                  _const_spec(p_meta.shape), new_row(GATE_W)] + [_const_spec(a.shape) for a in consts],
        out_specs=(old_row(D_MODEL), pl.BlockSpec((None, SUBLANES, 2 * D_FF), lambda t: (old_tile(t)[0], 0, 0))),
        out_shape=(jax.ShapeDtypeStruct((bsz, seq, D_MODEL), F32),
                   jax.ShapeDtypeStruct((bsz, SUBLANES, 2 * D_FF), F32)),
        scratch_shapes=[pltpu.VMEM((tm, Q_W), BF16), pltpu.VMEM((tm, D_MODEL), BF16),
                        pltpu.VMEM((N_META + tm, POOL_WIDTH), F32),
                        pltpu.VMEM((tm, D_MODEL), F32), pltpu.VMEM((tm, D_MODEL), BF16),
                        pltpu.VMEM((tm, D_FF), BF16)]
                       + [pltpu.VMEM((2 * FF_LANE_BLOCKS, SUBLANES + POST_SUB, LANES), F32)
                          for _ in range(D_FF // FF_CHUNK)],
        compiler_params=pltpu.CompilerParams(dimension_semantics=("arbitrary",), vmem_limit_bytes=VMEM_LIMIT),
        name="attn_mixer_ffn",
    )(q, k, k, v, v, mk, mv, kaug, qaug, x, p, p, p_meta, gates, *consts)


def _rope_tables(pos):
    half = HEAD_DIM // 2
    inv = ROPE_THETA ** (-np.arange(half, dtype=np.float64) / half)
    ang = np.asarray(pos, dtype=np.float64)[:, None] * inv[None, :]
    cos, sin = np.cos(ang), np.sin(ang)
    cos = np.tile(cos, (1, LANES // half))
    sin = np.tile(np.concatenate([-sin, sin], axis=1), (1, LANES // HEAD_DIM))
    return jnp.asarray(cos, dtype=F32), jnp.asarray(sin, dtype=F32)


def _score_table(sinks, rows_per_block, n_keys, keys_pad, masked_prefix=0):
    col = jnp.arange(keys_pad)
    base = jnp.where((col >= masked_prefix) & (col < n_keys), 0.0, NEG).astype(F32)
    head = (2 * jnp.arange(Q_BLOCKS)[:, None] + jnp.arange(2)[None, :])
    tab = jnp.where(col[None, None, :] == n_keys, sinks.astype(F32)[head][:, :, None], base[None, None, :])
    tab = tab.reshape(N_KV_HEADS, BLOCKS_PER_KV, 1, 2 * keys_pad)
    tab = jnp.broadcast_to(tab, (N_KV_HEADS, BLOCKS_PER_KV, rows_per_block, 2 * keys_pad))
    return tab.reshape(N_KV_HEADS, BLOCKS_PER_KV * rows_per_block, 2 * keys_pad)


def kernel(x_prompt, x_sample, cache_swa_k, cache_swa_v, cache_meta_k, cache_meta_v, state_pool, state_conv,
           meta_tokens, g_norm_mix, w_in, b_gate, sinks, w_attn_o, w_pool_grp, pool_scale, w_pool_o, w_out,
           g_norm_ffn, w_up, conv_w, conv_b, w_down, g_norm_final):
    bsz, seq, _ = x_prompt.shape
    dbsz, t_dec, _ = x_sample.shape
    row2 = lambda a: a.reshape(1, -1)
    gmix, gffn, gfin = row2(g_norm_mix), row2(g_norm_ffn), row2(g_norm_final)
    bgate, pscale, convb = row2(b_gate), row2(pool_scale), row2(conv_b)

    cos_p, sin_p = _rope_tables(N_META + np.arange(seq))
    (q, k, v, p, gates), (w_ao_b, w_po_b, w_out_b, w_up_b, w_down_b) = _inproj_call(
        x_prompt, cos_p, sin_p, gmix, w_in, bgate, (w_attn_o, w_pool_o, w_out, w_up, w_down))

    pos_side = np.concatenate([np.arange(N_META), np.tile(N_META + PAST_LEN + np.arange(t_dec), dbsz)])
    cos_s, sin_s = _rope_tables(pos_side)
    xs = jnp.concatenate([meta_tokens, x_sample.reshape(dbsz * t_dec, D_MODEL)], axis=0)
    spool16 = jnp.pad(state_pool, ((0, 0), (N_META - POOL_HIST, 0), (0, 0)))
    sconv8 = jnp.pad(state_conv, ((0, 0), (SUBLANES - (CONV_W - 1), 0), (0, 0)))
    tab_meta = _score_table(sinks, N_META, N_META, LANES)
    tab_dec = _score_table(sinks, t_dec, N_META + WINDOW + t_dec, 2 * LANES)
    kaug, qaug = _attn_aug_tables(sinks)
    w_pool_b = _pool_weight_call(w_pool_grp, pscale, w_po_b)
    y_side, k_side, v_side, p_side, up_side, _, km, vm, p_meta, up_meta8, mix_first = _side_call(
        (xs, cache_swa_k.reshape(dbsz, WINDOW, KV_W), cache_swa_v.reshape(dbsz, WINDOW, KV_W),
         cache_meta_k.reshape(dbsz, N_META, KV_W), cache_meta_v.reshape(dbsz, N_META, KV_W), spool16, sconv8,
         cos_s, sin_s, tab_meta, tab_dec, gmix, w_in, bgate, w_ao_b, w_pool_grp, pscale, w_po_b, w_out_b,
         gffn, w_up_b, conv_w, convb, w_down_b, gfin), q, k, v, kaug, qaug, p, gates, w_pool_b)
    y_prompt, up_tail = _stream_call(q, k, v, km, vm, kaug, qaug, x_prompt, p, p_meta, gates, mix_first, up_meta8,
                                     w_ao_b, w_pool_b, w_out_b, gffn, w_up_b, conv_w, convb, w_down_b, gfin)

    kv4 = lambda a, n: a.reshape(a.shape[0], n, N_KV_HEADS, HEAD_DIM)
    dec = lambda a: a[N_META:].reshape(dbsz, t_dec, -1)
    return (
        y_prompt,
        dec(y_side),
        kv4(k[:, seq - WINDOW:], WINDOW),
        kv4(v[:, seq - WINDOW:], WINDOW),
        jnp.broadcast_to(km.reshape(1, N_META, N_KV_HEADS, HEAD_DIM), (bsz, N_META, N_KV_HEADS, HEAD_DIM)),
        jnp.broadcast_to(vm.reshape(1, N_META, N_KV_HEADS, HEAD_DIM), (bsz, N_META, N_KV_HEADS, HEAD_DIM)),
        p[:, seq - POOL_HIST:],
        up_tail[:, SUBLANES - (CONV_W - 1):],
        kv4(dec(k_side), t_dec),
        kv4(dec(v_side), t_dec),
        dec(p_side)[:, t_dec - POOL_HIST:],
        dec(up_side)[:, t_dec - (CONV_W - 1):],
    )
```
